```python
import math
import jax, jax.numpy as jnp
from jax import lax
import numpy as np

D_MODEL = 1024
BATCH = 8
SEQ = 2048
DEPTH = 2

N_META = 16
HEAD_DIM = 64
N_Q_HEADS = 8
N_KV_HEADS = 2
GROUP = N_Q_HEADS // N_KV_HEADS
ATTN_WIDTH = N_Q_HEADS * HEAD_DIM
KV_WIDTH = N_KV_HEADS * HEAD_DIM
WINDOW = 128
BLOCK = 128
CONV_CH = D_MODEL // 2
CONV_WIDTH = 31
N_BUCKETS = 32
MAX_DISTANCE = 128
D_FF = 2816
N_EXPERTS = 8
TOP_K = 2
D_EXPERT = 1408
N_DENSE = (DEPTH + 1) // 2
N_MOE = DEPTH // 2
ALPHA = (2 * DEPTH) ** 0.25
BETA = (8 * DEPTH) ** -0.25
LN_EPS = 1e-5
Q_END = ATTN_WIDTH
K_END = Q_END + KV_WIDTH
V_END = K_END + KV_WIDTH
GLU_END = V_END + 2 * CONV_CH
GA_END = GLU_END + D_MODEL
GC_END = GA_END + D_MODEL
IN_WIDTH = GC_END

kernel_name = "hybrid_swa_conformer_moe_deepnorm"


def layer_norm(x, g, b):
    xf = x.astype(jnp.float32)
    mu = jnp.mean(xf, -1, keepdims=True)
    var = jnp.mean(jnp.square(xf - mu), -1, keepdims=True)
    y = (xf - mu) * lax.rsqrt(var + LN_EPS) * g.astype(jnp.float32) + b.astype(jnp.float32)
    return y.astype(x.dtype)


def rel_bucket(dist):
    n = jnp.maximum(dist, 0)
    max_exact = N_BUCKETS // 2
    nf = jnp.maximum(n, 1).astype(jnp.float32)
    large = max_exact + (jnp.log(nf / max_exact) / math.log(MAX_DISTANCE / max_exact)
                         * (N_BUCKETS - max_exact)).astype(jnp.int32)
    large = jnp.minimum(large, N_BUCKETS - 1)
    return jnp.where(n < max_exact, n, large)


def head_bias(rel_bias, bucket):
    b = rel_bias.astype(jnp.float32)[bucket]
    b = jnp.moveaxis(b, -1, -3)
    return b.reshape(b.shape[:-3] + (N_KV_HEADS, GROUP) + b.shape[-2:])


def sink_softmax(s, sink):
    m = jnp.maximum(jnp.max(s, -1, keepdims=True), sink)
    p = jnp.exp(s - m)
    return p / (jnp.sum(p, -1, keepdims=True) + jnp.exp(sink - m))


def sliding_window_attention(q, k, v, sinks, rel_bias):
    B, L = q.shape[:2]
    S = L - N_META
    nb = S // BLOCK
    scale = HEAD_DIM ** -0.5
    sink = sinks.astype(jnp.float32).reshape(N_KV_HEADS, GROUP)[:, :, None, None]

    qm = q[:, :N_META].reshape(B, N_META, N_KV_HEADS, GROUP, HEAD_DIM)
    km, vm = k[:, :N_META], v[:, :N_META]

    pos_m = jnp.arange(N_META)
    dist_mm = pos_m[:, None] - pos_m[None, :]
    s_mm = jnp.einsum('bqhgd,bkhd->bhgqk', qm, km).astype(jnp.float32) * scale
    s_mm = s_mm + head_bias(rel_bias, rel_bucket(dist_mm))
    s_mm = jnp.where(dist_mm >= 0, s_mm, -jnp.inf)
    p_mm = sink_softmax(s_mm, sink).astype(v.dtype)
    o_m = jnp.einsum('bhgqk,bkhd->bqhgd', p_mm, vm).reshape(B, N_META, ATTN_WIDTH)

    qr = q[:, N_META:].reshape(B, nb, BLOCK, N_KV_HEADS, GROUP, HEAD_DIM)
    kr = k[:, N_META:].reshape(B, nb, BLOCK, N_KV_HEADS, HEAD_DIM)
    vr = v[:, N_META:].reshape(B, nb, BLOCK, N_KV_HEADS, HEAD_DIM)
    zpad = jnp.zeros_like(kr[:, :1])
    kb = jnp.concatenate([jnp.concatenate([zpad, kr[:, :-1]], 1), kr], 2)
    vb = jnp.concatenate([jnp.concatenate([zpad, vr[:, :-1]], 1), vr], 2)

    a = jnp.arange(BLOCK)
    kk = jnp.arange(2 * BLOCK)
    blk = jnp.arange(nb)
    dist_band = BLOCK + a[:, None] - kk[None, :]
    mask_band = ((dist_band >= 0) & (dist_band < WINDOW))[None] & \
        ((blk[:, None, None] > 0) | (kk[None, None, :] >= BLOCK))
    q_pos = N_META + blk[:, None] * BLOCK + a[None, :]
    dist_meta = q_pos[:, :, None] - pos_m[None, None, :]

    s_band = jnp.einsum('bnqhgd,bnkhd->bnhgqk', qr, kb).astype(jnp.float32) * scale
    s_band = s_band + head_bias(rel_bias, rel_bucket(dist_band))
    s_band = jnp.where(mask_band[None, :, None, None], s_band, -jnp.inf)
    s_meta = jnp.einsum('bnqhgd,bmhd->bnhgqm', qr, km).astype(jnp.float32) * scale
    s_meta = s_meta + head_bias(rel_bias, rel_bucket(dist_meta))
    p = sink_softmax(jnp.concatenate([s_meta, s_band], -1), sink).astype(v.dtype)
    o_r = jnp.einsum('bnhgqm,bmhd->bnqhgd', p[..., :N_META], vm) + \
        jnp.einsum('bnhgqk,bnkhd->bnqhgd', p[..., N_META:], vb)
    o_r = o_r.reshape(B, S, ATTN_WIDTH)
    return jnp.concatenate([o_m, o_r], 1)


def conformer_conv(u, dw, db, g, b):
    val, gate = jnp.split(u, 2, axis=-1)
    h = val * jax.nn.sigmoid(gate)
    h = lax.conv_general_dilated(h, dw[:, None, :].astype(h.dtype), window_strides=(1,),
                                 padding=[(CONV_WIDTH - 1, 0)],
                                 dimension_numbers=('NWC', 'WIO', 'NWC'),
                                 feature_group_count=CONV_CH) + db
    h = layer_norm(h, g, b)
    return jax.nn.silu(h)


def swiglu(t, wg, wu, wd):
    return (jax.nn.silu(t @ wg) * (t @ wu)) @ wd


def moe_swiglu(h, router, wg, wu, wd):
    B, L, D = h.shape
    t = h.reshape(B * L, D)
    logits = (t @ router).astype(jnp.float32)
    vals, idx = lax.top_k(logits, TOP_K)
    gates = jax.nn.softmax(vals, axis=-1)
    combine = jnp.sum(jax.nn.one_hot(idx, N_EXPERTS, dtype=jnp.float32) * gates[..., None], 1)
    out = jnp.zeros_like(t)
    for e in range(N_EXPERTS):
        out = out + combine[:, e:e + 1].astype(t.dtype) * swiglu(t, wg[e], wu[e], wd[e])
    return out.reshape(B, L, D)


def setup_inputs(seed: int = 0) -> dict:
    key = jax.random.key(seed)
    ks = iter(jax.random.split(key, 32))
    n = lambda shape, s: jax.random.normal(next(ks), shape, jnp.float32) * s
    D = D_MODEL
    return {
        "x": n((BATCH, SEQ, D), 1.0),
        "meta_tokens": n((N_META, D), 1.0),
        "emb_ln_g": 1.0 + n((D,), 0.02),
        "emb_ln_b": n((D,), 0.02),
        "rel_bias": n((N_BUCKETS, N_Q_HEADS), 0.5),
        "w_in": n((DEPTH, D, IN_WIDTH), D ** -0.5),
        "conv_dw": n((DEPTH, CONV_WIDTH, CONV_CH), CONV_WIDTH ** -0.5),
        "conv_b": n((DEPTH, CONV_CH), 0.02),
        "conv_ln_g": 1.0 + n((DEPTH, CONV_CH), 0.02),
        "conv_ln_b": n((DEPTH, CONV_CH), 0.02),
        "sinks": n((DEPTH, N_Q_HEADS), 0.5),
        "w_attn_proj": n((DEPTH, ATTN_WIDTH, D), ATTN_WIDTH ** -0.5),
        "w_conv_proj": n((DEPTH, CONV_CH, D), CONV_CH ** -0.5),
        "w_out": n((DEPTH, D, D), BETA * D ** -0.5),
        "ln1_g": 1.0 + n((DEPTH, D), 0.02),
        "ln1_b": n((DEPTH, D), 0.02),
        "ffn_w_gate": n((N_DENSE, D, D_FF), D ** -0.5),
        "ffn_w_up": n((N_DENSE, D, D_FF), D ** -0.5),
        "ffn_w_down": n((N_DENSE, D_FF, D), BETA * D_FF ** -0.5),
        "router": n((N_MOE, D, N_EXPERTS), D ** -0.5),
        "moe_w_gate": n((N_MOE, N_EXPERTS, D, D_EXPERT), D ** -0.5),
        "moe_w_up": n((N_MOE, N_EXPERTS, D, D_EXPERT), D ** -0.5),
        "moe_w_down": n((N_MOE, N_EXPERTS, D_EXPERT, D), BETA * D_EXPERT ** -0.5),
        "ln2_g": 1.0 + n((DEPTH, D), 0.02),
        "ln2_b": n((DEPTH, D), 0.02),
    }


def reference(x, meta_tokens, emb_ln_g, emb_ln_b, rel_bias, w_in, conv_dw, conv_b,
              conv_ln_g, conv_ln_b, sinks, w_attn_proj, w_conv_proj, w_out, ln1_g, ln1_b,
              ffn_w_gate, ffn_w_up, ffn_w_down, router, moe_w_gate, moe_w_up, moe_w_down,
              ln2_g, ln2_b):
    B = x.shape[0]
    meta = jnp.broadcast_to(meta_tokens[None].astype(x.dtype), (B, N_META, D_MODEL))
    h = layer_norm(jnp.concatenate([meta, x], axis=1), emb_ln_g, emb_ln_b)
    L = h.shape[1]
    for i in range(DEPTH):
        u = h @ w_in[i]
        q = u[..., :Q_END].reshape(B, L, N_Q_HEADS, HEAD_DIM)
        k = u[..., Q_END:K_END].reshape(B, L, N_KV_HEADS, HEAD_DIM)
        v = u[..., K_END:V_END].reshape(B, L, N_KV_HEADS, HEAD_DIM)
        y_attn = sliding_window_attention(q, k, v, sinks[i], rel_bias) @ w_attn_proj[i]
        y_conv = conformer_conv(u[..., V_END:GLU_END], conv_dw[i], conv_b[i],
                                conv_ln_g[i], conv_ln_b[i]) @ w_conv_proj[i]
        mix = (jax.nn.sigmoid(u[..., GLU_END:GA_END]) * y_attn
               + jax.nn.sigmoid(u[..., GA_END:GC_END]) * y_conv) @ w_out[i]
        h = layer_norm(ALPHA * h + mix, ln1_g[i], ln1_b[i])
        j = i // 2
        if i % 2 == 0:
            f = swiglu(h, ffn_w_gate[j], ffn_w_up[j], ffn_w_down[j])
        else:
            f = moe_swiglu(h, router[j], moe_w_gate[j], moe_w_up[j], moe_w_down[j])
        h = layer_norm(ALPHA * h + f, ln2_g[i], ln2_b[i])
    return h[:, N_META:]
```

```python
import functools
import math

import numpy as np
import jax
import jax.numpy as jnp
from jax import lax
from jax.experimental import pallas as pl
from jax.experimental.pallas import tpu as pltpu

f32 = jnp.float32
bf16 = jnp.bfloat16
i32 = jnp.int32

D_MODEL = 1024
BATCH = 8
SEQ = 2048
DEPTH = 2
N_META = 16
HEAD_DIM = 64
N_Q_HEADS = 8
N_KV_HEADS = 2
GROUP = N_Q_HEADS // N_KV_HEADS
ATTN_WIDTH = N_Q_HEADS * HEAD_DIM
KV_WIDTH = N_KV_HEADS * HEAD_DIM
WINDOW = 128
BLOCK = 128
CONV_CH = D_MODEL // 2
CONV_WIDTH = 31
N_BUCKETS = 32
MAX_DISTANCE = 128
D_FF = 2816
N_EXPERTS = 8
D_EXPERT = 1408
ALPHA = (2 * DEPTH) ** 0.25
LN_EPS = 1e-5
Q_END = ATTN_WIDTH
K_END = Q_END + KV_WIDTH
V_END = K_END + KV_WIDTH
GLU_END = V_END + 2 * CONV_CH
GA_END = GLU_END + D_MODEL
GC_END = GA_END + D_MODEL

TM = 512
N_META_ROWS = BATCH * N_META
NT = TM + BATCH * SEQ
N_TILES = NT // TM
TILES_PER_BATCH = SEQ // TM
BLOCKS_PER_TILE = TM // BLOCK
KSEG = 3 * BLOCK
CONV_HALO = 32
NEG = -1e30

CHUNK = 256
ALIGN = 16
XS_W = D_MODEL + 128
CAP = 35 * TM
FFN_GRID = 85
FF_CHUNKS_DENSE = (256,) * 11
FF_CHUNKS_MOE = (256,) * 5 + (128,)

VMEM_LIMIT = 56 * 1024 * 1024


def _cparams(sem="arbitrary"):
    return pltpu.CompilerParams(dimension_semantics=(sem,), vmem_limit_bytes=VMEM_LIMIT)


def _ln(x, g, b):
    mu = jnp.mean(x, -1, keepdims=True)
    xc = x - mu
    var = jnp.mean(xc * xc, -1, keepdims=True)
    return xc * lax.rsqrt(var + LN_EPS) * g + b


def _full(shape):
    return pl.BlockSpec(shape, lambda *_: (0,) * len(shape))


def _smem():
    return pl.BlockSpec(memory_space=pltpu.SMEM)


def _embed_kernel(x_ref, meta_ref, g_ref, b_ref, o_ref):
    i = pl.program_id(0)

    @pl.when(i == 0)
    def _():
        m = _ln(meta_ref[...], g_ref[...], b_ref[...])
        for bb in range(BATCH):
            o_ref[bb * N_META:(bb + 1) * N_META, :] = m
        o_ref[N_META_ROWS:, :] = jnp.zeros((TM - N_META_ROWS, D_MODEL), f32)

    @pl.when(i > 0)
    def _():
        o_ref[...] = _ln(x_ref[...], g_ref[...], b_ref[...])


def _embed(x2d, meta, g, b):
    return pl.pallas_call(
        _embed_kernel,
        grid=(N_TILES,),
        in_specs=[
            pl.BlockSpec((TM, D_MODEL), lambda i: (jnp.maximum(i - 1, 0), 0)),
            _full((N_META, D_MODEL)),
            _full((1, D_MODEL)),
            _full((1, D_MODEL)),
        ],
        out_specs=pl.BlockSpec((TM, D_MODEL), lambda i: (i, 0)),
        out_shape=jax.ShapeDtypeStruct((NT, D_MODEL), f32),
        compiler_params=_cparams("arbitrary"),
        name="embed_ln",
    )(x2d, meta, g, b)


def _inproj_kernel(h_ref, w_ref, dw_ref, cb_ref, cg_ref, cbeta_ref,
                   q_ref, k_ref, v_ref, c_ref, win_ref, gm_ref, cv_ref):
    s = pl.program_id(0)
    hb = h_ref[...].astype(bf16)
    u = jnp.dot(hb, w_ref[...], preferred_element_type=f32)
    q_ref[...] = (u[:, :Q_END] * (HEAD_DIM ** -0.5)).astype(bf16)
    k_ref[...] = u[:, Q_END:K_END].astype(bf16)
    v_ref[...] = u[:, K_END:V_END].astype(bf16)
    glu = u[:, V_END:V_END + CONV_CH] * jax.nn.sigmoid(u[:, V_END + CONV_CH:GLU_END])
    win_ref[CONV_HALO:, :] = glu

    @pl.when(s == 0)
    def _():
        gm_ref[...] = glu[:N_META_ROWS]
        win_ref[:CONV_HALO, :] = jnp.zeros((CONV_HALO, CONV_CH), f32)
        pos = lax.broadcasted_iota(i32, (N_META_ROWS, 1), 0) % N_META
        for lc in range(CONV_CH // 128):
            ls = slice(lc * 128, (lc + 1) * 128)
            acc = jnp.zeros((N_META_ROWS, 128), f32)
            for d in range(N_META):
                tap = dw_ref[CONV_WIDTH - 1 - d:CONV_WIDTH - d, ls]
                xs = win_ref[CONV_HALO - d:CONV_HALO - d + N_META_ROWS, ls]
                acc = acc + jnp.where(pos >= d, tap * xs, 0.0)
            cv_ref[:N_META_ROWS, ls] = acc
        cv_ref[N_META_ROWS:, :] = jnp.zeros((TM - N_META_ROWS, CONV_CH), f32)

    @pl.when(s > 0)
    def _():
        j = (s - 1) % TILES_PER_BATCH
        b = (s - 1) // TILES_PER_BATCH

        @pl.when(j == 0)
        def _():
            win_ref[:CONV_HALO - N_META, :] = jnp.zeros((CONV_HALO - N_META, CONV_CH), f32)
            win_ref[CONV_HALO - N_META:CONV_HALO, :] = gm_ref[pl.ds(pl.multiple_of(b * N_META, N_META), N_META), :]

        off = CONV_HALO - (CONV_WIDTH - 1)
        for rc in range(TM // 128):
            for lc in range(CONV_CH // 128):
                ls = slice(lc * 128, (lc + 1) * 128)
                acc = jnp.zeros((128, 128), f32)
                for t in range(CONV_WIDTH):
                    r0 = rc * 128 + off + t
                    acc = acc + dw_ref[t:t + 1, ls] * win_ref[r0:r0 + 128, ls]
                cv_ref[rc * 128:(rc + 1) * 128, ls] = acc

    y = _ln(cv_ref[...] + cb_ref[...], cg_ref[...], cbeta_ref[...])
    c_ref[...] = (y * jax.nn.sigmoid(y)).astype(bf16)
    win_ref[:CONV_HALO, :] = win_ref[TM:TM + CONV_HALO, :]


def _inproj(h, w1, dw, cb, cg, cbeta):
    row = lambda w: pl.BlockSpec((TM, w), lambda s: (s, 0))
    return pl.pallas_call(
        _inproj_kernel,
        grid=(N_TILES,),
        in_specs=[
            row(D_MODEL),
            _full((D_MODEL, GLU_END)),
            _full((CONV_WIDTH, CONV_CH)),
            _full((1, CONV_CH)), _full((1, CONV_CH)), _full((1, CONV_CH)),
        ],
        out_specs=[row(ATTN_WIDTH), row(KV_WIDTH), row(KV_WIDTH), row(CONV_CH)],
        out_shape=[
            jax.ShapeDtypeStruct((NT, ATTN_WIDTH), bf16),
            jax.ShapeDtypeStruct((NT, KV_WIDTH), bf16),
            jax.ShapeDtypeStruct((NT, KV_WIDTH), bf16),
            jax.ShapeDtypeStruct((NT, CONV_CH), bf16),
        ],
        scratch_shapes=[
            pltpu.VMEM((CONV_HALO + TM, CONV_CH), f32),
            pltpu.VMEM((N_META_ROWS, CONV_CH), f32),
            pltpu.VMEM((TM, CONV_CH), f32),
        ],
        compiler_params=_cparams("arbitrary"),
        name="inproj_conv",
    )(h, w1, dw, cb, cg, cbeta)


def _rel_bucket_np(dist):
    n = np.maximum(dist, 0)
    max_exact = N_BUCKETS // 2
    nf = np.maximum(n, 1).astype(np.float32)
    scaled = (np.log(nf / np.float32(max_exact)) / np.float32(math.log(MAX_DISTANCE / max_exact))
              * np.float32(N_BUCKETS - max_exact))
    large = np.minimum(max_exact + scaled.astype(np.int32), N_BUCKETS - 1)
    return np.where(n < max_exact, n, large).astype(np.int32)


def _bucket_tables():
    a = np.arange(BLOCK)[:, None]
    m = np.arange(BLOCK)[None, :]
    kk = np.arange(2 * BLOCK)[None, :]
    dist_band = BLOCK + a - kk
    band_ok = (dist_band >= 0) & (dist_band < WINDOW)
    band = np.where(band_ok, _rel_bucket_np(dist_band), -1)
    real = np.full((2, BLOCK, KSEG), -1, np.int32)
    for t, n in enumerate((0, 1)):
        dist_meta = N_META + n * BLOCK + a - m
        real[t, :, :BLOCK] = np.where(m < N_META, _rel_bucket_np(dist_meta), -1)
        real[t, :, BLOCK:] = band
    real[0, :, BLOCK:2 * BLOCK] = -1
    for n in range(2, SEQ // BLOCK):
        dist_meta = N_META + n * BLOCK + a - m
        assert np.array_equal(np.where(m < N_META, _rel_bucket_np(dist_meta), -1), real[1, :, :BLOCK])
    r = np.arange(N_META_ROWS)
    same = (r[:, None] // N_META) == (r[None, :] // N_META)
    dmm = (r[:, None] % N_META) - (r[None, :] % N_META)
    meta = np.where(same & (dmm >= 0), _rel_bucket_np(dmm), -1).astype(np.int32)
    return real, meta


_BK_REAL, _BK_META = _bucket_tables()


def _bias_from_buckets(bk, rb_ref, h):
    acc = jnp.full(bk.shape, NEG, f32)
    for bkt in range(N_BUCKETS):
        acc = jnp.where(bk == bkt, rb_ref[bkt, h], acc)
    return acc


def _expand_kv(x2):
    xf = x2.astype(f32)
    r = pltpu.roll(xf, HEAD_DIM, axis=1)
    low = lax.broadcasted_iota(i32, xf.shape, 1) < HEAD_DIM
    h0 = jnp.where(low, xf, r).astype(bf16)
    h1 = jnp.where(low, r, xf).astype(bf16)
    return (jnp.concatenate([h0, h0], axis=1), jnp.concatenate([h1, h1], axis=1))


def _block_diag(x, n_keys):
    lane_blk = lax.broadcasted_iota(i32, (n_keys, GROUP * HEAD_DIM), 1) // HEAD_DIM
    zero = jnp.zeros_like(x)
    return jnp.concatenate([jnp.where(lane_blk == h, x, zero) for h in range(GROUP)], axis=0)


def _attend(qg, kx, vx, bias_of_head, sink_of_head, n_keys):
    m_rows = qg.shape[0]
    kbd = _block_diag(kx, n_keys)
    vbd = _block_diag(vx, n_keys)
    s = lax.dot_general(qg, kbd, (((1,), (1,)), ((), ())), preferred_element_type=f32)
    ps, rs = [], []
    for h in range(GROUP):
        sh = s[:, h * n_keys:(h + 1) * n_keys] + bias_of_head(h)
        sink = sink_of_head(h)
        mx = jnp.maximum(jnp.max(sh, -1, keepdims=True), sink)
        p = jnp.exp(sh - mx)
        den = jnp.sum(p, -1, keepdims=True) + jnp.exp(sink - mx)
        ps.append(p.astype(bf16))
        rs.append(1.0 / den)
    pm = jnp.concatenate(ps, axis=1)
    o = jnp.dot(pm, vbd, preferred_element_type=f32)
    ol = lax.broadcasted_iota(i32, (m_rows, GROUP * HEAD_DIM), 1) // HEAD_DIM
    scale = jnp.where(ol == 0, rs[0], jnp.where(ol == 1, rs[1], jnp.where(ol == 2, rs[2], rs[3])))
    return o * scale


def _attn_kernel(rb_ref, sink_ref, q_ref, k_ref, v_ref, kp_ref, vp_ref, km_ref, vm_ref,
                 bkr_ref, bkm_ref, o_ref, br_ref, bm_ref):
    s = pl.program_id(0)
    gw = GROUP * HEAD_DIM

    @pl.when(s == 0)
    def _():
        for h in range(N_Q_HEADS):
            for t in range(2):
                br_ref[t, h] = _bias_from_buckets(bkr_ref[t], rb_ref, h)
            bm_ref[h] = _bias_from_buckets(bkm_ref[...], rb_ref, h)
        kxs = _expand_kv(k_ref[:N_META_ROWS, :])
        vxs = _expand_kv(v_ref[:N_META_ROWS, :])
        for g in range(N_KV_HEADS):
            o = _attend(q_ref[:N_META_ROWS, g * gw:(g + 1) * gw], kxs[g], vxs[g],
                        lambda h, g=g: bm_ref[g * GROUP + h],
                        lambda h, g=g: sink_ref[g * GROUP + h], N_META_ROWS)
            o_ref[:N_META_ROWS, g * gw:(g + 1) * gw] = o.astype(bf16)
        o_ref[N_META_ROWS:, :] = jnp.zeros((TM - N_META_ROWS, ATTN_WIDTH), bf16)

    @pl.when(s > 0)
    def _():
        first = ((s - 1) % TILES_PER_BATCH) == 0
        tbl0 = jnp.where(first, 0, 1)
        zpad = jnp.zeros((BLOCK - N_META, KV_WIDTH), bf16)
        kmeta = jnp.concatenate([km_ref[...], zpad], axis=0)
        vmeta = jnp.concatenate([vm_ref[...], zpad], axis=0)
        for blk in range(BLOCKS_PER_TILE):
            rows = slice(blk * BLOCK, (blk + 1) * BLOCK)
            if blk == 0:
                kprev, vprev = kp_ref[...], vp_ref[...]
            else:
                prows = slice((blk - 1) * BLOCK, blk * BLOCK)
                kprev, vprev = k_ref[prows, :], v_ref[prows, :]
            kxs = _expand_kv(jnp.concatenate([kmeta, kprev, k_ref[rows, :]], axis=0))
            vxs = _expand_kv(jnp.concatenate([vmeta, vprev, v_ref[rows, :]], axis=0))
            for g in range(N_KV_HEADS):
                if blk == 0:
                    bias = lambda h, g=g: br_ref[tbl0, g * GROUP + h]
                else:
                    bias = lambda h, g=g: br_ref[1, g * GROUP + h]
                o = _attend(q_ref[rows, g * gw:(g + 1) * gw], kxs[g], vxs[g], bias,
                            lambda h, g=g: sink_ref[g * GROUP + h], KSEG)
                o_ref[rows, g * gw:(g + 1) * gw] = o.astype(bf16)


def _attention(rel_bias, sinks, q, k, v):
    row = lambda w: pl.BlockSpec((TM, w), lambda s: (s, 0))
    prev = pl.BlockSpec((BLOCK, KV_WIDTH), lambda s: (jnp.maximum(s * BLOCKS_PER_TILE - 1, 0), 0))
    meta = pl.BlockSpec((N_META, KV_WIDTH), lambda s: (jnp.maximum(s - 1, 0) // TILES_PER_BATCH, 0))
    return pl.pallas_call(
        _attn_kernel,
        grid=(N_TILES,),
        in_specs=[
            _smem(), _smem(),
            row(ATTN_WIDTH), row(KV_WIDTH), row(KV_WIDTH),
            prev, prev, meta, meta,
            _full((2, BLOCK, KSEG)), _full((N_META_ROWS, N_META_ROWS)),
        ],
        out_specs=row(ATTN_WIDTH),
        out_shape=jax.ShapeDtypeStruct((NT, ATTN_WIDTH), bf16),
        scratch_shapes=[
            pltpu.VMEM((2, N_Q_HEADS, BLOCK, KSEG), f32),
            pltpu.VMEM((N_Q_HEADS, N_META_ROWS, N_META_ROWS), f32),
        ],
        compiler_params=_cparams("arbitrary"),
        name="swa_attention",
    )(rel_bias, sinks, q, k, v, k, v, k, v, jnp.asarray(_BK_REAL), jnp.asarray(_BK_META))


def _mix_kernel(h_ref, a_ref, c_ref, wg_ref, wap_ref, wcp_ref, wo_ref, g_ref, b_ref, o_ref):
    h = h_ref[...]
    gates = jnp.dot(h.astype(bf16), wg_ref[...], preferred_element_type=f32)
    ya = jnp.dot(a_ref[...], wap_ref[...], preferred_element_type=f32)
    yc = jnp.dot(c_ref[...], wcp_ref[...], preferred_element_type=f32)
    m = jax.nn.sigmoid(gates[:, :D_MODEL]) * ya + jax.nn.sigmoid(gates[:, D_MODEL:]) * yc
    mix = jnp.dot(m.astype(bf16), wo_ref[...], preferred_element_type=f32)
    o_ref[...] = _ln(ALPHA * h + mix, g_ref[...], b_ref[...])


def _mix(h, a, c, wg, wap, wcp, wo, g, b):
    row = lambda w: pl.BlockSpec((TM, w), lambda s: (s, 0))
    return pl.pallas_call(
        _mix_kernel,
        grid=(N_TILES,),
        in_specs=[
            row(D_MODEL), row(ATTN_WIDTH), row(CONV_CH),
            _full((D_MODEL, 2 * D_MODEL)), _full((ATTN_WIDTH, D_MODEL)),
            _full((CONV_CH, D_MODEL)), _full((D_MODEL, D_MODEL)),
            _full((1, D_MODEL)), _full((1, D_MODEL)),
        ],
        out_specs=row(D_MODEL),
        out_shape=jax.ShapeDtypeStruct((NT, D_MODEL), f32),
        compiler_params=_cparams("parallel"),
        name="mix_ln1",
    )(h, a, c, wg, wap, wcp, wo, g, b)


def _interleave_gate_up(wg, wu, chunks):
    parts, c0 = [], 0
    for w in chunks:
        parts += [wg[..., c0:c0 + w], wu[..., c0:c0 + w]]
        c0 += w
    return jnp.concatenate(parts, axis=-1).astype(bf16)


def _swiglu_chunks(xb, wgu_ref, wd_ref, chunks, row_scale=None):
    acc = None
    c0 = 0
    for w in chunks:
        gu = jnp.dot(xb, wgu_ref[:, 2 * c0:2 * c0 + 2 * w], preferred_element_type=f32)
        gt = gu[:, :w]
        act = gt * jax.nn.sigmoid(gt) * gu[:, w:]
        if row_scale is not None:
            act = act * row_scale
        part = jnp.dot(act.astype(bf16), wd_ref[c0:c0 + w, :], preferred_element_type=f32)
        acc = part if acc is None else acc + part
        c0 += w
    return acc


def _ffn_kernel(h_ref, wgu_ref, wd_ref, g_ref, b_ref, o_ref):
    h = h_ref[...]
    f = _swiglu_chunks(h.astype(bf16), wgu_ref, wd_ref, FF_CHUNKS_DENSE)
    o_ref[...] = _ln(ALPHA * h + f, g_ref[...], b_ref[...])


def _dense_ffn(h, wgu, wd, g, b):
    row = pl.BlockSpec((TM, D_MODEL), lambda s: (s, 0))
    return pl.pallas_call(
        _ffn_kernel,
        grid=(N_TILES,),
        in_specs=[row, _full((D_MODEL, 2 * D_FF)), _full((D_FF, D_MODEL)),
                  _full((1, D_MODEL)), _full((1, D_MODEL))],
        out_specs=row,
        out_shape=jax.ShapeDtypeStruct((NT, D_MODEL), f32),
        compiler_params=_cparams("parallel"),
        name="dense_ffn_ln2",
    )(h, wgu, wd, g, b)


def _route_copy(xbuf_ref, xs_ref, sem_ref, slot, e, base):
    return pltpu.make_async_copy(
        xbuf_ref.at[slot, pl.ds(e * CHUNK, CHUNK), :],
        xs_ref.at[pl.ds(pl.multiple_of(e * CAP + base, ALIGN), CHUNK), :],
        sem_ref.at[slot, e])


def _zero_copy(zbuf_ref, xs_ref, sem_ref, e, base):
    return pltpu.make_async_copy(
        zbuf_ref,
        xs_ref.at[pl.ds(pl.multiple_of(e * CAP + base, ALIGN), TM), :],
        sem_ref.at[0, e])


def _route_kernel(h_ref, r_ref, info_ref, infot_ref, base_ref, fill_ref, xs_ref,
                  xbuf_ref, zbuf_ref, cnt_ref, sem_ref):
    s = pl.program_id(0)
    slot = s % 2

    @pl.when(s == 0)
    def _():
        for e in range(N_EXPERTS):
            cnt_ref[e] = 0
        zbuf_ref[...] = jnp.zeros((TM, XS_W), bf16)

    h = h_ref[...]
    hi = h.astype(bf16)
    lo = (h - hi.astype(f32)).astype(bf16)
    pa = jnp.dot(hi, r_ref[...], preferred_element_type=f32)
    pb = jnp.dot(lo, r_ref[...], preferred_element_type=f32)
    logits = pa + pltpu.roll(pa, 128 - N_EXPERTS, axis=1) + pb
    lt = logits.T[:N_EXPERTS, :]

    eid = lax.broadcasted_iota(i32, (N_EXPERTS, TM), 0)
    m1 = jnp.max(lt, axis=0, keepdims=True)
    i1 = jnp.min(jnp.where(lt == m1, eid, N_EXPERTS), axis=0, keepdims=True)
    lt2 = jnp.where(eid == i1, NEG, lt)
    m2 = jnp.max(lt2, axis=0, keepdims=True)
    i2 = jnp.min(jnp.where(lt2 == m2, eid, N_EXPERTS), axis=0, keepdims=True)
    ex = jnp.exp(m2 - m1)
    g1 = 1.0 / (1.0 + ex)
    g2 = ex * g1
    tok = lax.broadcasted_iota(i32, (1, TM), 1)
    valid = (s > 0) | (tok < N_META_ROWS)
    sel1 = (eid == i1) & valid
    sel2 = (eid == i2) & valid
    sel = sel1 | sel2
    onehot = jnp.where(sel, 1.0, 0.0)
    tri = jnp.where(lax.broadcasted_iota(i32, (TM, TM), 0) < lax.broadcasted_iota(i32, (TM, TM), 1), 1.0, 0.0)
    rank = jnp.dot(onehot, tri, preferred_element_type=f32)
    gate = jnp.where(sel1, g1, jnp.where(sel2, g2, 0.0))
    rank1 = jnp.sum(jnp.where(sel1, rank, 0.0), axis=0, keepdims=True)
    rank2 = jnp.sum(jnp.where(sel2, rank, 0.0), axis=0, keepdims=True)
    c1 = jnp.where(valid, i1.astype(f32) * CHUNK + rank1, -1.0)
    c2 = jnp.where(valid, i2.astype(f32) * CHUNK + rank2, -1.0)
    info = jnp.concatenate([c1, c2, jnp.zeros((128 - 2, TM), f32)], axis=0)
    infot_ref[...] = info[:8, :]
    info_ref[...] = info.T

    slot_id = lax.broadcasted_iota(i32, (CHUNK, TM), 0)
    lane = lax.broadcasted_iota(i32, (CHUNK, 128), 1)
    ranki = rank.astype(i32)
    ps = []
    for e in range(N_EXPERTS):
        pe = (slot_id == ranki[e:e + 1, :]) & sel[e:e + 1, :]
        ps.append(jnp.where(pe, 1.0, 0.0).astype(bf16))
        gs = jnp.sum(jnp.where(pe, gate[e:e + 1, :], 0.0), axis=1, keepdims=True)
        g_hi = gs.astype(bf16).astype(f32)
        g_mid = (gs - g_hi).astype(bf16).astype(f32)
        g_lo = gs - g_hi - g_mid
        parts = jnp.where(lane == 0, g_hi, jnp.where(lane == 1, g_mid, jnp.where(lane == 2, g_lo, 0.0)))
        xbuf_ref[slot, e * CHUNK:(e + 1) * CHUNK, D_MODEL:] = parts.astype(bf16)
    pall = jnp.concatenate(ps, axis=0)
    xbuf_ref[slot, :, :D_MODEL] = jnp.dot(pall, hi, preferred_element_type=f32).astype(bf16)

    @pl.when(s > 0)
    def _():
        for e in range(N_EXPERTS):
            _route_copy(xbuf_ref, xs_ref, sem_ref, 1 - slot, e, 0).wait()

    bases = []
    for e in range(N_EXPERTS):
        base = cnt_ref[e]
        bases.append(base)
        base_ref[s, e] = base
        _route_copy(xbuf_ref, xs_ref, sem_ref, slot, e, base).start()
        n_e = jnp.sum(onehot[e:e + 1, :]).astype(i32)
        cnt_ref[e] = base + ((n_e + (ALIGN - 1)) // ALIGN) * ALIGN

    @pl.when(s == N_TILES - 1)
    def _():
        for e in range(N_EXPERTS):
            _route_copy(xbuf_ref, xs_ref, sem_ref, slot, e, 0).wait()
        for e in range(N_EXPERTS):
            end = bases[e] + CHUNK
            fill_ref[e] = end
            _zero_copy(zbuf_ref, xs_ref, sem_ref, e, end).start()
        for e in range(N_EXPERTS):
            _zero_copy(zbuf_ref, xs_ref, sem_ref, e, 0).wait()


def _route(h, rsplit):
    return pl.pallas_call(
        _route_kernel,
        grid=(N_TILES,),
        in_specs=[pl.BlockSpec((TM, D_MODEL), lambda s: (s, 0)), _full((D_MODEL, 128))],
        out_specs=[
            pl.BlockSpec((TM, 128), lambda s: (s, 0)),
            pl.BlockSpec((None, 8, TM), lambda s: (s, 0, 0)),
            _smem(), _smem(),
            pl.BlockSpec(memory_space=pl.ANY),
        ],
        out_shape=[
            jax.ShapeDtypeStruct((NT, 128), f32),
            jax.ShapeDtypeStruct((N_TILES, 8, TM), f32),
            jax.ShapeDtypeStruct((N_TILES, N_EXPERTS), i32),
            jax.ShapeDtypeStruct((N_EXPERTS,), i32),
            jax.ShapeDtypeStruct((N_EXPERTS * CAP, XS_W), bf16),
        ],
        scratch_shapes=[
            pltpu.VMEM((2, N_EXPERTS * CHUNK, XS_W), bf16),
            pltpu.VMEM((TM, XS_W), bf16),
            pltpu.SMEM((N_EXPERTS,), i32),
            pltpu.SemaphoreType.DMA((2, N_EXPERTS)),
        ],
        compiler_params=_cparams("arbitrary"),
        name="moe_route_dispatch",
    )(h, rsplit)


def _expert_kernel(te_ref, tb_ref, tv_ref, x_ref, wgu_ref, wd_ref, y_ref):
    t = pl.program_id(0)

    @pl.when(tv_ref[t] > 0)
    def _():
        x = x_ref[...]
        gate = jnp.sum(x[:, D_MODEL:].astype(f32), axis=1, keepdims=True)
        y = _swiglu_chunks(x[:, :D_MODEL], wgu_ref, wd_ref, FF_CHUNKS_MOE, row_scale=gate)
        y_ref[...] = y.astype(bf16)


def _expert_ffn(tile_e, tile_blk, tile_valid, xs, wgu, wd):
    grid_spec = pltpu.PrefetchScalarGridSpec(
        num_scalar_prefetch=3,
        grid=(FFN_GRID,),
        in_specs=[
            pl.BlockSpec((TM, XS_W), lambda t, te, tb, tv: (tb[t], 0)),
            pl.BlockSpec((None, D_MODEL, 2 * D_EXPERT), lambda t, te, tb, tv: (te[t], 0, 0)),
            pl.BlockSpec((None, D_EXPERT, D_MODEL), lambda t, te, tb, tv: (te[t], 0, 0)),
        ],
        out_specs=pl.BlockSpec((TM, D_MODEL), lambda t, te, tb, tv: (tb[t], 0)),
    )
    return pl.pallas_call(
        _expert_kernel,
        grid_spec=grid_spec,
        out_shape=jax.ShapeDtypeStruct((N_EXPERTS * CAP, D_MODEL), bf16),
        compiler_params=_cparams("arbitrary"),
        name="moe_expert_ffn",
    )(tile_e, tile_blk, tile_valid, xs, wgu, wd)


def _combine_copy(ys_ref, ybuf_ref, sem_ref, e, base):
    return pltpu.make_async_copy(
        ys_ref.at[pl.ds(pl.multiple_of(e * CAP + base, ALIGN), CHUNK), :],
        ybuf_ref.at[pl.ds(e * CHUNK, CHUNK), :],
        sem_ref.at[e])


def _combine_kernel(base_ref, h_ref, info_ref, ys_ref, g_ref, b_ref, o_ref, ybuf_ref, sem_ref):
    s = pl.program_id(0) + 1
    for e in range(N_EXPERTS):
        _combine_copy(ys_ref, ybuf_ref, sem_ref, e, base_ref[s, e]).start()
    info = info_ref[...]
    c1 = info[:, 0:1].astype(i32)
    c2 = info[:, 1:2].astype(i32)
    col = lax.broadcasted_iota(i32, (TM, N_EXPERTS * CHUNK), 1)
    qm = jnp.where((col == c1) | (col == c2), 1.0, 0.0).astype(bf16)
    for e in range(N_EXPERTS):
        _combine_copy(ys_ref, ybuf_ref, sem_ref, e, 0).wait()
    f = jnp.dot(qm, ybuf_ref[...], preferred_element_type=f32)
    o_ref[...] = _ln(ALPHA * h_ref[...] + f, g_ref[...], b_ref[...])


def _combine(base, h, info, ys, g, b):
    n_out = N_TILES - 1
    grid_spec = pltpu.PrefetchScalarGridSpec(
        num_scalar_prefetch=1,
        grid=(n_out,),
        in_specs=[
            pl.BlockSpec((TM, D_MODEL), lambda s, base: (s + 1, 0)),
            pl.BlockSpec((TM, 128), lambda s, base: (s + 1, 0)),
            pl.BlockSpec(memory_space=pl.ANY),
            pl.BlockSpec((1, D_MODEL), lambda s, base: (0, 0)),
            pl.BlockSpec((1, D_MODEL), lambda s, base: (0, 0)),
        ],
        out_specs=pl.BlockSpec((TM, D_MODEL), lambda s, base: (s, 0)),
        scratch_shapes=[
            pltpu.VMEM((N_EXPERTS * CHUNK, D_MODEL), bf16),
            pltpu.SemaphoreType.DMA((N_EXPERTS,)),
        ],
    )
    return pl.pallas_call(
        _combine_kernel,
        grid_spec=grid_spec,
        out_shape=jax.ShapeDtypeStruct((n_out * TM, D_MODEL), f32),
        compiler_params=_cparams("arbitrary"),
        name="moe_combine_ln2",
    )(base, h, info, ys, g, b)


def _tile_map(fill):
    n_tiles = (fill + TM - 1) // TM
    ends = jnp.cumsum(n_tiles)
    starts = ends - n_tiles
    t = jnp.arange(FFN_GRID, dtype=i32)
    e = jnp.minimum(jnp.sum((t[:, None] >= ends[None, :]).astype(i32), axis=1), N_EXPERTS - 1)
    valid = (t < ends[-1]).astype(i32)
    blk = e * (CAP // TM) + (t - starts[e])
    last = jnp.maximum(ends[-1] - 1, 0)
    e_last = jnp.minimum(jnp.sum((last >= ends).astype(i32)), N_EXPERTS - 1)
    blk_last = e_last * (CAP // TM) + (last - starts[e_last])
    e = jnp.where(valid > 0, e, e_last).astype(i32)
    blk = jnp.where(valid > 0, blk, blk_last).astype(i32)
    return e, blk, valid


def _split2(x):
    hi = x.astype(bf16)
    lo = (x - hi.astype(f32)).astype(bf16)
    return hi, lo


def kernel(x, meta_tokens, emb_ln_g, emb_ln_b, rel_bias, w_in, conv_dw, conv_b, conv_ln_g, conv_ln_b, sinks, w_attn_proj, w_conv_proj, w_out, ln1_g, ln1_b, ffn_w_gate, ffn_w_up, ffn_w_down, router, moe_w_gate, moe_w_up, moe_w_down, ln2_g, ln2_b):
    row = lambda v: v.reshape(1, -1)
    h = _embed(x.reshape(BATCH * SEQ, D_MODEL), meta_tokens, row(emb_ln_g), row(emb_ln_b))
    out = None
    for i in range(DEPTH):
        w1 = w_in[i, :, :GLU_END].astype(bf16)
        w2 = w_in[i, :, GLU_END:].astype(bf16)
        q, k, v, c = _inproj(h, w1, conv_dw[i], row(conv_b[i]), row(conv_ln_g[i]), row(conv_ln_b[i]))
        a = _attention(rel_bias, sinks[i], q, k, v)
        h = _mix(h, a, c, w2, w_attn_proj[i].astype(bf16), w_conv_proj[i].astype(bf16),
                 w_out[i].astype(bf16), row(ln1_g[i]), row(ln1_b[i]))
        j = i // 2
        if i % 2 == 0:
            wgu = _interleave_gate_up(ffn_w_gate[j], ffn_w_up[j], FF_CHUNKS_DENSE)
            h = _dense_ffn(h, wgu, ffn_w_down[j].astype(bf16), row(ln2_g[i]), row(ln2_b[i]))
        else:
            r_hi, r_lo = _split2(router[j])
            rsplit = jnp.concatenate(
                [r_hi, r_lo, jnp.zeros((D_MODEL, 128 - 2 * N_EXPERTS), bf16)], axis=1)
            info, _, base, fill, xs = _route(h, rsplit)
            tile_e, tile_blk, tile_valid = _tile_map(fill)
            wgu = _interleave_gate_up(moe_w_gate[j], moe_w_up[j], FF_CHUNKS_MOE)
            ys = _expert_ffn(tile_e, tile_blk, tile_valid, xs, wgu, moe_w_down[j].astype(bf16))
            out = _combine(base, h, info, ys, row(ln2_g[i]), row(ln2_b[i]))
    return out.reshape(BATCH, SEQ, D_MODEL)
```

```python
import functools
import math

import numpy as np
import jax
import jax.numpy as jnp
from jax import lax
from jax.experimental import pallas as pl
from jax.experimental.pallas import tpu as pltpu

f32 = jnp.float32
bf16 = jnp.bfloat16
i32 = jnp.int32

D_MODEL = 1024
BATCH = 8
SEQ = 2048
DEPTH = 2
N_META = 16
HEAD_DIM = 64
N_Q_HEADS = 8
N_KV_HEADS = 2
GROUP = N_Q_HEADS // N_KV_HEADS
ATTN_WIDTH = N_Q_HEADS * HEAD_DIM
KV_WIDTH = N_KV_HEADS * HEAD_DIM
WINDOW = 128
BLOCK = 128
CONV_CH = D_MODEL // 2
CONV_WIDTH = 31
N_BUCKETS = 32
MAX_DISTANCE = 128
D_FF = 2816
N_EXPERTS = 8
D_EXPERT = 1408
ALPHA = (2 * DEPTH) ** 0.25
LN_EPS = 1e-5
Q_END = ATTN_WIDTH
K_END = Q_END + KV_WIDTH
V_END = K_END + KV_WIDTH
GLU_END = V_END + 2 * CONV_CH
GA_END = GLU_END + D_MODEL
GC_END = GA_END + D_MODEL

TM = 512
N_META_ROWS = BATCH * N_META
NT = TM + BATCH * SEQ
N_TILES = NT // TM
TILES_PER_BATCH = SEQ // TM
BLOCKS_PER_TILE = TM // BLOCK
KSEG = 3 * BLOCK
CONV_HALO = 32
NEG = -1e30

CHUNK = 256
ALIGN = 16
XS_W = D_MODEL + 128
CAP = 35 * TM
FFN_GRID = 85
FF_CHUNKS_DENSE = (256,) * 11
FF_CHUNKS_MOE = (256,) * 5 + (128,)

VMEM_LIMIT = 56 * 1024 * 1024


def _cparams(sem="arbitrary"):
    return pltpu.CompilerParams(dimension_semantics=(sem,), vmem_limit_bytes=VMEM_LIMIT)


def _ln(x, g, b):
    mu = jnp.mean(x, -1, keepdims=True)
    xc = x - mu
    var = jnp.mean(xc * xc, -1, keepdims=True)
    return xc * lax.rsqrt(var + LN_EPS) * g + b


def _full(shape):
    return pl.BlockSpec(shape, lambda *_: (0,) * len(shape))


def _smem():
    return pl.BlockSpec(memory_space=pltpu.SMEM)


def _embed_kernel(x_ref, meta_ref, g_ref, b_ref, o_ref):
    i = pl.program_id(0)

    @pl.when(i == 0)
    def _():
        m = _ln(meta_ref[...], g_ref[...], b_ref[...])
        for bb in range(BATCH):
            o_ref[bb * N_META:(bb + 1) * N_META, :] = m
        o_ref[N_META_ROWS:, :] = jnp.zeros((TM - N_META_ROWS, D_MODEL), f32)

    @pl.when(i > 0)
    def _():
        o_ref[...] = _ln(x_ref[...], g_ref[...], b_ref[...])


def _embed(x2d, meta, g, b):
    return pl.pallas_call(
        _embed_kernel,
        grid=(N_TILES,),
        in_specs=[
            pl.BlockSpec((TM, D_MODEL), lambda i: (jnp.maximum(i - 1, 0), 0)),
            _full((N_META, D_MODEL)),
            _full((1, D_MODEL)),
            _full((1, D_MODEL)),
        ],
        out_specs=pl.BlockSpec((TM, D_MODEL), lambda i: (i, 0)),
        out_shape=jax.ShapeDtypeStruct((NT, D_MODEL), f32),
        compiler_params=_cparams("arbitrary"),
        name="embed_ln",
    )(x2d, meta, g, b)


def _inproj_kernel(h_ref, w_ref, q_ref, k_ref, v_ref, glu_ref):
    hb = h_ref[...].astype(bf16)
    u = jnp.dot(hb, w_ref[...], preferred_element_type=f32)
    q_ref[...] = (u[:, :Q_END] * (HEAD_DIM ** -0.5)).astype(bf16)
    k_ref[...] = u[:, Q_END:K_END].astype(bf16)
    v_ref[...] = u[:, K_END:V_END].astype(bf16)
    glu = u[:, V_END:V_END + CONV_CH] * jax.nn.sigmoid(u[:, V_END + CONV_CH:GLU_END])
    glu_ref[...] = glu.astype(bf16)


def _inproj(h, w1):
    row = lambda w: pl.BlockSpec((TM, w), lambda s: (s, 0))
    return pl.pallas_call(
        _inproj_kernel,
        grid=(N_TILES,),
        in_specs=[row(D_MODEL), _full((D_MODEL, GLU_END))],
        out_specs=[row(ATTN_WIDTH), row(KV_WIDTH), row(KV_WIDTH), row(CONV_CH)],
        out_shape=[
            jax.ShapeDtypeStruct((NT, ATTN_WIDTH), bf16),
            jax.ShapeDtypeStruct((NT, KV_WIDTH), bf16),
            jax.ShapeDtypeStruct((NT, KV_WIDTH), bf16),
            jax.ShapeDtypeStruct((NT, CONV_CH), bf16),
        ],
        compiler_params=_cparams("parallel"),
        name="inproj_glu",
    )(h, w1)


def _rel_bucket_np(dist):
    n = np.maximum(dist, 0)
    max_exact = N_BUCKETS // 2
    nf = np.maximum(n, 1).astype(np.float32)
    scaled = (np.log(nf / np.float32(max_exact)) / np.float32(math.log(MAX_DISTANCE / max_exact))
              * np.float32(N_BUCKETS - max_exact))
    large = np.minimum(max_exact + scaled.astype(np.int32), N_BUCKETS - 1)
    return np.where(n < max_exact, n, large).astype(np.int32)


def _bucket_tables():
    a = np.arange(BLOCK)[:, None]
    m = np.arange(BLOCK)[None, :]
    kk = np.arange(2 * BLOCK)[None, :]
    dist_band = BLOCK + a - kk
    band_ok = (dist_band >= 0) & (dist_band < WINDOW)
    band = np.where(band_ok, _rel_bucket_np(dist_band), -1)
    real = np.full((2, BLOCK, KSEG), -1, np.int32)
    for t, n in enumerate((0, 1)):
        dist_meta = N_META + n * BLOCK + a - m
        real[t, :, :BLOCK] = np.where(m < N_META, _rel_bucket_np(dist_meta), -1)
        real[t, :, BLOCK:] = band
    real[0, :, BLOCK:2 * BLOCK] = -1
    for n in range(2, SEQ // BLOCK):
        dist_meta = N_META + n * BLOCK + a - m
        assert np.array_equal(np.where(m < N_META, _rel_bucket_np(dist_meta), -1), real[1, :, :BLOCK])
    r = np.arange(N_META_ROWS)
    same = (r[:, None] // N_META) == (r[None, :] // N_META)
    dmm = (r[:, None] % N_META) - (r[None, :] % N_META)
    meta = np.where(same & (dmm >= 0), _rel_bucket_np(dmm), -1).astype(np.int32)
    return real, meta


_BK_REAL, _BK_META = _bucket_tables()


def _bias_from_buckets(bk, rb_ref, h):
    acc = jnp.full(bk.shape, NEG, f32)
    for bkt in range(N_BUCKETS):
        acc = jnp.where(bk == bkt, rb_ref[bkt, h], acc)
    return acc


def _expand_kv(x2):
    xf = x2.astype(f32)
    r = pltpu.roll(xf, HEAD_DIM, axis=1)
    low = lax.broadcasted_iota(i32, xf.shape, 1) < HEAD_DIM
    h0 = jnp.where(low, xf, r).astype(bf16)
    h1 = jnp.where(low, r, xf).astype(bf16)
    return (jnp.concatenate([h0, h0], axis=1), jnp.concatenate([h1, h1], axis=1))


def _block_diag(x, n_keys):
    lane_blk = lax.broadcasted_iota(i32, (n_keys, GROUP * HEAD_DIM), 1) // HEAD_DIM
    zero = jnp.zeros_like(x)
    return jnp.concatenate([jnp.where(lane_blk == h, x, zero) for h in range(GROUP)], axis=0)


def _attend(qg, kx, vx, bias_of_head, sink_of_head, n_keys):
    m_rows = qg.shape[0]
    kbd = _block_diag(kx, n_keys)
    vbd = _block_diag(vx, n_keys)
    s = lax.dot_general(qg, kbd, (((1,), (1,)), ((), ())), preferred_element_type=f32)
    ps, rs = [], []
    for h in range(GROUP):
        sh = s[:, h * n_keys:(h + 1) * n_keys] + bias_of_head(h)
        sink = sink_of_head(h)
        mx = jnp.maximum(jnp.max(sh, -1, keepdims=True), sink)
        p = jnp.exp(sh - mx)
        den = jnp.sum(p, -1, keepdims=True) + jnp.exp(sink - mx)
        ps.append(p.astype(bf16))
        rs.append(1.0 / den)
    pm = jnp.concatenate(ps, axis=1)
    o = jnp.dot(pm, vbd, preferred_element_type=f32)
    ol = lax.broadcasted_iota(i32, (m_rows, GROUP * HEAD_DIM), 1) // HEAD_DIM
    scale = jnp.where(ol == 0, rs[0], jnp.where(ol == 1, rs[1], jnp.where(ol == 2, rs[2], rs[3])))
    return o * scale


def _attn_kernel(rb_ref, sink_ref, q_ref, k_ref, v_ref, kp_ref, vp_ref, km_ref, vm_ref,
                 bkr_ref, bkm_ref, o_ref, br_ref, bm_ref):
    s = pl.program_id(0)
    gw = GROUP * HEAD_DIM

    @pl.when(s == 0)
    def _():
        for h in range(N_Q_HEADS):
            for t in range(2):
                br_ref[t, h] = _bias_from_buckets(bkr_ref[t], rb_ref, h)
            bm_ref[h] = _bias_from_buckets(bkm_ref[...], rb_ref, h)
        kxs = _expand_kv(k_ref[:N_META_ROWS, :])
        vxs = _expand_kv(v_ref[:N_META_ROWS, :])
        for g in range(N_KV_HEADS):
            o = _attend(q_ref[:N_META_ROWS, g * gw:(g + 1) * gw], kxs[g], vxs[g],
                        lambda h, g=g: bm_ref[g * GROUP + h],
                        lambda h, g=g: sink_ref[g * GROUP + h], N_META_ROWS)
            o_ref[:N_META_ROWS, g * gw:(g + 1) * gw] = o.astype(bf16)
        o_ref[N_META_ROWS:, :] = jnp.zeros((TM - N_META_ROWS, ATTN_WIDTH), bf16)

    @pl.when(s > 0)
    def _():
        first = ((s - 1) % TILES_PER_BATCH) == 0
        tbl0 = jnp.where(first, 0, 1)
        zpad = jnp.zeros((BLOCK - N_META, KV_WIDTH), bf16)
        kmeta = jnp.concatenate([km_ref[...], zpad], axis=0)
        vmeta = jnp.concatenate([vm_ref[...], zpad], axis=0)
        for blk in range(BLOCKS_PER_TILE):
            rows = slice(blk * BLOCK, (blk + 1) * BLOCK)
            if blk == 0:
                kprev, vprev = kp_ref[...], vp_ref[...]
            else:
                prows = slice((blk - 1) * BLOCK, blk * BLOCK)
                kprev, vprev = k_ref[prows, :], v_ref[prows, :]
            kxs = _expand_kv(jnp.concatenate([kmeta, kprev, k_ref[rows, :]], axis=0))
            vxs = _expand_kv(jnp.concatenate([vmeta, vprev, v_ref[rows, :]], axis=0))
            for g in range(N_KV_HEADS):
                if blk == 0:
                    bias = lambda h, g=g: br_ref[tbl0, g * GROUP + h]
                else:
                    bias = lambda h, g=g: br_ref[1, g * GROUP + h]
                o = _attend(q_ref[rows, g * gw:(g + 1) * gw], kxs[g], vxs[g], bias,
                            lambda h, g=g: sink_ref[g * GROUP + h], KSEG)
                o_ref[rows, g * gw:(g + 1) * gw] = o.astype(bf16)


def _attention(rel_bias, sinks, q, k, v):
    row = lambda w: pl.BlockSpec((TM, w), lambda s: (s, 0))
    prev = pl.BlockSpec((BLOCK, KV_WIDTH), lambda s: (jnp.maximum(s * BLOCKS_PER_TILE - 1, 0), 0))
    meta = pl.BlockSpec((N_META, KV_WIDTH), lambda s: (jnp.maximum(s - 1, 0) // TILES_PER_BATCH, 0))
    return pl.pallas_call(
        _attn_kernel,
        grid=(N_TILES,),
        in_specs=[
            _smem(), _smem(),
            row(ATTN_WIDTH), row(KV_WIDTH), row(KV_WIDTH),
            prev, prev, meta, meta,
            _full((2, BLOCK, KSEG)), _full((N_META_ROWS, N_META_ROWS)),
        ],
        out_specs=row(ATTN_WIDTH),
        out_shape=jax.ShapeDtypeStruct((NT, ATTN_WIDTH), bf16),
        scratch_shapes=[
            pltpu.VMEM((2, N_Q_HEADS, BLOCK, KSEG), f32),
            pltpu.VMEM((N_Q_HEADS, N_META_ROWS, N_META_ROWS), f32),
        ],
        compiler_params=_cparams("arbitrary"),
        name="swa_attention",
    )(rel_bias, sinks, q, k, v, k, v, k, v, jnp.asarray(_BK_REAL), jnp.asarray(_BK_META))


SPREAD = TM // BATCH
SH_ROWS = CONV_HALO + TM - 8


def _conv_module(s, glu_ref, gprev_ref, gmeta_ref, dw_ref, cb_ref, cg_ref, cbeta_ref,
                 win_ref, sh_ref, cv_ref):
    is_meta = s == 0
    first = ((s - 1) % TILES_PER_BATCH) == 0
    glu = glu_ref[...].astype(f32)
    zgap = jnp.zeros((SPREAD - N_META, CONV_CH), f32)
    spread = jnp.concatenate(
        [p for bb in range(BATCH) for p in (zgap, glu[bb * N_META:(bb + 1) * N_META])], axis=0)
    win_ref[CONV_HALO:, :] = jnp.where(is_meta, spread, glu)
    halo_first = jnp.concatenate(
        [jnp.zeros((CONV_HALO - N_META, CONV_CH), f32), gmeta_ref[...].astype(f32)], axis=0)
    halo = jnp.where(first, halo_first, gprev_ref[...].astype(f32))
    win_ref[:CONV_HALO, :] = jnp.where(is_meta, 0.0, halo)

    for r in range(1, 8):
        sh_ref[r - 1] = win_ref[r:r + SH_ROWS, :]
    off = CONV_HALO - (CONV_WIDTH - 1)
    for rc in range(TM // 128):
        for lc in range(CONV_CH // 128):
            ls = slice(lc * 128, (lc + 1) * 128)
            acc = None
            for t in range(CONV_WIDTH):
                r = (off + t) % 8
                r0 = rc * 128 + (off + t) - r
                x = win_ref[r0:r0 + 128, ls] if r == 0 else sh_ref[r - 1, r0:r0 + 128, ls]
                term = dw_ref[t:t + 1, ls] * x
                acc = term if acc is None else acc + term
            cv_ref[rc * 128:(rc + 1) * 128, ls] = acc
    y = _ln(cv_ref[...] + cb_ref[...], cg_ref[...], cbeta_ref[...])
    c = y * jax.nn.sigmoid(y)
    gathered = jnp.concatenate(
        [c[bb * SPREAD + SPREAD - N_META:(bb + 1) * SPREAD] for bb in range(BATCH)]
        + [jnp.zeros((TM - N_META_ROWS, CONV_CH), f32)], axis=0)
    return jnp.where(is_meta, gathered, c).astype(bf16)


def _mix_kernel(h_ref, a_ref, glu_ref, gprev_ref, gmeta_ref, dw_ref, cb_ref, cg_ref, cbeta_ref,
                wg_ref, wap_ref, wcp_ref, wo_ref, g_ref, b_ref, o_ref, win_ref, sh_ref, cv_ref):
    s = pl.program_id(0)
    h = h_ref[...]
    gates = jnp.dot(h.astype(bf16), wg_ref[...], preferred_element_type=f32)
    ya = jnp.dot(a_ref[...], wap_ref[...], preferred_element_type=f32)
    c = _conv_module(s, glu_ref, gprev_ref, gmeta_ref, dw_ref, cb_ref, cg_ref, cbeta_ref,
                     win_ref, sh_ref, cv_ref)
    yc = jnp.dot(c, wcp_ref[...], preferred_element_type=f32)
    m = jax.nn.sigmoid(gates[:, :D_MODEL]) * ya + jax.nn.sigmoid(gates[:, D_MODEL:]) * yc
    mix = jnp.dot(m.astype(bf16), wo_ref[...], preferred_element_type=f32)
    o_ref[...] = _ln(ALPHA * h + mix, g_ref[...], b_ref[...])


def _mix(h, a, glu, dw, cb, cg, cbeta, wg, wap, wcp, wo, g, b):
    row = lambda w: pl.BlockSpec((TM, w), lambda s: (s, 0))
    gprev = pl.BlockSpec((CONV_HALO, CONV_CH),
                         lambda s: (jnp.maximum(s * (TM // CONV_HALO) - 1, 0), 0))
    gmeta = pl.BlockSpec((N_META, CONV_CH), lambda s: (jnp.maximum(s - 1, 0) // TILES_PER_BATCH, 0))
    return pl.pallas_call(
        _mix_kernel,
        grid=(N_TILES,),
        in_specs=[
            row(D_MODEL), row(ATTN_WIDTH), row(CONV_CH), gprev, gmeta,
            _full((CONV_WIDTH, CONV_CH)), _full((1, CONV_CH)), _full((1, CONV_CH)), _full((1, CONV_CH)),
            _full((D_MODEL, 2 * D_MODEL)), _full((ATTN_WIDTH, D_MODEL)),
            _full((CONV_CH, D_MODEL)), _full((D_MODEL, D_MODEL)),
            _full((1, D_MODEL)), _full((1, D_MODEL)),
        ],
        out_specs=row(D_MODEL),
        out_shape=jax.ShapeDtypeStruct((NT, D_MODEL), f32),
        scratch_shapes=[
            pltpu.VMEM((CONV_HALO + TM, CONV_CH), f32),
            pltpu.VMEM((7, SH_ROWS, CONV_CH), f32),
            pltpu.VMEM((TM, CONV_CH), f32),
        ],
        compiler_params=_cparams("parallel"),
        name="conv_mix_ln1",
    )(h, a, glu, glu, glu, dw, cb, cg, cbeta, wg, wap, wcp, wo, g, b)


def _interleave_gate_up(wg, wu, chunks):
    parts, c0 = [], 0
    for w in chunks:
        parts += [wg[..., c0:c0 + w], wu[..., c0:c0 + w]]
        c0 += w
    return jnp.concatenate(parts, axis=-1).astype(bf16)


def _swiglu_chunks(xb, wgu_ref, wd_ref, chunks, row_scale=None):
    acc = None
    c0 = 0
    for w in chunks:
        gu = jnp.dot(xb, wgu_ref[:, 2 * c0:2 * c0 + 2 * w], preferred_element_type=f32)
        gt = gu[:, :w]
        act = gt * jax.nn.sigmoid(gt) * gu[:, w:]
        if row_scale is not None:
            act = act * row_scale
        part = jnp.dot(act.astype(bf16), wd_ref[c0:c0 + w, :], preferred_element_type=f32)
        acc = part if acc is None else acc + part
        c0 += w
    return acc


def _ffn_kernel(h_ref, wgu_ref, wd_ref, g_ref, b_ref, o_ref):
    h = h_ref[...]
    f = _swiglu_chunks(h.astype(bf16), wgu_ref, wd_ref, FF_CHUNKS_DENSE)
    o_ref[...] = _ln(ALPHA * h + f, g_ref[...], b_ref[...])


def _dense_ffn(h, wgu, wd, g, b):
    row = pl.BlockSpec((TM, D_MODEL), lambda s: (s, 0))
    return pl.pallas_call(
        _ffn_kernel,
        grid=(N_TILES,),
        in_specs=[row, _full((D_MODEL, 2 * D_FF)), _full((D_FF, D_MODEL)),
                  _full((1, D_MODEL)), _full((1, D_MODEL))],
        out_specs=row,
        out_shape=jax.ShapeDtypeStruct((NT, D_MODEL), f32),
        compiler_params=_cparams("parallel"),
        name="dense_ffn_ln2",
    )(h, wgu, wd, g, b)


def _route_copy(xbuf_ref, xs_ref, sem_ref, slot, e, base):
    return pltpu.make_async_copy(
        xbuf_ref.at[slot, pl.ds(e * CHUNK, CHUNK), :],
        xs_ref.at[pl.ds(pl.multiple_of(e * CAP + base, ALIGN), CHUNK), :],
        sem_ref.at[slot, e])


def _zero_copy(zbuf_ref, xs_ref, sem_ref, e, base):
    return pltpu.make_async_copy(
        zbuf_ref,
        xs_ref.at[pl.ds(pl.multiple_of(e * CAP + base, ALIGN), TM), :],
        sem_ref.at[0, e])


def _route_kernel(h_ref, r_ref, info_ref, infot_ref, base_ref, fill_ref, xs_ref,
                  xbuf_ref, zbuf_ref, cnt_ref, sem_ref):
    s = pl.program_id(0)
    slot = s % 2

    @pl.when(s == 0)
    def _():
        for e in range(N_EXPERTS):
            cnt_ref[e] = 0
        zbuf_ref[...] = jnp.zeros((TM, XS_W), bf16)

    h = h_ref[...]
    hi = h.astype(bf16)
    lo = (h - hi.astype(f32)).astype(bf16)
    pa = jnp.dot(hi, r_ref[...], preferred_element_type=f32)
    pb = jnp.dot(lo, r_ref[...], preferred_element_type=f32)
    logits = pa + pltpu.roll(pa, 128 - N_EXPERTS, axis=1) + pb
    lt = logits.T[:N_EXPERTS, :]

    eid = lax.broadcasted_iota(i32, (N_EXPERTS, TM), 0)
    m1 = jnp.max(lt, axis=0, keepdims=True)
    i1 = jnp.min(jnp.where(lt == m1, eid, N_EXPERTS), axis=0, keepdims=True)
    lt2 = jnp.where(eid == i1, NEG, lt)
    m2 = jnp.max(lt2, axis=0, keepdims=True)
    i2 = jnp.min(jnp.where(lt2 == m2, eid, N_EXPERTS), axis=0, keepdims=True)
    ex = jnp.exp(m2 - m1)
    g1 = 1.0 / (1.0 + ex)
    g2 = ex * g1
    tok = lax.broadcasted_iota(i32, (1, TM), 1)
    valid = (s > 0) | (tok < N_META_ROWS)
    sel1 = (eid == i1) & valid
    sel2 = (eid == i2) & valid
    sel = sel1 | sel2
    onehot = jnp.where(sel, 1.0, 0.0)
    tri = jnp.where(lax.broadcasted_iota(i32, (TM, TM), 0) < lax.broadcasted_iota(i32, (TM, TM), 1), 1.0, 0.0)
    rank = jnp.dot(onehot, tri, preferred_element_type=f32)
    gate = jnp.where(sel1, g1, jnp.where(sel2, g2, 0.0))
    rank1 = jnp.sum(jnp.where(sel1, rank, 0.0), axis=0, keepdims=True)
    rank2 = jnp.sum(jnp.where(sel2, rank, 0.0), axis=0, keepdims=True)
    c1 = jnp.where(valid, i1.astype(f32) * CHUNK + rank1, -1.0)
    c2 = jnp.where(valid, i2.astype(f32) * CHUNK + rank2, -1.0)
    info = jnp.concatenate([c1, c2, jnp.zeros((128 - 2, TM), f32)], axis=0)
    infot_ref[...] = info[:8, :]
    info_ref[...] = info.T

    slot_id = lax.broadcasted_iota(i32, (CHUNK, TM), 0)
    lane = lax.broadcasted_iota(i32, (CHUNK, 128), 1)
    ranki = rank.astype(i32)
    ps = []
    for e in range(N_EXPERTS):
        pe = (slot_id == ranki[e:e + 1, :]) & sel[e:e + 1, :]
        ps.append(jnp.where(pe, 1.0, 0.0).astype(bf16))
        gs = jnp.sum(jnp.where(pe, gate[e:e + 1, :], 0.0), axis=1, keepdims=True)
        g_hi = gs.astype(bf16).astype(f32)
        g_mid = (gs - g_hi).astype(bf16).astype(f32)
        g_lo = gs - g_hi - g_mid
        parts = jnp.where(lane == 0, g_hi, jnp.where(lane == 1, g_mid, jnp.where(lane == 2, g_lo, 0.0)))
        xbuf_ref[slot, e * CHUNK:(e + 1) * CHUNK, D_MODEL:] = parts.astype(bf16)
    pall = jnp.concatenate(ps, axis=0)
    xbuf_ref[slot, :, :D_MODEL] = jnp.dot(pall, hi, preferred_element_type=f32).astype(bf16)

    @pl.when(s > 0)
    def _():
        for e in range(N_EXPERTS):
            _route_copy(xbuf_ref, xs_ref, sem_ref, 1 - slot, e, 0).wait()

    bases = []
    for e in range(N_EXPERTS):
        base = cnt_ref[e]
        bases.append(base)
        base_ref[s, e] = base
        _route_copy(xbuf_ref, xs_ref, sem_ref, slot, e, base).start()
        n_e = jnp.sum(onehot[e:e + 1, :]).astype(i32)
        cnt_ref[e] = base + ((n_e + (ALIGN - 1)) // ALIGN) * ALIGN

    @pl.when(s == N_TILES - 1)
    def _():
        for e in range(N_EXPERTS):
            _route_copy(xbuf_ref, xs_ref, sem_ref, slot, e, 0).wait()
        for e in range(N_EXPERTS):
            end = bases[e] + CHUNK
            fill_ref[e] = end
            _zero_copy(zbuf_ref, xs_ref, sem_ref, e, end).start()
        for e in range(N_EXPERTS):
            _zero_copy(zbuf_ref, xs_ref, sem_ref, e, 0).wait()


def _route(h, rsplit):
    return pl.pallas_call(
        _route_kernel,
        grid=(N_TILES,),
        in_specs=[pl.BlockSpec((TM, D_MODEL), lambda s: (s, 0)), _full((D_MODEL, 128))],
        out_specs=[
            pl.BlockSpec((TM, 128), lambda s: (s, 0)),
            pl.BlockSpec((None, 8, TM), lambda s: (s, 0, 0)),
            _smem(), _smem(),
            pl.BlockSpec(memory_space=pl.ANY),
        ],
        out_shape=[
            jax.ShapeDtypeStruct((NT, 128), f32),
            jax.ShapeDtypeStruct((N_TILES, 8, TM), f32),
            jax.ShapeDtypeStruct((N_TILES, N_EXPERTS), i32),
            jax.ShapeDtypeStruct((N_EXPERTS,), i32),
            jax.ShapeDtypeStruct((N_EXPERTS * CAP, XS_W), bf16),
        ],
        scratch_shapes=[
            pltpu.VMEM((2, N_EXPERTS * CHUNK, XS_W), bf16),
            pltpu.VMEM((TM, XS_W), bf16),
            pltpu.SMEM((N_EXPERTS,), i32),
            pltpu.SemaphoreType.DMA((2, N_EXPERTS)),
        ],
        compiler_params=_cparams("arbitrary"),
        name="moe_route_dispatch",
    )(h, rsplit)


def _expert_kernel(te_ref, tb_ref, tv_ref, x_ref, wgu_ref, wd_ref, y_ref):
    t = pl.program_id(0)

    @pl.when(tv_ref[t] > 0)
    def _():
        x = x_ref[...]
        gate = jnp.sum(x[:, D_MODEL:].astype(f32), axis=1, keepdims=True)
        y = _swiglu_chunks(x[:, :D_MODEL], wgu_ref, wd_ref, FF_CHUNKS_MOE, row_scale=gate)
        y_ref[...] = y.astype(bf16)


def _expert_ffn(tile_e, tile_blk, tile_valid, xs, wgu, wd):
    grid_spec = pltpu.PrefetchScalarGridSpec(
        num_scalar_prefetch=3,
        grid=(FFN_GRID,),
        in_specs=[
            pl.BlockSpec((TM, XS_W), lambda t, te, tb, tv: (tb[t], 0)),
            pl.BlockSpec((None, D_MODEL, 2 * D_EXPERT), lambda t, te, tb, tv: (te[t], 0, 0)),
            pl.BlockSpec((None, D_EXPERT, D_MODEL), lambda t, te, tb, tv: (te[t], 0, 0)),
        ],
        out_specs=pl.BlockSpec((TM, D_MODEL), lambda t, te, tb, tv: (tb[t], 0)),
    )
    return pl.pallas_call(
        _expert_kernel,
        grid_spec=grid_spec,
        out_shape=jax.ShapeDtypeStruct((N_EXPERTS * CAP, D_MODEL), bf16),
        compiler_params=_cparams("arbitrary"),
        name="moe_expert_ffn",
    )(tile_e, tile_blk, tile_valid, xs, wgu, wd)


def _combine_copy(ys_ref, ybuf_ref, sem_ref, e, base):
    return pltpu.make_async_copy(
        ys_ref.at[pl.ds(pl.multiple_of(e * CAP + base, ALIGN), CHUNK), :],
        ybuf_ref.at[pl.ds(e * CHUNK, CHUNK), :],
        sem_ref.at[e])


def _combine_kernel(base_ref, h_ref, info_ref, ys_ref, g_ref, b_ref, o_ref, ybuf_ref, sem_ref):
    s = pl.program_id(0) + 1
    for e in range(N_EXPERTS):
        _combine_copy(ys_ref, ybuf_ref, sem_ref, e, base_ref[s, e]).start()
    info = info_ref[...]
    c1 = info[:, 0:1].astype(i32)
    c2 = info[:, 1:2].astype(i32)
    col = lax.broadcasted_iota(i32, (TM, N_EXPERTS * CHUNK), 1)
    qm = jnp.where((col == c1) | (col == c2), 1.0, 0.0).astype(bf16)
    for e in range(N_EXPERTS):
        _combine_copy(ys_ref, ybuf_ref, sem_ref, e, 0).wait()
    f = jnp.dot(qm, ybuf_ref[...], preferred_element_type=f32)
    o_ref[...] = _ln(ALPHA * h_ref[...] + f, g_ref[...], b_ref[...])


def _combine(base, h, info, ys, g, b):
    n_out = N_TILES - 1
    grid_spec = pltpu.PrefetchScalarGridSpec(
        num_scalar_prefetch=1,
        grid=(n_out,),
        in_specs=[
            pl.BlockSpec((TM, D_MODEL), lambda s, base: (s + 1, 0)),
            pl.BlockSpec((TM, 128), lambda s, base: (s + 1, 0)),
            pl.BlockSpec(memory_space=pl.ANY),
            pl.BlockSpec((1, D_MODEL), lambda s, base: (0, 0)),
            pl.BlockSpec((1, D_MODEL), lambda s, base: (0, 0)),
        ],
        out_specs=pl.BlockSpec((TM, D_MODEL), lambda s, base: (s, 0)),
        scratch_shapes=[
            pltpu.VMEM((N_EXPERTS * CHUNK, D_MODEL), bf16),
            pltpu.SemaphoreType.DMA((N_EXPERTS,)),
        ],
    )
    return pl.pallas_call(
        _combine_kernel,
        grid_spec=grid_spec,
        out_shape=jax.ShapeDtypeStruct((n_out * TM, D_MODEL), f32),
        compiler_params=_cparams("arbitrary"),
        name="moe_combine_ln2",
    )(base, h, info, ys, g, b)


def _tile_map(fill):
    n_tiles = (fill + TM - 1) // TM
    ends = jnp.cumsum(n_tiles)
    starts = ends - n_tiles
    t = jnp.arange(FFN_GRID, dtype=i32)
    e = jnp.minimum(jnp.sum((t[:, None] >= ends[None, :]).astype(i32), axis=1), N_EXPERTS - 1)
    valid = (t < ends[-1]).astype(i32)
    blk = e * (CAP // TM) + (t - starts[e])
    last = jnp.maximum(ends[-1] - 1, 0)
    e_last = jnp.minimum(jnp.sum((last >= ends).astype(i32)), N_EXPERTS - 1)
    blk_last = e_last * (CAP // TM) + (last - starts[e_last])
    e = jnp.where(valid > 0, e, e_last).astype(i32)
    blk = jnp.where(valid > 0, blk, blk_last).astype(i32)
    return e, blk, valid


def _split2(x):
    hi = x.astype(bf16)
    lo = (x - hi.astype(f32)).astype(bf16)
    return hi, lo


def kernel(x, meta_tokens, emb_ln_g, emb_ln_b, rel_bias, w_in, conv_dw, conv_b, conv_ln_g, conv_ln_b, sinks, w_attn_proj, w_conv_proj, w_out, ln1_g, ln1_b, ffn_w_gate, ffn_w_up, ffn_w_down, router, moe_w_gate, moe_w_up, moe_w_down, ln2_g, ln2_b):
    row = lambda v: v.reshape(1, -1)
    h = _embed(x.reshape(BATCH * SEQ, D_MODEL), meta_tokens, row(emb_ln_g), row(emb_ln_b))
    out = None
    for i in range(DEPTH):
        w1 = w_in[i, :, :GLU_END].astype(bf16)
        w2 = w_in[i, :, GLU_END:].astype(bf16)
        q, k, v, glu = _inproj(h, w1)
        a = _attention(rel_bias, sinks[i], q, k, v)
        h = _mix(h, a, glu, conv_dw[i], row(conv_b[i]), row(conv_ln_g[i]), row(conv_ln_b[i]),
                 w2, w_attn_proj[i].astype(bf16), w_conv_proj[i].astype(bf16),
                 w_out[i].astype(bf16), row(ln1_g[i]), row(ln1_b[i]))
        j = i // 2
        if i % 2 == 0:
            wgu = _interleave_gate_up(ffn_w_gate[j], ffn_w_up[j], FF_CHUNKS_DENSE)
            h = _dense_ffn(h, wgu, ffn_w_down[j].astype(bf16), row(ln2_g[i]), row(ln2_b[i]))
        else:
            r_hi, r_lo = _split2(router[j])
            rsplit = jnp.concatenate(
                [r_hi, r_lo, jnp.zeros((D_MODEL, 128 - 2 * N_EXPERTS), bf16)], axis=1)
            info, _, base, fill, xs = _route(h, rsplit)
            tile_e, tile_blk, tile_valid = _tile_map(fill)
            wgu = _interleave_gate_up(moe_w_gate[j], moe_w_up[j], FF_CHUNKS_MOE)
            ys = _expert_ffn(tile_e, tile_blk, tile_valid, xs, wgu, moe_w_down[j].astype(bf16))
            out = _combine(base, h, info, ys, row(ln2_g[i]), row(ln2_b[i]))
    return out.reshape(BATCH, SEQ, D_MODEL)
```

```python
import functools
import math

import numpy as np
import jax
import jax.numpy as jnp
from jax import lax
from jax.experimental import pallas as pl
from jax.experimental.pallas import tpu as pltpu

f32 = jnp.float32
bf16 = jnp.bfloat16
i32 = jnp.int32

D_MODEL = 1024
BATCH = 8
SEQ = 2048
DEPTH = 2
N_META = 16
HEAD_DIM = 64
N_Q_HEADS = 8
N_KV_HEADS = 2
GROUP = N_Q_HEADS // N_KV_HEADS
ATTN_WIDTH = N_Q_HEADS * HEAD_DIM
KV_WIDTH = N_KV_HEADS * HEAD_DIM
WINDOW = 128
BLOCK = 128
CONV_CH = D_MODEL // 2
CONV_WIDTH = 31
N_BUCKETS = 32
MAX_DISTANCE = 128
D_FF = 2816
N_EXPERTS = 8
D_EXPERT = 1408
ALPHA = (2 * DEPTH) ** 0.25
LN_EPS = 1e-5
Q_END = ATTN_WIDTH
K_END = Q_END + KV_WIDTH
V_END = K_END + KV_WIDTH
GLU_END = V_END + 2 * CONV_CH
GA_END = GLU_END + D_MODEL
GC_END = GA_END + D_MODEL

TM = 512
N_META_ROWS = BATCH * N_META
NT = TM + BATCH * SEQ
N_TILES = NT // TM
TILES_PER_BATCH = SEQ // TM
BLOCKS_PER_TILE = TM // BLOCK
KSEG = 3 * BLOCK
CONV_HALO = 32
NEG = -1e30

CHUNK = 256
ALIGN = 16
XS_W = D_MODEL + 128
CAP = 35 * TM
FFN_GRID = 85
FF_CHUNKS_DENSE = (256,) * 11
FF_CHUNKS_MOE = (256,) * 5 + (128,)

VMEM_LIMIT = 56 * 1024 * 1024


def _cparams(sem="arbitrary"):
    return pltpu.CompilerParams(dimension_semantics=(sem,), vmem_limit_bytes=VMEM_LIMIT)


def _ln(x, g, b):
    mu = jnp.mean(x, -1, keepdims=True)
    xc = x - mu
    var = jnp.mean(xc * xc, -1, keepdims=True)
    return xc * lax.rsqrt(var + LN_EPS) * g + b


def _full(shape):
    return pl.BlockSpec(shape, lambda *_: (0,) * len(shape))


def _smem():
    return pl.BlockSpec(memory_space=pltpu.SMEM)


def _embed_tile(s, x_ref, meta_ref, g_ref, b_ref):
    m = _ln(meta_ref[...], g_ref[...], b_ref[...])
    meta_tile = jnp.concatenate([m] * BATCH + [jnp.zeros((TM - N_META_ROWS, D_MODEL), f32)], axis=0)
    return jnp.where(s == 0, meta_tile, _ln(x_ref[...], g_ref[...], b_ref[...]))


def _inproj_kernel(embed, *refs):
    s = pl.program_id(0)
    if embed:
        x_ref, meta_ref, eg_ref, eb_ref, w_ref, q_ref, k_ref, v_ref, glu_ref, h_ref, wb_ref = refs
        h = _embed_tile(s, x_ref, meta_ref, eg_ref, eb_ref)
        h_ref[...] = h
    else:
        hin_ref, w_ref, q_ref, k_ref, v_ref, glu_ref, wb_ref = refs
        h = hin_ref[...]

    @pl.when(s == 0)
    def _():
        wb_ref[...] = w_ref[...].astype(bf16)

    u = jnp.dot(h.astype(bf16), wb_ref[...], preferred_element_type=f32)
    q_ref[...] = (u[:, :Q_END] * (HEAD_DIM ** -0.5)).astype(bf16)
    k_ref[...] = u[:, Q_END:K_END].astype(bf16)
    v_ref[...] = u[:, K_END:V_END].astype(bf16)
    glu = u[:, V_END:V_END + CONV_CH] * jax.nn.sigmoid(u[:, V_END + CONV_CH:GLU_END])
    glu_ref[...] = glu.astype(bf16)


def _inproj(layer, w_in, h=None, x2d=None, meta=None, eg=None, eb=None):
    embed = h is None
    row = lambda w: pl.BlockSpec((TM, w), lambda s: (s, 0))
    w_spec = pl.BlockSpec((None, D_MODEL, GLU_END), lambda s: (layer, 0, 0), pipeline_mode=pl.Buffered(1))
    out_specs = [row(ATTN_WIDTH), row(KV_WIDTH), row(KV_WIDTH), row(CONV_CH)]
    out_shape = [
        jax.ShapeDtypeStruct((NT, ATTN_WIDTH), bf16),
        jax.ShapeDtypeStruct((NT, KV_WIDTH), bf16),
        jax.ShapeDtypeStruct((NT, KV_WIDTH), bf16),
        jax.ShapeDtypeStruct((NT, CONV_CH), bf16),
    ]
    if embed:
        in_specs = [
            pl.BlockSpec((TM, D_MODEL), lambda s: (jnp.maximum(s - 1, 0), 0)),
            _full((N_META, D_MODEL)), _full((1, D_MODEL)), _full((1, D_MODEL)), w_spec,
        ]
        args = (x2d, meta, eg, eb, w_in)
        out_specs.append(row(D_MODEL))
        out_shape.append(jax.ShapeDtypeStruct((NT, D_MODEL), f32))
    else:
        in_specs = [row(D_MODEL), w_spec]
        args = (h, w_in)
    return pl.pallas_call(
        functools.partial(_inproj_kernel, embed),
        grid=(N_TILES,),
        in_specs=in_specs,
        out_specs=out_specs,
        out_shape=out_shape,
        scratch_shapes=[pltpu.VMEM((D_MODEL, GLU_END), bf16)],
        compiler_params=_cparams("arbitrary"),
        name="inproj_glu",
    )(*args)


def _rel_bucket_np(dist):
    n = np.maximum(dist, 0)
    max_exact = N_BUCKETS // 2
    nf = np.maximum(n, 1).astype(np.float32)
    scaled = (np.log(nf / np.float32(max_exact)) / np.float32(math.log(MAX_DISTANCE / max_exact))
              * np.float32(N_BUCKETS - max_exact))
    large = np.minimum(max_exact + scaled.astype(np.int32), N_BUCKETS - 1)
    return np.where(n < max_exact, n, large).astype(np.int32)


def _bucket_tables():
    a = np.arange(BLOCK)[:, None]
    m = np.arange(BLOCK)[None, :]
    kk = np.arange(2 * BLOCK)[None, :]
    dist_band = BLOCK + a - kk
    band_ok = (dist_band >= 0) & (dist_band < WINDOW)
    band = np.where(band_ok, _rel_bucket_np(dist_band), -1)
    real = np.full((2, BLOCK, KSEG), -1, np.int32)
    for t, n in enumerate((0, 1)):
        dist_meta = N_META + n * BLOCK + a - m
        real[t, :, :BLOCK] = np.where(m < N_META, _rel_bucket_np(dist_meta), -1)
        real[t, :, BLOCK:] = band
    real[0, :, BLOCK:2 * BLOCK] = -1
    for n in range(2, SEQ // BLOCK):
        dist_meta = N_META + n * BLOCK + a - m
        assert np.array_equal(np.where(m < N_META, _rel_bucket_np(dist_meta), -1), real[1, :, :BLOCK])
    r = np.arange(N_META_ROWS)
    same = (r[:, None] // N_META) == (r[None, :] // N_META)
    dmm = (r[:, None] % N_META) - (r[None, :] % N_META)
    meta = np.where(same & (dmm >= 0), _rel_bucket_np(dmm), -1).astype(np.int32)
    return real, meta


_BK_REAL, _BK_META = _bucket_tables()


def _bias_from_buckets(bk, rb_ref, h):
    acc = jnp.full(bk.shape, NEG, f32)
    for bkt in range(N_BUCKETS):
        acc = jnp.where(bk == bkt, rb_ref[bkt, h], acc)
    return acc


def _expand_kv(x2):
    xf = x2.astype(f32)
    r = pltpu.roll(xf, HEAD_DIM, axis=1)
    low = lax.broadcasted_iota(i32, xf.shape, 1) < HEAD_DIM
    h0 = jnp.where(low, xf, r).astype(bf16)
    h1 = jnp.where(low, r, xf).astype(bf16)
    return (jnp.concatenate([h0, h0], axis=1), jnp.concatenate([h1, h1], axis=1))


def _block_diag(x, n_keys):
    lane_blk = lax.broadcasted_iota(i32, (n_keys, GROUP * HEAD_DIM), 1) // HEAD_DIM
    zero = jnp.zeros_like(x)
    return jnp.concatenate([jnp.where(lane_blk == h, x, zero) for h in range(GROUP)], axis=0)


def _attend(qg, kx, vx, bias_of_head, sink_of_head, n_keys):
    m_rows = qg.shape[0]
    kbd = _block_diag(kx, n_keys)
    vbd = _block_diag(vx, n_keys)
    s = lax.dot_general(qg, kbd, (((1,), (1,)), ((), ())), preferred_element_type=f32)
    ps, rs = [], []
    for h in range(GROUP):
        sh = s[:, h * n_keys:(h + 1) * n_keys] + bias_of_head(h)
        sink = sink_of_head(h)
        mx = jnp.maximum(jnp.max(sh, -1, keepdims=True), sink)
        p = jnp.exp(sh - mx)
        den = jnp.sum(p, -1, keepdims=True) + jnp.exp(sink - mx)
        ps.append(p.astype(bf16))
        rs.append(1.0 / den)
    pm = jnp.concatenate(ps, axis=1)
    o = jnp.dot(pm, vbd, preferred_element_type=f32)
    ol = lax.broadcasted_iota(i32, (m_rows, GROUP * HEAD_DIM), 1) // HEAD_DIM
    scale = jnp.where(ol == 0, rs[0], jnp.where(ol == 1, rs[1], jnp.where(ol == 2, rs[2], rs[3])))
    return o * scale


def _attn_kernel(rb_ref, sink_ref, q_ref, k_ref, v_ref, kp_ref, vp_ref, km_ref, vm_ref,
                 bkr_ref, bkm_ref, o_ref, br_ref, bm_ref):
    s = pl.program_id(0)
    gw = GROUP * HEAD_DIM

    @pl.when(s == 0)
    def _():
        for h in range(N_Q_HEADS):
            for t in range(2):
                br_ref[t, h] = _bias_from_buckets(bkr_ref[t], rb_ref, h)
            bm_ref[h] = _bias_from_buckets(bkm_ref[...], rb_ref, h)
        kxs = _expand_kv(k_ref[:N_META_ROWS, :])
        vxs = _expand_kv(v_ref[:N_META_ROWS, :])
        for g in range(N_KV_HEADS):
            o = _attend(q_ref[:N_META_ROWS, g * gw:(g + 1) * gw], kxs[g], vxs[g],
                        lambda h, g=g: bm_ref[g * GROUP + h],
                        lambda h, g=g: sink_ref[g * GROUP + h], N_META_ROWS)
            o_ref[:N_META_ROWS, g * gw:(g + 1) * gw] = o.astype(bf16)
        o_ref[N_META_ROWS:, :] = jnp.zeros((TM - N_META_ROWS, ATTN_WIDTH), bf16)

    @pl.when(s > 0)
    def _():
        first = ((s - 1) % TILES_PER_BATCH) == 0
        tbl0 = jnp.where(first, 0, 1)
        zpad = jnp.zeros((BLOCK - N_META, KV_WIDTH), bf16)
        kmeta = jnp.concatenate([km_ref[...], zpad], axis=0)
        vmeta = jnp.concatenate([vm_ref[...], zpad], axis=0)
        for blk in range(BLOCKS_PER_TILE):
            rows = slice(blk * BLOCK, (blk + 1) * BLOCK)
            if blk == 0:
                kprev, vprev = kp_ref[...], vp_ref[...]
            else:
                prows = slice((blk - 1) * BLOCK, blk * BLOCK)
                kprev, vprev = k_ref[prows, :], v_ref[prows, :]
            kxs = _expand_kv(jnp.concatenate([kmeta, kprev, k_ref[rows, :]], axis=0))
            vxs = _expand_kv(jnp.concatenate([vmeta, vprev, v_ref[rows, :]], axis=0))
            for g in range(N_KV_HEADS):
                if blk == 0:
                    bias = lambda h, g=g: br_ref[tbl0, g * GROUP + h]
                else:
                    bias = lambda h, g=g: br_ref[1, g * GROUP + h]
                o = _attend(q_ref[rows, g * gw:(g + 1) * gw], kxs[g], vxs[g], bias,
                            lambda h, g=g: sink_ref[g * GROUP + h], KSEG)
                o_ref[rows, g * gw:(g + 1) * gw] = o.astype(bf16)


def _attention(rel_bias, sinks, q, k, v):
    row = lambda w: pl.BlockSpec((TM, w), lambda s: (s, 0))
    prev = pl.BlockSpec((BLOCK, KV_WIDTH), lambda s: (jnp.maximum(s * BLOCKS_PER_TILE - 1, 0), 0))
    meta = pl.BlockSpec((N_META, KV_WIDTH), lambda s: (jnp.maximum(s - 1, 0) // TILES_PER_BATCH, 0))
    return pl.pallas_call(
        _attn_kernel,
        grid=(N_TILES,),
        in_specs=[
            _smem(), _smem(),
            row(ATTN_WIDTH), row(KV_WIDTH), row(KV_WIDTH),
            prev, prev, meta, meta,
            _full((2, BLOCK, KSEG)), _full((N_META_ROWS, N_META_ROWS)),
        ],
        out_specs=row(ATTN_WIDTH),
        out_shape=jax.ShapeDtypeStruct((NT, ATTN_WIDTH), bf16),
        scratch_shapes=[
            pltpu.VMEM((2, N_Q_HEADS, BLOCK, KSEG), f32),
            pltpu.VMEM((N_Q_HEADS, N_META_ROWS, N_META_ROWS), f32),
        ],
        compiler_params=_cparams("arbitrary"),
        name="swa_attention",
    )(rel_bias, sinks, q, k, v, k, v, k, v, jnp.asarray(_BK_REAL), jnp.asarray(_BK_META))


SPREAD = TM // BATCH
SH_ROWS = CONV_HALO + TM - 8


def _conv_module(s, glu_ref, gprev_ref, gmeta_ref, dw_ref, cb_ref, cg_ref, cbeta_ref,
                 win_ref, sh_ref, cv_ref):
    is_meta = s == 0
    first = ((s - 1) % TILES_PER_BATCH) == 0
    glu = glu_ref[...].astype(f32)
    zgap = jnp.zeros((SPREAD - N_META, CONV_CH), f32)
    spread = jnp.concatenate(
        [p for bb in range(BATCH) for p in (zgap, glu[bb * N_META:(bb + 1) * N_META])], axis=0)
    win_ref[CONV_HALO:, :] = jnp.where(is_meta, spread, glu)
    halo_first = jnp.concatenate(
        [jnp.zeros((CONV_HALO - N_META, CONV_CH), f32), gmeta_ref[...].astype(f32)], axis=0)
    halo = jnp.where(first, halo_first, gprev_ref[...].astype(f32))
    win_ref[:CONV_HALO, :] = jnp.where(is_meta, 0.0, halo)

    for r in range(1, 8):
        sh_ref[r - 1] = win_ref[r:r + SH_ROWS, :]
    off = CONV_HALO - (CONV_WIDTH - 1)
    for rc in range(TM // 128):
        for lc in range(CONV_CH // 128):
            ls = slice(lc * 128, (lc + 1) * 128)
            acc = None
            for t in range(CONV_WIDTH):
                r = (off + t) % 8
                r0 = rc * 128 + (off + t) - r
                x = win_ref[r0:r0 + 128, ls] if r == 0 else sh_ref[r - 1, r0:r0 + 128, ls]
                term = dw_ref[t:t + 1, ls] * x
                acc = term if acc is None else acc + term
            cv_ref[rc * 128:(rc + 1) * 128, ls] = acc
    y = _ln(cv_ref[...] + cb_ref[...], cg_ref[...], cbeta_ref[...])
    c = y * jax.nn.sigmoid(y)
    gathered = jnp.concatenate(
        [c[bb * SPREAD + SPREAD - N_META:(bb + 1) * SPREAD] for bb in range(BATCH)]
        + [jnp.zeros((TM - N_META_ROWS, CONV_CH), f32)], axis=0)
    return jnp.where(is_meta, gathered, c).astype(bf16)


def _mix_kernel(h_ref, a_ref, glu_ref, gprev_ref, gmeta_ref, dw_ref, cb_ref, cg_ref, cbeta_ref,
                wg_ref, wap_ref, wcp_ref, wo_ref, g_ref, b_ref, o_ref,
                win_ref, sh_ref, cv_ref, wgb_ref, wapb_ref, wcpb_ref, wob_ref):
    s = pl.program_id(0)

    @pl.when(s == 0)
    def _():
        wgb_ref[...] = wg_ref[...].astype(bf16)
        wapb_ref[...] = wap_ref[...].astype(bf16)
        wcpb_ref[...] = wcp_ref[...].astype(bf16)
        wob_ref[...] = wo_ref[...].astype(bf16)

    h = h_ref[...]
    gates = jnp.dot(h.astype(bf16), wgb_ref[...], preferred_element_type=f32)
    ya = jnp.dot(a_ref[...], wapb_ref[...], preferred_element_type=f32)
    c = _conv_module(s, glu_ref, gprev_ref, gmeta_ref, dw_ref, cb_ref, cg_ref, cbeta_ref,
                     win_ref, sh_ref, cv_ref)
    yc = jnp.dot(c, wcpb_ref[...], preferred_element_type=f32)
    m = jax.nn.sigmoid(gates[:, :D_MODEL]) * ya + jax.nn.sigmoid(gates[:, D_MODEL:]) * yc
    mix = jnp.dot(m.astype(bf16), wob_ref[...], preferred_element_type=f32)
    o_ref[...] = _ln(ALPHA * h + mix, g_ref[...], b_ref[...])


def _mix(layer, h, a, glu, dw, cb, cg, cbeta, w_gates, wap, wcp, wo, g, b):
    row = lambda w: pl.BlockSpec((TM, w), lambda s: (s, 0))
    gprev = pl.BlockSpec((CONV_HALO, CONV_CH),
                         lambda s: (jnp.maximum(s * (TM // CONV_HALO) - 1, 0), 0))
    gmeta = pl.BlockSpec((N_META, CONV_CH), lambda s: (jnp.maximum(s - 1, 0) // TILES_PER_BATCH, 0))
    per_layer = lambda r, c: pl.BlockSpec((None, r, c), lambda s: (layer, 0, 0), pipeline_mode=pl.Buffered(1))
    return pl.pallas_call(
        _mix_kernel,
        grid=(N_TILES,),
        in_specs=[
            row(D_MODEL), row(ATTN_WIDTH), row(CONV_CH), gprev, gmeta,
            _full((CONV_WIDTH, CONV_CH)), _full((1, CONV_CH)), _full((1, CONV_CH)), _full((1, CONV_CH)),
            _full((D_MODEL, 2 * D_MODEL)), per_layer(ATTN_WIDTH, D_MODEL),
            per_layer(CONV_CH, D_MODEL), per_layer(D_MODEL, D_MODEL),
            _full((1, D_MODEL)), _full((1, D_MODEL)),
        ],
        out_specs=row(D_MODEL),
        out_shape=jax.ShapeDtypeStruct((NT, D_MODEL), f32),
        scratch_shapes=[
            pltpu.VMEM((CONV_HALO + TM, CONV_CH), f32),
            pltpu.VMEM((7, SH_ROWS, CONV_CH), f32),
            pltpu.VMEM((TM, CONV_CH), f32),
            pltpu.VMEM((D_MODEL, 2 * D_MODEL), bf16),
            pltpu.VMEM((ATTN_WIDTH, D_MODEL), bf16),
            pltpu.VMEM((CONV_CH, D_MODEL), bf16),
            pltpu.VMEM((D_MODEL, D_MODEL), bf16),
        ],
        compiler_params=_cparams("arbitrary"),
        name="conv_mix_ln1",
    )(h, a, glu, glu, glu, dw, cb, cg, cbeta, w_gates, wap, wcp, wo, g, b)


def _swiglu_chunks(xb, wg_ref, wu_ref, wd_ref, chunks, row_scale=None):
    acc = None
    c0 = 0
    for w in chunks:
        gt = jnp.dot(xb, wg_ref[:, c0:c0 + w].astype(bf16), preferred_element_type=f32)
        up = jnp.dot(xb, wu_ref[:, c0:c0 + w].astype(bf16), preferred_element_type=f32)
        act = gt * jax.nn.sigmoid(gt) * up
        if row_scale is not None:
            act = act * row_scale
        part = jnp.dot(act.astype(bf16), wd_ref[c0:c0 + w, :].astype(bf16), preferred_element_type=f32)
        acc = part if acc is None else acc + part
        c0 += w
    return acc


def _ffn_kernel(h_ref, wg_ref, wu_ref, wd_ref, g_ref, b_ref, o_ref):
    h = h_ref[...]
    f = _swiglu_chunks(h.astype(bf16), wg_ref, wu_ref, wd_ref, FF_CHUNKS_DENSE)
    o_ref[...] = _ln(ALPHA * h + f, g_ref[...], b_ref[...])


def _dense_ffn(j, h, wg, wu, wd, g, b):
    row = pl.BlockSpec((TM, D_MODEL), lambda s: (s, 0))
    per_layer = lambda r, c: pl.BlockSpec((None, r, c), lambda s: (j, 0, 0), pipeline_mode=pl.Buffered(1))
    return pl.pallas_call(
        _ffn_kernel,
        grid=(N_TILES,),
        in_specs=[row, per_layer(D_MODEL, D_FF), per_layer(D_MODEL, D_FF), per_layer(D_FF, D_MODEL),
                  _full((1, D_MODEL)), _full((1, D_MODEL))],
        out_specs=row,
        out_shape=jax.ShapeDtypeStruct((NT, D_MODEL), f32),
        compiler_params=_cparams("parallel"),
        name="dense_ffn_ln2",
    )(h, wg, wu, wd, g, b)


def _route_copy(xbuf_ref, xs_ref, sem_ref, slot, e, base):
    return pltpu.make_async_copy(
        xbuf_ref.at[slot, pl.ds(e * CHUNK, CHUNK), :],
        xs_ref.at[pl.ds(pl.multiple_of(e * CAP + base, ALIGN), CHUNK), :],
        sem_ref.at[slot, e])


def _zero_copy(zbuf_ref, xs_ref, sem_ref, e, base):
    return pltpu.make_async_copy(
        zbuf_ref,
        xs_ref.at[pl.ds(pl.multiple_of(e * CAP + base, ALIGN), TM), :],
        sem_ref.at[0, e])


def _route_kernel(h_ref, r_ref, info_ref, infot_ref, base_ref, fill_ref, xs_ref,
                  xbuf_ref, zbuf_ref, cnt_ref, sem_ref):
    s = pl.program_id(0)
    slot = s % 2

    @pl.when(s == 0)
    def _():
        for e in range(N_EXPERTS):
            cnt_ref[e] = 0
        zbuf_ref[...] = jnp.zeros((TM, XS_W), bf16)

    h = h_ref[...]
    hi = h.astype(bf16)
    lo = (h - hi.astype(f32)).astype(bf16)
    pa = jnp.dot(hi, r_ref[...], preferred_element_type=f32)
    pb = jnp.dot(lo, r_ref[...], preferred_element_type=f32)
    logits = pa + pltpu.roll(pa, 128 - N_EXPERTS, axis=1) + pb
    lt = logits.T[:N_EXPERTS, :]

    eid = lax.broadcasted_iota(i32, (N_EXPERTS, TM), 0)
    m1 = jnp.max(lt, axis=0, keepdims=True)
    i1 = jnp.min(jnp.where(lt == m1, eid, N_EXPERTS), axis=0, keepdims=True)
    lt2 = jnp.where(eid == i1, NEG, lt)
    m2 = jnp.max(lt2, axis=0, keepdims=True)
    i2 = jnp.min(jnp.where(lt2 == m2, eid, N_EXPERTS), axis=0, keepdims=True)
    ex = jnp.exp(m2 - m1)
    g1 = 1.0 / (1.0 + ex)
    g2 = ex * g1
    tok = lax.broadcasted_iota(i32, (1, TM), 1)
    valid = (s > 0) | (tok < N_META_ROWS)
    sel1 = (eid == i1) & valid
    sel2 = (eid == i2) & valid
    sel = sel1 | sel2
    onehot = jnp.where(sel, 1.0, 0.0)
    tri = jnp.where(lax.broadcasted_iota(i32, (TM, TM), 0) < lax.broadcasted_iota(i32, (TM, TM), 1), 1.0, 0.0)
    rank = jnp.dot(onehot, tri, preferred_element_type=f32)
    gate = jnp.where(sel1, g1, jnp.where(sel2, g2, 0.0))
    rank1 = jnp.sum(jnp.where(sel1, rank, 0.0), axis=0, keepdims=True)
    rank2 = jnp.sum(jnp.where(sel2, rank, 0.0), axis=0, keepdims=True)
    c1 = jnp.where(valid, i1.astype(f32) * CHUNK + rank1, -1.0)
    c2 = jnp.where(valid, i2.astype(f32) * CHUNK + rank2, -1.0)
    info = jnp.concatenate([c1, c2, jnp.zeros((128 - 2, TM), f32)], axis=0)
    infot_ref[...] = info[:8, :]
    info_ref[...] = info.T

    slot_id = lax.broadcasted_iota(i32, (CHUNK, TM), 0)
    lane = lax.broadcasted_iota(i32, (CHUNK, 128), 1)
    ranki = rank.astype(i32)
    ps = []
    for e in range(N_EXPERTS):
        pe = (slot_id == ranki[e:e + 1, :]) & sel[e:e + 1, :]
        ps.append(jnp.where(pe, 1.0, 0.0).astype(bf16))
        gs = jnp.sum(jnp.where(pe, gate[e:e + 1, :], 0.0), axis=1, keepdims=True)
        g_hi = gs.astype(bf16).astype(f32)
        g_mid = (gs - g_hi).astype(bf16).astype(f32)
        g_lo = gs - g_hi - g_mid
        parts = jnp.where(lane == 0, g_hi, jnp.where(lane == 1, g_mid, jnp.where(lane == 2, g_lo, 0.0)))
        xbuf_ref[slot, e * CHUNK:(e + 1) * CHUNK, D_MODEL:] = parts.astype(bf16)
    pall = jnp.concatenate(ps, axis=0)
    xbuf_ref[slot, :, :D_MODEL] = jnp.dot(pall, hi, preferred_element_type=f32).astype(bf16)

    @pl.when(s > 0)
    def _():
        for e in range(N_EXPERTS):
            _route_copy(xbuf_ref, xs_ref, sem_ref, 1 - slot, e, 0).wait()

    bases = []
    for e in range(N_EXPERTS):
        base = cnt_ref[e]
        bases.append(base)
        base_ref[s, e] = base
        _route_copy(xbuf_ref, xs_ref, sem_ref, slot, e, base).start()
        n_e = jnp.sum(onehot[e:e + 1, :]).astype(i32)
        cnt_ref[e] = base + ((n_e + (ALIGN - 1)) // ALIGN) * ALIGN

    @pl.when(s == N_TILES - 1)
    def _():
        for e in range(N_EXPERTS):
            _route_copy(xbuf_ref, xs_ref, sem_ref, slot, e, 0).wait()
        for e in range(N_EXPERTS):
            end = bases[e] + CHUNK
            fill_ref[e] = end
            _zero_copy(zbuf_ref, xs_ref, sem_ref, e, end).start()
        for e in range(N_EXPERTS):
            _zero_copy(zbuf_ref, xs_ref, sem_ref, e, 0).wait()


def _route(h, rsplit):
    return pl.pallas_call(
        _route_kernel,
        grid=(N_TILES,),
        in_specs=[pl.BlockSpec((TM, D_MODEL), lambda s: (s, 0)), _full((D_MODEL, 128))],
        out_specs=[
            pl.BlockSpec((TM, 128), lambda s: (s, 0)),
            pl.BlockSpec((None, 8, TM), lambda s: (s, 0, 0)),
            _smem(), _smem(),
            pl.BlockSpec(memory_space=pl.ANY),
        ],
        out_shape=[
            jax.ShapeDtypeStruct((NT, 128), f32),
            jax.ShapeDtypeStruct((N_TILES, 8, TM), f32),
            jax.ShapeDtypeStruct((N_TILES, N_EXPERTS), i32),
            jax.ShapeDtypeStruct((N_EXPERTS,), i32),
            jax.ShapeDtypeStruct((N_EXPERTS * CAP, XS_W), bf16),
        ],
        scratch_shapes=[
            pltpu.VMEM((2, N_EXPERTS * CHUNK, XS_W), bf16),
            pltpu.VMEM((TM, XS_W), bf16),
            pltpu.SMEM((N_EXPERTS,), i32),
            pltpu.SemaphoreType.DMA((2, N_EXPERTS)),
        ],
        compiler_params=_cparams("arbitrary"),
        name="moe_route_dispatch",
    )(h, rsplit)


def _expert_kernel(te_ref, tb_ref, tv_ref, x_ref, wg_ref, wu_ref, wd_ref, y_ref):
    t = pl.program_id(0)

    @pl.when(tv_ref[t] > 0)
    def _():
        x = x_ref[...]
        gate = jnp.sum(x[:, D_MODEL:].astype(f32), axis=1, keepdims=True)
        y = _swiglu_chunks(x[:, :D_MODEL], wg_ref, wu_ref, wd_ref, FF_CHUNKS_MOE, row_scale=gate)
        y_ref[...] = y.astype(bf16)


def _expert_ffn(j, tile_e, tile_blk, tile_valid, xs, wg, wu, wd):
    per_expert = lambda r, c: pl.BlockSpec((None, None, r, c), lambda t, te, tb, tv: (j, te[t], 0, 0))
    grid_spec = pltpu.PrefetchScalarGridSpec(
        num_scalar_prefetch=3,
        grid=(FFN_GRID,),
        in_specs=[
            pl.BlockSpec((TM, XS_W), lambda t, te, tb, tv: (tb[t], 0)),
            per_expert(D_MODEL, D_EXPERT), per_expert(D_MODEL, D_EXPERT), per_expert(D_EXPERT, D_MODEL),
        ],
        out_specs=pl.BlockSpec((TM, D_MODEL), lambda t, te, tb, tv: (tb[t], 0)),
    )
    return pl.pallas_call(
        _expert_kernel,
        grid_spec=grid_spec,
        out_shape=jax.ShapeDtypeStruct((N_EXPERTS * CAP, D_MODEL), bf16),
        compiler_params=_cparams("arbitrary"),
        name="moe_expert_ffn",
    )(tile_e, tile_blk, tile_valid, xs, wg, wu, wd)


def _combine_copy(ys_ref, ybuf_ref, sem_ref, e, base):
    return pltpu.make_async_copy(
        ys_ref.at[pl.ds(pl.multiple_of(e * CAP + base, ALIGN), CHUNK), :],
        ybuf_ref.at[pl.ds(e * CHUNK, CHUNK), :],
        sem_ref.at[e])


def _combine_kernel(base_ref, h_ref, info_ref, ys_ref, g_ref, b_ref, o_ref, ybuf_ref, sem_ref):
    s = pl.program_id(0) + 1
    for e in range(N_EXPERTS):
        _combine_copy(ys_ref, ybuf_ref, sem_ref, e, base_ref[s, e]).start()
    info = info_ref[...]
    c1 = info[:, 0:1].astype(i32)
    c2 = info[:, 1:2].astype(i32)
    col = lax.broadcasted_iota(i32, (TM, N_EXPERTS * CHUNK), 1)
    qm = jnp.where((col == c1) | (col == c2), 1.0, 0.0).astype(bf16)
    for e in range(N_EXPERTS):
        _combine_copy(ys_ref, ybuf_ref, sem_ref, e, 0).wait()
    f = jnp.dot(qm, ybuf_ref[...], preferred_element_type=f32)
    o_ref[...] = _ln(ALPHA * h_ref[...] + f, g_ref[...], b_ref[...])


def _combine(base, h, info, ys, g, b):
    n_out = N_TILES - 1
    grid_spec = pltpu.PrefetchScalarGridSpec(
        num_scalar_prefetch=1,
        grid=(n_out,),
        in_specs=[
            pl.BlockSpec((TM, D_MODEL), lambda s, base: (s + 1, 0)),
            pl.BlockSpec((TM, 128), lambda s, base: (s + 1, 0)),
            pl.BlockSpec(memory_space=pl.ANY),
            pl.BlockSpec((1, D_MODEL), lambda s, base: (0, 0)),
            pl.BlockSpec((1, D_MODEL), lambda s, base: (0, 0)),
        ],
        out_specs=pl.BlockSpec((TM, D_MODEL), lambda s, base: (s, 0)),
        scratch_shapes=[
            pltpu.VMEM((N_EXPERTS * CHUNK, D_MODEL), bf16),
            pltpu.SemaphoreType.DMA((N_EXPERTS,)),
        ],
    )
    return pl.pallas_call(
        _combine_kernel,
        grid_spec=grid_spec,
        out_shape=jax.ShapeDtypeStruct((n_out * TM, D_MODEL), f32),
        compiler_params=_cparams("arbitrary"),
        name="moe_combine_ln2",
    )(base, h, info, ys, g, b)


def _tile_map(fill):
    n_tiles = (fill + TM - 1) // TM
    ends = jnp.cumsum(n_tiles)
    starts = ends - n_tiles
    t = jnp.arange(FFN_GRID, dtype=i32)
    e = jnp.minimum(jnp.sum((t[:, None] >= ends[None, :]).astype(i32), axis=1), N_EXPERTS - 1)
    valid = (t < ends[-1]).astype(i32)
    blk = e * (CAP // TM) + (t - starts[e])
    last = jnp.maximum(ends[-1] - 1, 0)
    e_last = jnp.minimum(jnp.sum((last >= ends).astype(i32)), N_EXPERTS - 1)
    blk_last = e_last * (CAP // TM) + (last - starts[e_last])
    e = jnp.where(valid > 0, e, e_last).astype(i32)
    blk = jnp.where(valid > 0, blk, blk_last).astype(i32)
    return e, blk, valid


def _split2(x):
    hi = x.astype(bf16)
    lo = (x - hi.astype(f32)).astype(bf16)
    return hi, lo


def kernel(x, meta_tokens, emb_ln_g, emb_ln_b, rel_bias, w_in, conv_dw, conv_b, conv_ln_g, conv_ln_b, sinks, w_attn_proj, w_conv_proj, w_out, ln1_g, ln1_b, ffn_w_gate, ffn_w_up, ffn_w_down, router, moe_w_gate, moe_w_up, moe_w_down, ln2_g, ln2_b):
    row = lambda v: v.reshape(1, -1)
    h = None
    out = None
    for i in range(DEPTH):
        if i == 0:
            q, k, v, glu, h = _inproj(i, w_in, x2d=x.reshape(BATCH * SEQ, D_MODEL), meta=meta_tokens,
                                      eg=row(emb_ln_g), eb=row(emb_ln_b))
        else:
            q, k, v, glu = _inproj(i, w_in, h=h)
        a = _attention(rel_bias, sinks[i], q, k, v)
        h = _mix(i, h, a, glu, conv_dw[i], row(conv_b[i]), row(conv_ln_g[i]), row(conv_ln_b[i]),
                 w_in[i, :, GLU_END:], w_attn_proj, w_conv_proj, w_out, row(ln1_g[i]), row(ln1_b[i]))
        j = i // 2
        if i % 2 == 0:
            h = _dense_ffn(j, h, ffn_w_gate, ffn_w_up, ffn_w_down, row(ln2_g[i]), row(ln2_b[i]))
        else:
            r_hi, r_lo = _split2(router[j])
            rsplit = jnp.concatenate(
                [r_hi, r_lo, jnp.zeros((D_MODEL, 128 - 2 * N_EXPERTS), bf16)], axis=1)
            info, _, base, fill, xs = _route(h, rsplit)
            tile_e, tile_blk, tile_valid = _tile_map(fill)
            ys = _expert_ffn(j, tile_e, tile_blk, tile_valid, xs, moe_w_gate, moe_w_up, moe_w_down)
            out = _combine(base, h, info, ys, row(ln2_g[i]), row(ln2_b[i]))
    return out.reshape(BATCH, SEQ, D_MODEL)
```

```python
import functools
import math

import numpy as np
import jax
import jax.numpy as jnp
from jax import lax
from jax.experimental import pallas as pl
from jax.experimental.pallas import tpu as pltpu

f32 = jnp.float32
bf16 = jnp.bfloat16
i32 = jnp.int32

D_MODEL = 1024
BATCH = 8
SEQ = 2048
DEPTH = 2
N_META = 16
HEAD_DIM = 64
N_Q_HEADS = 8
N_KV_HEADS = 2
GROUP = N_Q_HEADS // N_KV_HEADS
ATTN_WIDTH = N_Q_HEADS * HEAD_DIM
KV_WIDTH = N_KV_HEADS * HEAD_DIM
WINDOW = 128
BLOCK = 128
CONV_CH = D_MODEL // 2
CONV_WIDTH = 31
N_BUCKETS = 32
MAX_DISTANCE = 128
D_FF = 2816
N_EXPERTS = 8
D_EXPERT = 1408
ALPHA = (2 * DEPTH) ** 0.25
LN_EPS = 1e-5
Q_END = ATTN_WIDTH
K_END = Q_END + KV_WIDTH
V_END = K_END + KV_WIDTH
GLU_END = V_END + 2 * CONV_CH
GA_END = GLU_END + D_MODEL
GC_END = GA_END + D_MODEL

TM = 512
N_META_ROWS = BATCH * N_META
NT = TM + BATCH * SEQ
N_TILES = NT // TM
TILES_PER_BATCH = SEQ // TM
BLOCKS_PER_TILE = TM // BLOCK
KSEG = 3 * BLOCK
CONV_HALO = 32
NEG = -1e30

CHUNK = 192
N_PASS = -(-TM // CHUNK)
ALIGN = 16
XS_W = D_MODEL + 128
CAP = 35 * TM
FFN_GRID = 85
FF_CHUNKS_DENSE = (256,) * 11
FF_CHUNKS_MOE = (256,) * 5 + (128,)

VMEM_LIMIT = 56 * 1024 * 1024


def _cparams(sem="arbitrary", flags=None):
    return pltpu.CompilerParams(dimension_semantics=(sem,), vmem_limit_bytes=VMEM_LIMIT, flags=flags)


def _ln(x, g, b):
    mu = jnp.mean(x, -1, keepdims=True)
    xc = x - mu
    var = jnp.mean(xc * xc, -1, keepdims=True)
    return xc * lax.rsqrt(var + LN_EPS) * g + b


def _full(shape):
    return pl.BlockSpec(shape, lambda *_: (0,) * len(shape))


def _smem():
    return pl.BlockSpec(memory_space=pltpu.SMEM)


def _embed_tile(s, x_ref, meta_ref, g_ref, b_ref):
    m = _ln(meta_ref[...], g_ref[...], b_ref[...])
    meta_tile = jnp.concatenate([m] * BATCH + [jnp.zeros((TM - N_META_ROWS, D_MODEL), f32)], axis=0)
    return jnp.where(s == 0, meta_tile, _ln(x_ref[...], g_ref[...], b_ref[...]))


def _inproj_kernel(embed, *refs):
    s = pl.program_id(0)
    if embed:
        x_ref, meta_ref, eg_ref, eb_ref, w_ref, q_ref, k_ref, v_ref, glu_ref, h_ref, wb_ref = refs
        h = _embed_tile(s, x_ref, meta_ref, eg_ref, eb_ref)
        h_ref[...] = h
    else:
        hin_ref, w_ref, q_ref, k_ref, v_ref, glu_ref, wb_ref = refs
        h = hin_ref[...]

    @pl.when(s == 0)
    def _():
        wb_ref[...] = w_ref[...].astype(bf16)

    u = jnp.dot(h.astype(bf16), wb_ref[...], preferred_element_type=f32)
    q_ref[...] = (u[:, :Q_END] * (HEAD_DIM ** -0.5)).astype(bf16)
    k_ref[...] = u[:, Q_END:K_END].astype(bf16)
    v_ref[...] = u[:, K_END:V_END].astype(bf16)
    glu = u[:, V_END:V_END + CONV_CH] * jax.nn.sigmoid(u[:, V_END + CONV_CH:GLU_END])
    glu_ref[...] = glu.astype(bf16)


def _inproj(layer, w_in, h=None, x2d=None, meta=None, eg=None, eb=None):
    embed = h is None
    row = lambda w: pl.BlockSpec((TM, w), lambda s: (s, 0))
    w_spec = pl.BlockSpec((None, D_MODEL, GLU_END), lambda s: (layer, 0, 0), pipeline_mode=pl.Buffered(1))
    out_specs = [row(ATTN_WIDTH), row(KV_WIDTH), row(KV_WIDTH), row(CONV_CH)]
    out_shape = [
        jax.ShapeDtypeStruct((NT, ATTN_WIDTH), bf16),
        jax.ShapeDtypeStruct((NT, KV_WIDTH), bf16),
        jax.ShapeDtypeStruct((NT, KV_WIDTH), bf16),
        jax.ShapeDtypeStruct((NT, CONV_CH), bf16),
    ]
    if embed:
        in_specs = [
            pl.BlockSpec((TM, D_MODEL), lambda s: (jnp.maximum(s - 1, 0), 0)),
            _full((N_META, D_MODEL)), _full((1, D_MODEL)), _full((1, D_MODEL)), w_spec,
        ]
        args = (x2d, meta, eg, eb, w_in)
        out_specs.append(row(D_MODEL))
        out_shape.append(jax.ShapeDtypeStruct((NT, D_MODEL), f32))
    else:
        in_specs = [row(D_MODEL), w_spec]
        args = (h, w_in)
    return pl.pallas_call(
        functools.partial(_inproj_kernel, embed),
        grid=(N_TILES,),
        in_specs=in_specs,
        out_specs=out_specs,
        out_shape=out_shape,
        scratch_shapes=[pltpu.VMEM((D_MODEL, GLU_END), bf16)],
        compiler_params=_cparams("arbitrary"),
        name="inproj_glu",
    )(*args)


def _rel_bucket_np(dist):
    n = np.maximum(dist, 0)
    max_exact = N_BUCKETS // 2
    nf = np.maximum(n, 1).astype(np.float32)
    scaled = (np.log(nf / np.float32(max_exact)) / np.float32(math.log(MAX_DISTANCE / max_exact))
              * np.float32(N_BUCKETS - max_exact))
    large = np.minimum(max_exact + scaled.astype(np.int32), N_BUCKETS - 1)
    return np.where(n < max_exact, n, large).astype(np.int32)


def _bucket_tables():
    a = np.arange(BLOCK)[:, None]
    m = np.arange(BLOCK)[None, :]
    kk = np.arange(2 * BLOCK)[None, :]
    dist_band = BLOCK + a - kk
    band_ok = (dist_band >= 0) & (dist_band < WINDOW)
    band = np.where(band_ok, _rel_bucket_np(dist_band), -1)
    real = np.full((2, BLOCK, KSEG), -1, np.int32)
    for t, n in enumerate((0, 1)):
        dist_meta = N_META + n * BLOCK + a - m
        real[t, :, :BLOCK] = np.where(m < N_META, _rel_bucket_np(dist_meta), -1)
        real[t, :, BLOCK:] = band
    real[0, :, BLOCK:2 * BLOCK] = -1
    for n in range(2, SEQ // BLOCK):
        dist_meta = N_META + n * BLOCK + a - m
        assert np.array_equal(np.where(m < N_META, _rel_bucket_np(dist_meta), -1), real[1, :, :BLOCK])
    r = np.arange(N_META_ROWS)
    same = (r[:, None] // N_META) == (r[None, :] // N_META)
    dmm = (r[:, None] % N_META) - (r[None, :] % N_META)
    meta = np.where(same & (dmm >= 0), _rel_bucket_np(dmm), -1).astype(np.int32)
    return real, meta


_BK_REAL, _BK_META = _bucket_tables()


def _bias_from_buckets(bk, rb_ref, h):
    acc = jnp.full(bk.shape, NEG, f32)
    for bkt in range(N_BUCKETS):
        acc = jnp.where(bk == bkt, rb_ref[bkt, h], acc)
    return acc


def _expand_kv(x2):
    xf = x2.astype(f32)
    r = pltpu.roll(xf, HEAD_DIM, axis=1)
    low = lax.broadcasted_iota(i32, xf.shape, 1) < HEAD_DIM
    h0 = jnp.where(low, xf, r).astype(bf16)
    h1 = jnp.where(low, r, xf).astype(bf16)
    return (jnp.concatenate([h0, h0], axis=1), jnp.concatenate([h1, h1], axis=1))


def _block_diag(x, n_keys):
    lane_blk = lax.broadcasted_iota(i32, (n_keys, GROUP * HEAD_DIM), 1) // HEAD_DIM
    zero = jnp.zeros_like(x)
    return jnp.concatenate([jnp.where(lane_blk == h, x, zero) for h in range(GROUP)], axis=0)


def _attend(qg, kx, vx, bias_of_head, sink_of_head, n_keys):
    m_rows = qg.shape[0]
    kbd = _block_diag(kx, n_keys)
    vbd = _block_diag(vx, n_keys)
    s = lax.dot_general(qg, kbd, (((1,), (1,)), ((), ())), preferred_element_type=f32)
    ps, rs = [], []
    for h in range(GROUP):
        sh = s[:, h * n_keys:(h + 1) * n_keys] + bias_of_head(h)
        sink = sink_of_head(h)
        mx = jnp.maximum(jnp.max(sh, -1, keepdims=True), sink)
        p = jnp.exp(sh - mx)
        den = jnp.sum(p, -1, keepdims=True) + jnp.exp(sink - mx)
        ps.append(p.astype(bf16))
        rs.append(1.0 / den)
    pm = jnp.concatenate(ps, axis=1)
    o = jnp.dot(pm, vbd, preferred_element_type=f32)
    ol = lax.broadcasted_iota(i32, (m_rows, GROUP * HEAD_DIM), 1) // HEAD_DIM
    scale = jnp.where(ol == 0, rs[0], jnp.where(ol == 1, rs[1], jnp.where(ol == 2, rs[2], rs[3])))
    return o * scale


def _attn_kernel(rb_ref, sink_ref, q_ref, k_ref, v_ref, kp_ref, vp_ref, km_ref, vm_ref,
                 bkr_ref, bkm_ref, o_ref, br_ref, bm_ref):
    s = pl.program_id(0)
    gw = GROUP * HEAD_DIM

    @pl.when(s == 0)
    def _():
        for h in range(N_Q_HEADS):
            for t in range(2):
                br_ref[t, h] = _bias_from_buckets(bkr_ref[t], rb_ref, h)
            bm_ref[h] = _bias_from_buckets(bkm_ref[...], rb_ref, h)
        kxs = _expand_kv(k_ref[:N_META_ROWS, :])
        vxs = _expand_kv(v_ref[:N_META_ROWS, :])
        for g in range(N_KV_HEADS):
            o = _attend(q_ref[:N_META_ROWS, g * gw:(g + 1) * gw], kxs[g], vxs[g],
                        lambda h, g=g: bm_ref[g * GROUP + h],
                        lambda h, g=g: sink_ref[g * GROUP + h], N_META_ROWS)
            o_ref[:N_META_ROWS, g * gw:(g + 1) * gw] = o.astype(bf16)
        o_ref[N_META_ROWS:, :] = jnp.zeros((TM - N_META_ROWS, ATTN_WIDTH), bf16)

    @pl.when(s > 0)
    def _():
        first = ((s - 1) % TILES_PER_BATCH) == 0
        tbl0 = jnp.where(first, 0, 1)
        zpad = jnp.zeros((BLOCK - N_META, KV_WIDTH), bf16)
        kmeta = jnp.concatenate([km_ref[...], zpad], axis=0)
        vmeta = jnp.concatenate([vm_ref[...], zpad], axis=0)
        for blk in range(BLOCKS_PER_TILE):
            rows = slice(blk * BLOCK, (blk + 1) * BLOCK)
            if blk == 0:
                kprev, vprev = kp_ref[...], vp_ref[...]
            else:
                prows = slice((blk - 1) * BLOCK, blk * BLOCK)
                kprev, vprev = k_ref[prows, :], v_ref[prows, :]
            kxs = _expand_kv(jnp.concatenate([kmeta, kprev, k_ref[rows, :]], axis=0))
            vxs = _expand_kv(jnp.concatenate([vmeta, vprev, v_ref[rows, :]], axis=0))
            for g in range(N_KV_HEADS):
                if blk == 0:
                    bias = lambda h, g=g: br_ref[tbl0, g * GROUP + h]
                else:
                    bias = lambda h, g=g: br_ref[1, g * GROUP + h]
                o = _attend(q_ref[rows, g * gw:(g + 1) * gw], kxs[g], vxs[g], bias,
                            lambda h, g=g: sink_ref[g * GROUP + h], KSEG)
                o_ref[rows, g * gw:(g + 1) * gw] = o.astype(bf16)


def _attention(rel_bias, sinks, q, k, v):
    row = lambda w: pl.BlockSpec((TM, w), lambda s: (s, 0))
    prev = pl.BlockSpec((BLOCK, KV_WIDTH), lambda s: (jnp.maximum(s * BLOCKS_PER_TILE - 1, 0), 0))
    meta = pl.BlockSpec((N_META, KV_WIDTH), lambda s: (jnp.maximum(s - 1, 0) // TILES_PER_BATCH, 0))
    return pl.pallas_call(
        _attn_kernel,
        grid=(N_TILES,),
        in_specs=[
            _smem(), _smem(),
            row(ATTN_WIDTH), row(KV_WIDTH), row(KV_WIDTH),
            prev, prev, meta, meta,
            _full((2, BLOCK, KSEG)), _full((N_META_ROWS, N_META_ROWS)),
        ],
        out_specs=row(ATTN_WIDTH),
        out_shape=jax.ShapeDtypeStruct((NT, ATTN_WIDTH), bf16),
        scratch_shapes=[
            pltpu.VMEM((2, N_Q_HEADS, BLOCK, KSEG), f32),
            pltpu.VMEM((N_Q_HEADS, N_META_ROWS, N_META_ROWS), f32),
        ],
        compiler_params=_cparams("arbitrary"),
        name="swa_attention",
    )(rel_bias, sinks, q, k, v, k, v, k, v, jnp.asarray(_BK_REAL), jnp.asarray(_BK_META))


SPREAD = TM // BATCH
SH_ROWS = CONV_HALO + TM - 8
MIX_COLS = 256


def _conv_window(s, glu_ref, gprev_ref, gmeta_ref, win_ref, sh_ref):
    is_meta = s == 0
    first = ((s - 1) % TILES_PER_BATCH) == 0
    glu = glu_ref[...].astype(f32)
    zgap = jnp.zeros((SPREAD - N_META, CONV_CH), f32)
    spread = jnp.concatenate(
        [p for bb in range(BATCH) for p in (zgap, glu[bb * N_META:(bb + 1) * N_META])], axis=0)
    win_ref[CONV_HALO:, :] = jnp.where(is_meta, spread, glu)
    halo_first = jnp.concatenate(
        [jnp.zeros((CONV_HALO - N_META, CONV_CH), f32), gmeta_ref[...].astype(f32)], axis=0)
    halo = jnp.where(first, halo_first, gprev_ref[...].astype(f32))
    win_ref[:CONV_HALO, :] = jnp.where(is_meta, 0.0, halo)

    for r in range(1, 8):
        sh_ref[r - 1] = win_ref[r:r + SH_ROWS, :]


def _conv_finish(s, dw_ref, cb_ref, cg_ref, cbeta_ref, win_ref, sh_ref, cv_ref):
    off = CONV_HALO - (CONV_WIDTH - 1)
    for rc in range(TM // 128):
        for lc in range(CONV_CH // 128):
            ls = slice(lc * 128, (lc + 1) * 128)
            acc = None
            for t in range(CONV_WIDTH):
                r = (off + t) % 8
                r0 = rc * 128 + (off + t) - r
                x = win_ref[r0:r0 + 128, ls] if r == 0 else sh_ref[r - 1, r0:r0 + 128, ls]
                term = dw_ref[t:t + 1, ls] * x
                acc = term if acc is None else acc + term
            cv_ref[rc * 128:(rc + 1) * 128, ls] = acc
    y = _ln(cv_ref[...] + cb_ref[...], cg_ref[...], cbeta_ref[...])
    c = y * jax.nn.sigmoid(y)
    gathered = jnp.concatenate(
        [c[bb * SPREAD + SPREAD - N_META:(bb + 1) * SPREAD] for bb in range(BATCH)]
        + [jnp.zeros((TM - N_META_ROWS, CONV_CH), f32)], axis=0)
    return jnp.where(s == 0, gathered, c).astype(bf16)


def _mix_kernel(h_ref, a_ref, glu_ref, gprev_ref, gmeta_ref, dw_ref, cb_ref, cg_ref, cbeta_ref,
                wg_ref, wap_ref, wcp_ref, wo_ref, g_ref, b_ref, o_ref,
                win_ref, sh_ref, cv_ref, wgb_ref, wapb_ref, wcpb_ref, wob_ref):
    s = pl.program_id(0)

    @pl.when(s == 0)
    def _():
        wgb_ref[...] = wg_ref[...].astype(bf16)
        wapb_ref[...] = wap_ref[...].astype(bf16)
        wcpb_ref[...] = wcp_ref[...].astype(bf16)
        wob_ref[...] = wo_ref[...].astype(bf16)

    h = h_ref[...]
    gates = jnp.dot(h.astype(bf16), wgb_ref[...], preferred_element_type=f32)
    ya = jnp.dot(a_ref[...], wapb_ref[...], preferred_element_type=f32)
    _conv_window(s, glu_ref, gprev_ref, gmeta_ref, win_ref, sh_ref)
    c = _conv_finish(s, dw_ref, cb_ref, cg_ref, cbeta_ref, win_ref, sh_ref, cv_ref)
    yc = jnp.dot(c, wcpb_ref[...], preferred_element_type=f32)
    m = jax.nn.sigmoid(gates[:, :D_MODEL]) * ya + jax.nn.sigmoid(gates[:, D_MODEL:]) * yc
    mix = jnp.dot(m.astype(bf16), wob_ref[...], preferred_element_type=f32)
    o_ref[...] = _ln(ALPHA * h + mix, g_ref[...], b_ref[...])


def _mix(layer, h, a, glu, dw, cb, cg, cbeta, w_gates, wap, wcp, wo, g, b):
    row = lambda w: pl.BlockSpec((TM, w), lambda s: (s, 0))
    gprev = pl.BlockSpec((CONV_HALO, CONV_CH),
                         lambda s: (jnp.maximum(s * (TM // CONV_HALO) - 1, 0), 0))
    gmeta = pl.BlockSpec((N_META, CONV_CH), lambda s: (jnp.maximum(s - 1, 0) // TILES_PER_BATCH, 0))
    per_layer = lambda r, c: pl.BlockSpec((None, r, c), lambda s: (layer, 0, 0), pipeline_mode=pl.Buffered(1))
    return pl.pallas_call(
        _mix_kernel,
        grid=(N_TILES,),
        in_specs=[
            row(D_MODEL), row(ATTN_WIDTH), row(CONV_CH), gprev, gmeta,
            _full((CONV_WIDTH, CONV_CH)), _full((1, CONV_CH)), _full((1, CONV_CH)), _full((1, CONV_CH)),
            _full((D_MODEL, 2 * D_MODEL)), per_layer(ATTN_WIDTH, D_MODEL),
            per_layer(CONV_CH, D_MODEL), per_layer(D_MODEL, D_MODEL),
            _full((1, D_MODEL)), _full((1, D_MODEL)),
        ],
        out_specs=row(D_MODEL),
        out_shape=jax.ShapeDtypeStruct((NT, D_MODEL), f32),
        scratch_shapes=[
            pltpu.VMEM((CONV_HALO + TM, CONV_CH), f32),
            pltpu.VMEM((7, SH_ROWS, CONV_CH), f32),
            pltpu.VMEM((TM, CONV_CH), f32),
            pltpu.VMEM((D_MODEL, 2 * D_MODEL), bf16),
            pltpu.VMEM((ATTN_WIDTH, D_MODEL), bf16),
            pltpu.VMEM((CONV_CH, D_MODEL), bf16),
            pltpu.VMEM((D_MODEL, D_MODEL), bf16),
        ],
        compiler_params=_cparams("arbitrary"),
        name="conv_mix_ln1",
    )(h, a, glu, glu, glu, dw, cb, cg, cbeta, w_gates, wap, wcp, wo, g, b)


def _swiglu_chunks(xb, wg_ref, wu_ref, wd_ref, chunks, row_scale=None):
    acc = None
    c0 = 0
    for w in chunks:
        gt = jnp.dot(xb, wg_ref[:, c0:c0 + w].astype(bf16), preferred_element_type=f32)
        up = jnp.dot(xb, wu_ref[:, c0:c0 + w].astype(bf16), preferred_element_type=f32)
        act = gt * jax.nn.sigmoid(gt) * up
        if row_scale is not None:
            act = act * row_scale
        part = jnp.dot(act.astype(bf16), wd_ref[c0:c0 + w, :].astype(bf16), preferred_element_type=f32)
        acc = part if acc is None else acc + part
        c0 += w
    return acc


def _ffn_kernel(h_ref, wg_ref, wu_ref, wd_ref, g_ref, b_ref, o_ref):
    h = h_ref[...]
    f = _swiglu_chunks(h.astype(bf16), wg_ref, wu_ref, wd_ref, FF_CHUNKS_DENSE)
    o_ref[...] = _ln(ALPHA * h + f, g_ref[...], b_ref[...])


def _dense_ffn(j, h, wg, wu, wd, g, b):
    row = pl.BlockSpec((TM, D_MODEL), lambda s: (s, 0))
    per_layer = lambda r, c: pl.BlockSpec((None, r, c), lambda s: (j, 0, 0), pipeline_mode=pl.Buffered(1))
    return pl.pallas_call(
        _ffn_kernel,
        grid=(N_TILES,),
        in_specs=[row, per_layer(D_MODEL, D_FF), per_layer(D_MODEL, D_FF), per_layer(D_FF, D_MODEL),
                  _full((1, D_MODEL)), _full((1, D_MODEL))],
        out_specs=row,
        out_shape=jax.ShapeDtypeStruct((NT, D_MODEL), f32),
        compiler_params=_cparams("parallel"),
        name="dense_ffn_ln2",
    )(h, wg, wu, wd, g, b)


def _route_copy(src_ref, xs_ref, sem, e, row):
    return pltpu.make_async_copy(
        src_ref.at[pl.ds(e * CHUNK, CHUNK), :],
        xs_ref.at[pl.ds(pl.multiple_of(e * CAP + row, ALIGN), CHUNK), :],
        sem)


def _zero_copy(zbuf_ref, xs_ref, sem, e, row):
    return pltpu.make_async_copy(
        zbuf_ref,
        xs_ref.at[pl.ds(pl.multiple_of(e * CAP + row, ALIGN), TM), :],
        sem)


def _dispatch_pass(p, ranki, sel, gate, hi, dst_ref):
    slot_id = lax.broadcasted_iota(i32, (CHUNK, TM), 0) + p * CHUNK
    lane = lax.broadcasted_iota(i32, (CHUNK, 128), 1)
    ps = []
    for e in range(N_EXPERTS):
        pe = (slot_id == ranki[e:e + 1, :]) & sel[e:e + 1, :]
        ps.append(jnp.where(pe, 1.0, 0.0).astype(bf16))
        gs = jnp.sum(jnp.where(pe, gate[e:e + 1, :], 0.0), axis=1, keepdims=True)
        g_hi = gs.astype(bf16).astype(f32)
        g_mid = (gs - g_hi).astype(bf16).astype(f32)
        g_lo = gs - g_hi - g_mid
        parts = jnp.where(lane == 0, g_hi, jnp.where(lane == 1, g_mid, jnp.where(lane == 2, g_lo, 0.0)))
        dst_ref[e * CHUNK:(e + 1) * CHUNK, D_MODEL:] = parts.astype(bf16)
    pall = jnp.concatenate(ps, axis=0)
    dst_ref[:, :D_MODEL] = jnp.dot(pall, hi, preferred_element_type=f32).astype(bf16)


def _route_kernel(h_ref, r_ref, info_ref, base_ref, used_ref, xs_ref,
                  xbuf_ref, xov_ref, zbuf_ref, cnt_ref, sem_ref):
    s = pl.program_id(0)
    slot = s % 2

    @pl.when(s == 0)
    def _():
        for e in range(N_EXPERTS):
            cnt_ref[e] = 0
        zbuf_ref[...] = jnp.zeros((TM, XS_W), bf16)

    h = h_ref[...]
    hi = h.astype(bf16)
    lo = (h - hi.astype(f32)).astype(bf16)
    pa = jnp.dot(hi, r_ref[...], preferred_element_type=f32)
    pb = jnp.dot(lo, r_ref[...], preferred_element_type=f32)
    logits = pa + pltpu.roll(pa, 128 - N_EXPERTS, axis=1) + pb
    lt = logits.T[:N_EXPERTS, :]

    eid = lax.broadcasted_iota(i32, (N_EXPERTS, TM), 0)
    m1 = jnp.max(lt, axis=0, keepdims=True)
    i1 = jnp.min(jnp.where(lt == m1, eid, N_EXPERTS), axis=0, keepdims=True)
    lt2 = jnp.where(eid == i1, NEG, lt)
    m2 = jnp.max(lt2, axis=0, keepdims=True)
    i2 = jnp.min(jnp.where(lt2 == m2, eid, N_EXPERTS), axis=0, keepdims=True)
    ex = jnp.exp(m2 - m1)
    g1 = 1.0 / (1.0 + ex)
    g2 = ex * g1
    tok = lax.broadcasted_iota(i32, (1, TM), 1)
    valid = (s > 0) | (tok < N_META_ROWS)
    sel1 = (eid == i1) & valid
    sel2 = (eid == i2) & valid
    sel = sel1 | sel2
    onehot = jnp.where(sel, 1.0, 0.0)
    tri = jnp.where(lax.broadcasted_iota(i32, (TM, TM), 0) < lax.broadcasted_iota(i32, (TM, TM), 1), 1.0, 0.0)
    rank = jnp.dot(onehot, tri, preferred_element_type=f32)
    gate = jnp.where(sel1, g1, jnp.where(sel2, g2, 0.0))
    rank1 = jnp.sum(jnp.where(sel1, rank, 0.0), axis=0, keepdims=True)
    rank2 = jnp.sum(jnp.where(sel2, rank, 0.0), axis=0, keepdims=True)
    info = jnp.concatenate([i1.astype(f32), i2.astype(f32), rank1, rank2,
                            jnp.zeros((128 - 4, TM), f32)], axis=0)
    info_ref[...] = info.T
    ranki = rank.astype(i32)

    _dispatch_pass(0, ranki, sel, gate, hi, xbuf_ref.at[slot])

    @pl.when(s > 0)
    def _():
        for e in range(N_EXPERTS):
            _route_copy(xbuf_ref.at[1 - slot], xs_ref, sem_ref.at[1 - slot, e], e, 0).wait()

    bases, counts = [], []
    for e in range(N_EXPERTS):
        base = cnt_ref[e]
        n_e = jnp.sum(onehot[e:e + 1, :]).astype(i32)
        bases.append(base)
        counts.append(n_e)
        base_ref[s, e] = base
        _route_copy(xbuf_ref.at[slot], xs_ref, sem_ref.at[slot, e], e, base).start()
        cnt_ref[e] = base + ((n_e + (ALIGN - 1)) // ALIGN) * ALIGN
    most = functools.reduce(jnp.maximum, counts)

    for p in range(1, N_PASS):
        @pl.when(most > p * CHUNK)
        def _(p=p):
            _dispatch_pass(p, ranki, sel, gate, hi, xov_ref)
            for e in range(N_EXPERTS):
                _route_copy(xov_ref, xs_ref, sem_ref.at[2, e], e, bases[e] + p * CHUNK).start()
            for e in range(N_EXPERTS):
                _route_copy(xov_ref, xs_ref, sem_ref.at[2, e], e, 0).wait()

    @pl.when(s == N_TILES - 1)
    def _():
        for e in range(N_EXPERTS):
            _route_copy(xbuf_ref.at[slot], xs_ref, sem_ref.at[slot, e], e, 0).wait()
        for e in range(N_EXPERTS):
            used = cnt_ref[e]
            used_ref[e] = used
            _zero_copy(zbuf_ref, xs_ref, sem_ref.at[2, e], e, used).start()
        for e in range(N_EXPERTS):
            _zero_copy(zbuf_ref, xs_ref, sem_ref.at[2, e], e, 0).wait()


def _route(h, rsplit):
    return pl.pallas_call(
        _route_kernel,
        grid=(N_TILES,),
        in_specs=[pl.BlockSpec((TM, D_MODEL), lambda s: (s, 0)), _full((D_MODEL, 128))],
        out_specs=[
            pl.BlockSpec((TM, 128), lambda s: (s, 0)),
            _smem(), _smem(),
            pl.BlockSpec(memory_space=pl.ANY),
        ],
        out_shape=[
            jax.ShapeDtypeStruct((NT, 128), f32),
            jax.ShapeDtypeStruct((N_TILES, N_EXPERTS), i32),
            jax.ShapeDtypeStruct((N_EXPERTS,), i32),
            jax.ShapeDtypeStruct((N_EXPERTS * CAP, XS_W), bf16),
        ],
        scratch_shapes=[
            pltpu.VMEM((2, N_EXPERTS * CHUNK, XS_W), bf16),
            pltpu.VMEM((N_EXPERTS * CHUNK, XS_W), bf16),
            pltpu.VMEM((TM, XS_W), bf16),
            pltpu.SMEM((N_EXPERTS,), i32),
            pltpu.SemaphoreType.DMA((3, N_EXPERTS)),
        ],
        compiler_params=_cparams("arbitrary"),
        name="moe_route_dispatch",
    )(h, rsplit)


def _expert_kernel(te_ref, tb_ref, tv_ref, x_ref, wg_ref, wu_ref, wd_ref, y_ref):
    t = pl.program_id(0)

    @pl.when(tv_ref[t] > 0)
    def _():
        x = x_ref[...]
        gate = jnp.sum(x[:, D_MODEL:].astype(f32), axis=1, keepdims=True)
        y = _swiglu_chunks(x[:, :D_MODEL], wg_ref, wu_ref, wd_ref, FF_CHUNKS_MOE, row_scale=gate)
        y_ref[...] = y.astype(bf16)


def _expert_ffn(j, tile_e, tile_blk, tile_valid, xs, wg, wu, wd):
    per_expert = lambda r, c: pl.BlockSpec((None, None, r, c), lambda t, te, tb, tv: (j, te[t], 0, 0))
    grid_spec = pltpu.PrefetchScalarGridSpec(
        num_scalar_prefetch=3,
        grid=(FFN_GRID,),
        in_specs=[
            pl.BlockSpec((TM, XS_W), lambda t, te, tb, tv: (tb[t], 0)),
            per_expert(D_MODEL, D_EXPERT), per_expert(D_MODEL, D_EXPERT), per_expert(D_EXPERT, D_MODEL),
        ],
        out_specs=pl.BlockSpec((TM, D_MODEL), lambda t, te, tb, tv: (tb[t], 0)),
    )
    return pl.pallas_call(
        _expert_kernel,
        grid_spec=grid_spec,
        out_shape=jax.ShapeDtypeStruct((N_EXPERTS * CAP, D_MODEL), bf16),
        compiler_params=_cparams("arbitrary"),
        name="moe_expert_ffn",
    )(tile_e, tile_blk, tile_valid, xs, wg, wu, wd)


def _combine_copy(ys_ref, dst_ref, sem, e, row):
    return pltpu.make_async_copy(
        ys_ref.at[pl.ds(pl.multiple_of(e * CAP + row, ALIGN), CHUNK), :],
        dst_ref.at[pl.ds(e * CHUNK, CHUNK), :],
        sem)


def _window(base_ref, rows_ref, s, e, p):
    want = base_ref[s, e] + p * CHUNK
    start = jnp.minimum(want, rows_ref[e] - CHUNK)
    return start, want - start


def _onehot_cols(info, deltas, p):
    col = lax.broadcasted_iota(i32, (TM, N_EXPERTS * CHUNK), 1)
    hit = None
    for k in range(2):
        ek = info[:, k:k + 1].astype(i32)
        rk = info[:, 2 + k:3 + k].astype(i32) - p * CHUNK
        dk = functools.reduce(lambda a, b: a + b,
                              [jnp.where(ek == e, deltas[e], 0) for e in range(N_EXPERTS)])
        ck = jnp.where((rk >= 0) & (rk < CHUNK), ek * CHUNK + rk + dk, -1)
        hk = col == ck
        hit = hk if hit is None else hit | hk
    return jnp.where(hit, 1.0, 0.0).astype(bf16)


def _combine_kernel(base_ref, rows_ref, h_ref, info_ref, ys_ref, g_ref, b_ref, o_ref,
                    ybuf_ref, yov_ref, f_ref, sem_ref):
    i = pl.program_id(0)
    n = pl.num_programs(0)
    s = i + 1
    slot = i % 2

    def start_fetch(tile, to_slot):
        for e in range(N_EXPERTS):
            start, _ = _window(base_ref, rows_ref, tile, e, 0)
            _combine_copy(ys_ref, ybuf_ref.at[to_slot], sem_ref.at[to_slot, e], e, start).start()

    @pl.when(i == 0)
    def _():
        start_fetch(s, slot)

    @pl.when(i + 1 < n)
    def _():
        start_fetch(s + 1, 1 - slot)

    info = info_ref[...]
    deltas = [_window(base_ref, rows_ref, s, e, 0)[1] for e in range(N_EXPERTS)]
    qm = _onehot_cols(info, deltas, 0)
    for e in range(N_EXPERTS):
        _combine_copy(ys_ref, ybuf_ref.at[slot], sem_ref.at[slot, e], e, 0).wait()
    f_ref[...] = jnp.dot(qm, ybuf_ref[slot], preferred_element_type=f32)

    deepest = jnp.max(jnp.maximum(info[:, 2:3], info[:, 3:4])).astype(i32)
    for p in range(1, N_PASS):
        @pl.when(deepest >= p * CHUNK)
        def _(p=p):
            wins = [_window(base_ref, rows_ref, s, e, p) for e in range(N_EXPERTS)]
            for e in range(N_EXPERTS):
                _combine_copy(ys_ref, yov_ref, sem_ref.at[2, e], e, wins[e][0]).start()
            qp = _onehot_cols(info, [w[1] for w in wins], p)
            for e in range(N_EXPERTS):
                _combine_copy(ys_ref, yov_ref, sem_ref.at[2, e], e, 0).wait()
            f_ref[...] += jnp.dot(qp, yov_ref[...], preferred_element_type=f32)

    o_ref[...] = _ln(ALPHA * h_ref[...] + f_ref[...], g_ref[...], b_ref[...])


def _combine(base, rows, h, info, ys, g, b):
    n_out = N_TILES - 1
    grid_spec = pltpu.PrefetchScalarGridSpec(
        num_scalar_prefetch=2,
        grid=(n_out,),
        in_specs=[
            pl.BlockSpec((TM, D_MODEL), lambda i, base, rows: (i + 1, 0)),
            pl.BlockSpec((TM, 128), lambda i, base, rows: (i + 1, 0)),
            pl.BlockSpec(memory_space=pl.ANY),
            pl.BlockSpec((1, D_MODEL), lambda i, base, rows: (0, 0)),
            pl.BlockSpec((1, D_MODEL), lambda i, base, rows: (0, 0)),
        ],
        out_specs=pl.BlockSpec((TM, D_MODEL), lambda i, base, rows: (i, 0)),
        scratch_shapes=[
            pltpu.VMEM((2, N_EXPERTS * CHUNK, D_MODEL), bf16),
            pltpu.VMEM((N_EXPERTS * CHUNK, D_MODEL), bf16),
            pltpu.VMEM((TM, D_MODEL), f32),
            pltpu.SemaphoreType.DMA((3, N_EXPERTS)),
        ],
    )
    return pl.pallas_call(
        _combine_kernel,
        grid_spec=grid_spec,
        out_shape=jax.ShapeDtypeStruct((n_out * TM, D_MODEL), f32),
        compiler_params=_cparams("arbitrary"),
        name="moe_combine_ln2",
    )(base, rows, h, info, ys, g, b)


def _tile_map(used):
    n_tiles = jnp.maximum((used + TM - 1) // TM, 1)
    ends = jnp.cumsum(n_tiles)
    starts = ends - n_tiles
    t = jnp.arange(FFN_GRID, dtype=i32)
    e = jnp.minimum(jnp.sum((t[:, None] >= ends[None, :]).astype(i32), axis=1), N_EXPERTS - 1)
    valid = (t < ends[-1]).astype(i32)
    blk = e * (CAP // TM) + (t - starts[e])
    last = ends[-1] - 1
    e_last = jnp.minimum(jnp.sum((last >= ends).astype(i32)), N_EXPERTS - 1)
    blk_last = e_last * (CAP // TM) + (last - starts[e_last])
    e = jnp.where(valid > 0, e, e_last).astype(i32)
    blk = jnp.where(valid > 0, blk, blk_last).astype(i32)
    return e, blk, valid, (n_tiles * TM).astype(i32)


def _split2(x):
    hi = x.astype(bf16)
    lo = (x - hi.astype(f32)).astype(bf16)
    return hi, lo


def kernel(x, meta_tokens, emb_ln_g, emb_ln_b, rel_bias, w_in, conv_dw, conv_b, conv_ln_g, conv_ln_b, sinks, w_attn_proj, w_conv_proj, w_out, ln1_g, ln1_b, ffn_w_gate, ffn_w_up, ffn_w_down, router, moe_w_gate, moe_w_up, moe_w_down, ln2_g, ln2_b):
    row = lambda v: v.reshape(1, -1)
    h = None
    out = None
    for i in range(DEPTH):
        if i == 0:
            q, k, v, glu, h = _inproj(i, w_in, x2d=x.reshape(BATCH * SEQ, D_MODEL), meta=meta_tokens,
                                      eg=row(emb_ln_g), eb=row(emb_ln_b))
        else:
            q, k, v, glu = _inproj(i, w_in, h=h)
        a = _attention(rel_bias, sinks[i], q, k, v)
        h = _mix(i, h, a, glu, conv_dw[i], row(conv_b[i]), row(conv_ln_g[i]), row(conv_ln_b[i]),
                 w_in[i, :, GLU_END:], w_attn_proj, w_conv_proj, w_out, row(ln1_g[i]), row(ln1_b[i]))
        j = i // 2
        if i % 2 == 0:
            h = _dense_ffn(j, h, ffn_w_gate, ffn_w_up, ffn_w_down, row(ln2_g[i]), row(ln2_b[i]))
        else:
            r_hi, r_lo = _split2(router[j])
            rsplit = jnp.concatenate(
                [r_hi, r_lo, jnp.zeros((D_MODEL, 128 - 2 * N_EXPERTS), bf16)], axis=1)
            info, base, used, xs = _route(h, rsplit)
            tile_e, tile_blk, tile_valid, rows = _tile_map(used)
            ys = _expert_ffn(j, tile_e, tile_blk, tile_valid, xs, moe_w_gate, moe_w_up, moe_w_down)
            out = _combine(base, rows, h, info, ys, row(ln2_g[i]), row(ln2_b[i]))
    return out.reshape(BATCH, SEQ, D_MODEL)
```

```python
import functools
import math

import numpy as np
import jax
import jax.numpy as jnp
from jax import lax
from jax.experimental import pallas as pl
from jax.experimental.pallas import tpu as pltpu

f32 = jnp.float32
bf16 = jnp.bfloat16
i32 = jnp.int32

D_MODEL = 1024
BATCH = 8
SEQ = 2048
DEPTH = 2
N_META = 16
HEAD_DIM = 64
N_Q_HEADS = 8
N_KV_HEADS = 2
GROUP = N_Q_HEADS // N_KV_HEADS
ATTN_WIDTH = N_Q_HEADS * HEAD_DIM
KV_WIDTH = N_KV_HEADS * HEAD_DIM
WINDOW = 128
BLOCK = 128
CONV_CH = D_MODEL // 2
CONV_WIDTH = 31
N_BUCKETS = 32
MAX_DISTANCE = 128
D_FF = 2816
N_EXPERTS = 8
D_EXPERT = 1408
ALPHA = (2 * DEPTH) ** 0.25
LN_EPS = 1e-5
Q_END = ATTN_WIDTH
K_END = Q_END + KV_WIDTH
V_END = K_END + KV_WIDTH
GLU_END = V_END + 2 * CONV_CH
GA_END = GLU_END + D_MODEL
GC_END = GA_END + D_MODEL

TM = 512
N_META_ROWS = BATCH * N_META
NT = TM + BATCH * SEQ
N_TILES = NT // TM
TILES_PER_BATCH = SEQ // TM
BLOCKS_PER_TILE = TM // BLOCK
KSEG = 3 * BLOCK
CONV_HALO = 32
NEG = -1e30

CHUNK = 160
N_PASS = -(-TM // CHUNK)
ALIGN = 16
XS_W = D_MODEL + 128
CAP = 35 * TM
FFN_GRID = 85
FF_CHUNKS_DENSE = (256,) * 11
FF_CHUNKS_MOE = (256,) * 5 + (128,)

VMEM_LIMIT = 56 * 1024 * 1024


def _cparams(sem="arbitrary", flags=None):
    return pltpu.CompilerParams(dimension_semantics=(sem,), vmem_limit_bytes=VMEM_LIMIT, flags=flags)


def _ln(x, g, b):
    mu = jnp.mean(x, -1, keepdims=True)
    xc = x - mu
    var = jnp.mean(xc * xc, -1, keepdims=True)
    return xc * lax.rsqrt(var + LN_EPS) * g + b


def _full(shape):
    return pl.BlockSpec(shape, lambda *_: (0,) * len(shape))


def _smem():
    return pl.BlockSpec(memory_space=pltpu.SMEM)


def _embed_tile(s, x_ref, meta_ref, g_ref, b_ref):
    m = _ln(meta_ref[...], g_ref[...], b_ref[...])
    meta_tile = jnp.concatenate([m] * BATCH + [jnp.zeros((TM - N_META_ROWS, D_MODEL), f32)], axis=0)
    return jnp.where(s == 0, meta_tile, _ln(x_ref[...], g_ref[...], b_ref[...]))


def _inproj_kernel(embed, *refs):
    s = pl.program_id(0)
    if embed:
        x_ref, meta_ref, eg_ref, eb_ref, w_ref, q_ref, k_ref, v_ref, glu_ref, h_ref, wb_ref = refs
        h = _embed_tile(s, x_ref, meta_ref, eg_ref, eb_ref)
        h_ref[...] = h
    else:
        hin_ref, w_ref, q_ref, k_ref, v_ref, glu_ref, wb_ref = refs
        h = hin_ref[...]

    @pl.when(s == 0)
    def _():
        wb_ref[...] = w_ref[...].astype(bf16)

    u = jnp.dot(h.astype(bf16), wb_ref[...], preferred_element_type=f32)
    q_ref[...] = (u[:, :Q_END] * (HEAD_DIM ** -0.5)).astype(bf16)
    k_ref[...] = u[:, Q_END:K_END].astype(bf16)
    v_ref[...] = u[:, K_END:V_END].astype(bf16)
    glu = u[:, V_END:V_END + CONV_CH] * jax.nn.sigmoid(u[:, V_END + CONV_CH:GLU_END])
    glu_ref[...] = glu.astype(bf16)


def _inproj(layer, w_in, h=None, x2d=None, meta=None, eg=None, eb=None):
    embed = h is None
    row = lambda w: pl.BlockSpec((TM, w), lambda s: (s, 0))
    w_spec = pl.BlockSpec((None, D_MODEL, GLU_END), lambda s: (layer, 0, 0), pipeline_mode=pl.Buffered(1))
    out_specs = [row(ATTN_WIDTH), row(KV_WIDTH), row(KV_WIDTH), row(CONV_CH)]
    out_shape = [
        jax.ShapeDtypeStruct((NT, ATTN_WIDTH), bf16),
        jax.ShapeDtypeStruct((NT, KV_WIDTH), bf16),
        jax.ShapeDtypeStruct((NT, KV_WIDTH), bf16),
        jax.ShapeDtypeStruct((NT, CONV_CH), bf16),
    ]
    if embed:
        in_specs = [
            pl.BlockSpec((TM, D_MODEL), lambda s: (jnp.maximum(s - 1, 0), 0)),
            _full((N_META, D_MODEL)), _full((1, D_MODEL)), _full((1, D_MODEL)), w_spec,
        ]
        args = (x2d, meta, eg, eb, w_in)
        out_specs.append(row(D_MODEL))
        out_shape.append(jax.ShapeDtypeStruct((NT, D_MODEL), f32))
    else:
        in_specs = [row(D_MODEL), w_spec]
        args = (h, w_in)
    return pl.pallas_call(
        functools.partial(_inproj_kernel, embed),
        grid=(N_TILES,),
        in_specs=in_specs,
        out_specs=out_specs,
        out_shape=out_shape,
        scratch_shapes=[pltpu.VMEM((D_MODEL, GLU_END), bf16)],
        compiler_params=_cparams("arbitrary"),
        name="inproj_glu",
    )(*args)


def _rel_bucket_np(dist):
    n = np.maximum(dist, 0)
    max_exact = N_BUCKETS // 2
    nf = np.maximum(n, 1).astype(np.float32)
    scaled = (np.log(nf / np.float32(max_exact)) / np.float32(math.log(MAX_DISTANCE / max_exact))
              * np.float32(N_BUCKETS - max_exact))
    large = np.minimum(max_exact + scaled.astype(np.int32), N_BUCKETS - 1)
    return np.where(n < max_exact, n, large).astype(np.int32)


def _bucket_tables():
    a = np.arange(BLOCK)[:, None]
    m = np.arange(BLOCK)[None, :]
    kk = np.arange(2 * BLOCK)[None, :]
    dist_band = BLOCK + a - kk
    band_ok = (dist_band >= 0) & (dist_band < WINDOW)
    band = np.where(band_ok, _rel_bucket_np(dist_band), -1)
    real = np.full((2, BLOCK, KSEG), -1, np.int32)
    for t, n in enumerate((0, 1)):
        dist_meta = N_META + n * BLOCK + a - m
        real[t, :, :BLOCK] = np.where(m < N_META, _rel_bucket_np(dist_meta), -1)
        real[t, :, BLOCK:] = band
    real[0, :, BLOCK:2 * BLOCK] = -1
    for n in range(2, SEQ // BLOCK):
        dist_meta = N_META + n * BLOCK + a - m
        assert np.array_equal(np.where(m < N_META, _rel_bucket_np(dist_meta), -1), real[1, :, :BLOCK])
    r = np.arange(N_META_ROWS)
    same = (r[:, None] // N_META) == (r[None, :] // N_META)
    dmm = (r[:, None] % N_META) - (r[None, :] % N_META)
    meta = np.where(same & (dmm >= 0), _rel_bucket_np(dmm), -1).astype(np.int32)
    return real, meta


_BK_REAL, _BK_META = _bucket_tables()


def _bias_from_buckets(bk, rb_ref, h):
    acc = jnp.full(bk.shape, NEG, f32)
    for bkt in range(N_BUCKETS):
        acc = jnp.where(bk == bkt, rb_ref[bkt, h], acc)
    return acc


def _expand_kv(x2):
    xf = x2.astype(f32)
    r = pltpu.roll(xf, HEAD_DIM, axis=1)
    low = lax.broadcasted_iota(i32, xf.shape, 1) < HEAD_DIM
    h0 = jnp.where(low, xf, r).astype(bf16)
    h1 = jnp.where(low, r, xf).astype(bf16)
    return (jnp.concatenate([h0, h0], axis=1), jnp.concatenate([h1, h1], axis=1))


def _block_diag(x, n_keys):
    lane_blk = lax.broadcasted_iota(i32, (n_keys, GROUP * HEAD_DIM), 1) // HEAD_DIM
    zero = jnp.zeros_like(x)
    return jnp.concatenate([jnp.where(lane_blk == h, x, zero) for h in range(GROUP)], axis=0)


def _attend(qg, kx, vx, bias_of_head, sink_of_head, n_keys):
    m_rows = qg.shape[0]
    kbd = _block_diag(kx, n_keys)
    vbd = _block_diag(vx, n_keys)
    s = lax.dot_general(qg, kbd, (((1,), (1,)), ((), ())), preferred_element_type=f32)
    ps, rs = [], []
    for h in range(GROUP):
        sh = s[:, h * n_keys:(h + 1) * n_keys] + bias_of_head(h)
        sink = sink_of_head(h)
        mx = jnp.maximum(jnp.max(sh, -1, keepdims=True), sink)
        p = jnp.exp(sh - mx)
        den = jnp.sum(p, -1, keepdims=True) + jnp.exp(sink - mx)
        ps.append(p.astype(bf16))
        rs.append(1.0 / den)
    pm = jnp.concatenate(ps, axis=1)
    o = jnp.dot(pm, vbd, preferred_element_type=f32)
    ol = lax.broadcasted_iota(i32, (m_rows, GROUP * HEAD_DIM), 1) // HEAD_DIM
    scale = jnp.where(ol == 0, rs[0], jnp.where(ol == 1, rs[1], jnp.where(ol == 2, rs[2], rs[3])))
    return o * scale


def _attn_kernel(rb_ref, sink_ref, q_ref, k_ref, v_ref, kp_ref, vp_ref, km_ref, vm_ref,
                 bkr_ref, bkm_ref, o_ref, br_ref, bm_ref):
    s = pl.program_id(0)
    gw = GROUP * HEAD_DIM

    @pl.when(s == 0)
    def _():
        for h in range(N_Q_HEADS):
            for t in range(2):
                br_ref[t, h] = _bias_from_buckets(bkr_ref[t], rb_ref, h)
            bm_ref[h] = _bias_from_buckets(bkm_ref[...], rb_ref, h)
        kxs = _expand_kv(k_ref[:N_META_ROWS, :])
        vxs = _expand_kv(v_ref[:N_META_ROWS, :])
        for g in range(N_KV_HEADS):
            o = _attend(q_ref[:N_META_ROWS, g * gw:(g + 1) * gw], kxs[g], vxs[g],
                        lambda h, g=g: bm_ref[g * GROUP + h],
                        lambda h, g=g: sink_ref[g * GROUP + h], N_META_ROWS)
            o_ref[:N_META_ROWS, g * gw:(g + 1) * gw] = o.astype(bf16)
        o_ref[N_META_ROWS:, :] = jnp.zeros((TM - N_META_ROWS, ATTN_WIDTH), bf16)

    @pl.when(s > 0)
    def _():
        first = ((s - 1) % TILES_PER_BATCH) == 0
        tbl0 = jnp.where(first, 0, 1)
        zpad = jnp.zeros((BLOCK - N_META, KV_WIDTH), bf16)
        kmeta = jnp.concatenate([km_ref[...], zpad], axis=0)
        vmeta = jnp.concatenate([vm_ref[...], zpad], axis=0)
        for blk in range(BLOCKS_PER_TILE):
            rows = slice(blk * BLOCK, (blk + 1) * BLOCK)
            if blk == 0:
                kprev, vprev = kp_ref[...], vp_ref[...]
            else:
                prows = slice((blk - 1) * BLOCK, blk * BLOCK)
                kprev, vprev = k_ref[prows, :], v_ref[prows, :]
            kxs = _expand_kv(jnp.concatenate([kmeta, kprev, k_ref[rows, :]], axis=0))
            vxs = _expand_kv(jnp.concatenate([vmeta, vprev, v_ref[rows, :]], axis=0))
            for g in range(N_KV_HEADS):
                if blk == 0:
                    bias = lambda h, g=g: br_ref[tbl0, g * GROUP + h]
                else:
                    bias = lambda h, g=g: br_ref[1, g * GROUP + h]
                o = _attend(q_ref[rows, g * gw:(g + 1) * gw], kxs[g], vxs[g], bias,
                            lambda h, g=g: sink_ref[g * GROUP + h], KSEG)
                o_ref[rows, g * gw:(g + 1) * gw] = o.astype(bf16)


def _attention(rel_bias, sinks, q, k, v):
    row = lambda w: pl.BlockSpec((TM, w), lambda s: (s, 0))
    prev = pl.BlockSpec((BLOCK, KV_WIDTH), lambda s: (jnp.maximum(s * BLOCKS_PER_TILE - 1, 0), 0))
    meta = pl.BlockSpec((N_META, KV_WIDTH), lambda s: (jnp.maximum(s - 1, 0) // TILES_PER_BATCH, 0))
    return pl.pallas_call(
        _attn_kernel,
        grid=(N_TILES,),
        in_specs=[
            _smem(), _smem(),
            row(ATTN_WIDTH), row(KV_WIDTH), row(KV_WIDTH),
            prev, prev, meta, meta,
            _full((2, BLOCK, KSEG)), _full((N_META_ROWS, N_META_ROWS)),
        ],
        out_specs=row(ATTN_WIDTH),
        out_shape=jax.ShapeDtypeStruct((NT, ATTN_WIDTH), bf16),
        scratch_shapes=[
            pltpu.VMEM((2, N_Q_HEADS, BLOCK, KSEG), f32),
            pltpu.VMEM((N_Q_HEADS, N_META_ROWS, N_META_ROWS), f32),
        ],
        compiler_params=_cparams("arbitrary"),
        name="swa_attention",
    )(rel_bias, sinks, q, k, v, k, v, k, v, jnp.asarray(_BK_REAL), jnp.asarray(_BK_META))


SPREAD = TM // BATCH
SH_ROWS = CONV_HALO + TM - 8
MIX_COLS = 256


def _conv_window(s, glu_ref, gprev_ref, gmeta_ref, win_ref, sh_ref):
    is_meta = s == 0
    first = ((s - 1) % TILES_PER_BATCH) == 0
    glu = glu_ref[...].astype(f32)
    zgap = jnp.zeros((SPREAD - N_META, CONV_CH), f32)
    spread = jnp.concatenate(
        [p for bb in range(BATCH) for p in (zgap, glu[bb * N_META:(bb + 1) * N_META])], axis=0)
    win_ref[CONV_HALO:, :] = jnp.where(is_meta, spread, glu)
    halo_first = jnp.concatenate(
        [jnp.zeros((CONV_HALO - N_META, CONV_CH), f32), gmeta_ref[...].astype(f32)], axis=0)
    halo = jnp.where(first, halo_first, gprev_ref[...].astype(f32))
    win_ref[:CONV_HALO, :] = jnp.where(is_meta, 0.0, halo)

    for r in range(1, 8):
        sh_ref[r - 1] = win_ref[r:r + SH_ROWS, :]


def _conv_finish(s, dw_ref, cb_ref, cg_ref, cbeta_ref, win_ref, sh_ref, cv_ref):
    off = CONV_HALO - (CONV_WIDTH - 1)
    for rc in range(TM // 128):
        for lc in range(CONV_CH // 128):
            ls = slice(lc * 128, (lc + 1) * 128)
            acc = None
            for t in range(CONV_WIDTH):
                r = (off + t) % 8
                r0 = rc * 128 + (off + t) - r
                x = win_ref[r0:r0 + 128, ls] if r == 0 else sh_ref[r - 1, r0:r0 + 128, ls]
                term = dw_ref[t:t + 1, ls] * x
                acc = term if acc is None else acc + term
            cv_ref[rc * 128:(rc + 1) * 128, ls] = acc
    y = _ln(cv_ref[...] + cb_ref[...], cg_ref[...], cbeta_ref[...])
    c = y * jax.nn.sigmoid(y)
    gathered = jnp.concatenate(
        [c[bb * SPREAD + SPREAD - N_META:(bb + 1) * SPREAD] for bb in range(BATCH)]
        + [jnp.zeros((TM - N_META_ROWS, CONV_CH), f32)], axis=0)
    return jnp.where(s == 0, gathered, c).astype(bf16)


def _mix_kernel(h_ref, a_ref, glu_ref, gprev_ref, gmeta_ref, dw_ref, cb_ref, cg_ref, cbeta_ref,
                wg_ref, wap_ref, wcp_ref, wo_ref, g_ref, b_ref, o_ref,
                win_ref, sh_ref, cv_ref, wgb_ref, wapb_ref, wcpb_ref, wob_ref):
    s = pl.program_id(0)

    @pl.when(s == 0)
    def _():
        wgb_ref[...] = wg_ref[...].astype(bf16)
        wapb_ref[...] = wap_ref[...].astype(bf16)
        wcpb_ref[...] = wcp_ref[...].astype(bf16)
        wob_ref[...] = wo_ref[...].astype(bf16)

    h = h_ref[...]
    gates = jnp.dot(h.astype(bf16), wgb_ref[...], preferred_element_type=f32)
    ya = jnp.dot(a_ref[...], wapb_ref[...], preferred_element_type=f32)
    _conv_window(s, glu_ref, gprev_ref, gmeta_ref, win_ref, sh_ref)
    c = _conv_finish(s, dw_ref, cb_ref, cg_ref, cbeta_ref, win_ref, sh_ref, cv_ref)
    yc = jnp.dot(c, wcpb_ref[...], preferred_element_type=f32)
    m = jax.nn.sigmoid(gates[:, :D_MODEL]) * ya + jax.nn.sigmoid(gates[:, D_MODEL:]) * yc
    mix = jnp.dot(m.astype(bf16), wob_ref[...], preferred_element_type=f32)
    o_ref[...] = _ln(ALPHA * h + mix, g_ref[...], b_ref[...])


def _mix(layer, h, a, glu, dw, cb, cg, cbeta, w_gates, wap, wcp, wo, g, b):
    row = lambda w: pl.BlockSpec((TM, w), lambda s: (s, 0))
    gprev = pl.BlockSpec((CONV_HALO, CONV_CH),
                         lambda s: (jnp.maximum(s * (TM // CONV_HALO) - 1, 0), 0))
    gmeta = pl.BlockSpec((N_META, CONV_CH), lambda s: (jnp.maximum(s - 1, 0) // TILES_PER_BATCH, 0))
    per_layer = lambda r, c: pl.BlockSpec((None, r, c), lambda s: (layer, 0, 0), pipeline_mode=pl.Buffered(1))
    return pl.pallas_call(
        _mix_kernel,
        grid=(N_TILES,),
        in_specs=[
            row(D_MODEL), row(ATTN_WIDTH), row(CONV_CH), gprev, gmeta,
            _full((CONV_WIDTH, CONV_CH)), _full((1, CONV_CH)), _full((1, CONV_CH)), _full((1, CONV_CH)),
            _full((D_MODEL, 2 * D_MODEL)), per_layer(ATTN_WIDTH, D_MODEL),
            per_layer(CONV_CH, D_MODEL), per_layer(D_MODEL, D_MODEL),
            _full((1, D_MODEL)), _full((1, D_MODEL)),
        ],
        out_specs=row(D_MODEL),
        out_shape=jax.ShapeDtypeStruct((NT, D_MODEL), f32),
        scratch_shapes=[
            pltpu.VMEM((CONV_HALO + TM, CONV_CH), f32),
            pltpu.VMEM((7, SH_ROWS, CONV_CH), f32),
            pltpu.VMEM((TM, CONV_CH), f32),
            pltpu.VMEM((D_MODEL, 2 * D_MODEL), bf16),
            pltpu.VMEM((ATTN_WIDTH, D_MODEL), bf16),
            pltpu.VMEM((CONV_CH, D_MODEL), bf16),
            pltpu.VMEM((D_MODEL, D_MODEL), bf16),
        ],
        compiler_params=_cparams("arbitrary"),
        name="conv_mix_ln1",
    )(h, a, glu, glu, glu, dw, cb, cg, cbeta, w_gates, wap, wcp, wo, g, b)


def _swiglu_chunks(xb, wg_ref, wu_ref, wd_ref, chunks, row_scale=None):
    acc = None
    c0 = 0
    for w in chunks:
        gt = jnp.dot(xb, wg_ref[:, c0:c0 + w].astype(bf16), preferred_element_type=f32)
        up = jnp.dot(xb, wu_ref[:, c0:c0 + w].astype(bf16), preferred_element_type=f32)
        act = gt * jax.nn.sigmoid(gt) * up
        if row_scale is not None:
            act = act * row_scale
        part = jnp.dot(act.astype(bf16), wd_ref[c0:c0 + w, :].astype(bf16), preferred_element_type=f32)
        acc = part if acc is None else acc + part
        c0 += w
    return acc


def _ffn_kernel(h_ref, wg_ref, wu_ref, wd_ref, g_ref, b_ref, o_ref):
    h = h_ref[...]
    f = _swiglu_chunks(h.astype(bf16), wg_ref, wu_ref, wd_ref, FF_CHUNKS_DENSE)
    o_ref[...] = _ln(ALPHA * h + f, g_ref[...], b_ref[...])


def _dense_ffn(j, h, wg, wu, wd, g, b):
    row = pl.BlockSpec((TM, D_MODEL), lambda s: (s, 0))
    per_layer = lambda r, c: pl.BlockSpec((None, r, c), lambda s: (j, 0, 0), pipeline_mode=pl.Buffered(1))
    return pl.pallas_call(
        _ffn_kernel,
        grid=(N_TILES,),
        in_specs=[row, per_layer(D_MODEL, D_FF), per_layer(D_MODEL, D_FF), per_layer(D_FF, D_MODEL),
                  _full((1, D_MODEL)), _full((1, D_MODEL))],
        out_specs=row,
        out_shape=jax.ShapeDtypeStruct((NT, D_MODEL), f32),
        compiler_params=_cparams("parallel"),
        name="dense_ffn_ln2",
    )(h, wg, wu, wd, g, b)


def _route_copy(src_ref, xs_ref, sem, e, row):
    return pltpu.make_async_copy(
        src_ref.at[pl.ds(e * CHUNK, CHUNK), :],
        xs_ref.at[pl.ds(pl.multiple_of(e * CAP + row, ALIGN), CHUNK), :],
        sem)


def _zero_copy(zbuf_ref, xs_ref, sem, e, row):
    return pltpu.make_async_copy(
        zbuf_ref,
        xs_ref.at[pl.ds(pl.multiple_of(e * CAP + row, ALIGN), TM), :],
        sem)


def _dispatch_pass(p, ranki, sel, gate, hi, dst_ref):
    slot_id = lax.broadcasted_iota(i32, (CHUNK, TM), 0) + p * CHUNK
    lane = lax.broadcasted_iota(i32, (CHUNK, 128), 1)
    ps = []
    for e in range(N_EXPERTS):
        pe = (slot_id == ranki[e:e + 1, :]) & sel[e:e + 1, :]
        ps.append(jnp.where(pe, 1.0, 0.0).astype(bf16))
        gs = jnp.sum(jnp.where(pe, gate[e:e + 1, :], 0.0), axis=1, keepdims=True)
        g_hi = gs.astype(bf16).astype(f32)
        g_mid = (gs - g_hi).astype(bf16).astype(f32)
        g_lo = gs - g_hi - g_mid
        parts = jnp.where(lane == 0, g_hi, jnp.where(lane == 1, g_mid, jnp.where(lane == 2, g_lo, 0.0)))
        dst_ref[e * CHUNK:(e + 1) * CHUNK, D_MODEL:] = parts.astype(bf16)
    pall = jnp.concatenate(ps, axis=0)
    dst_ref[:, :D_MODEL] = jnp.dot(pall, hi, preferred_element_type=f32).astype(bf16)


def _route_kernel(h_ref, r_ref, info_ref, base_ref, used_ref, xs_ref,
                  xbuf_ref, xov_ref, zbuf_ref, cnt_ref, sem_ref):
    s = pl.program_id(0)
    slot = s % 2

    @pl.when(s == 0)
    def _():
        for e in range(N_EXPERTS):
            cnt_ref[e] = 0
        zbuf_ref[...] = jnp.zeros((TM, XS_W), bf16)

    h = h_ref[...]
    hi = h.astype(bf16)
    lo = (h - hi.astype(f32)).astype(bf16)
    pa = jnp.dot(hi, r_ref[...], preferred_element_type=f32)
    pb = jnp.dot(lo, r_ref[...], preferred_element_type=f32)
    logits = pa + pltpu.roll(pa, 128 - N_EXPERTS, axis=1) + pb
    lt = logits.T[:N_EXPERTS, :]

    eid = lax.broadcasted_iota(i32, (N_EXPERTS, TM), 0)
    m1 = jnp.max(lt, axis=0, keepdims=True)
    i1 = jnp.min(jnp.where(lt == m1, eid, N_EXPERTS), axis=0, keepdims=True)
    lt2 = jnp.where(eid == i1, NEG, lt)
    m2 = jnp.max(lt2, axis=0, keepdims=True)
    i2 = jnp.min(jnp.where(lt2 == m2, eid, N_EXPERTS), axis=0, keepdims=True)
    ex = jnp.exp(m2 - m1)
    g1 = 1.0 / (1.0 + ex)
    g2 = ex * g1
    tok = lax.broadcasted_iota(i32, (1, TM), 1)
    valid = (s > 0) | (tok < N_META_ROWS)
    sel1 = (eid == i1) & valid
    sel2 = (eid == i2) & valid
    sel = sel1 | sel2
    onehot = jnp.where(sel, 1.0, 0.0)
    tri = jnp.where(lax.broadcasted_iota(i32, (TM, TM), 0) < lax.broadcasted_iota(i32, (TM, TM), 1), 1.0, 0.0)
    rank = jnp.dot(onehot, tri, preferred_element_type=f32)
    gate = jnp.where(sel1, g1, jnp.where(sel2, g2, 0.0))
    rank1 = jnp.sum(jnp.where(sel1, rank, 0.0), axis=0, keepdims=True)
    rank2 = jnp.sum(jnp.where(sel2, rank, 0.0), axis=0, keepdims=True)
    info = jnp.concatenate([i1.astype(f32), i2.astype(f32), rank1, rank2,
                            jnp.zeros((128 - 4, TM), f32)], axis=0)
    info_ref[...] = info.T
    ranki = rank.astype(i32)

    _dispatch_pass(0, ranki, sel, gate, hi, xbuf_ref.at[slot])

    @pl.when(s > 0)
    def _():
        for e in range(N_EXPERTS):
            _route_copy(xbuf_ref.at[1 - slot], xs_ref, sem_ref.at[1 - slot, e], e, 0).wait()

    bases, counts = [], []
    for e in range(N_EXPERTS):
        base = cnt_ref[e]
        n_e = jnp.sum(onehot[e:e + 1, :]).astype(i32)
        bases.append(base)
        counts.append(n_e)
        base_ref[s, e] = base
        _route_copy(xbuf_ref.at[slot], xs_ref, sem_ref.at[slot, e], e, base).start()
        cnt_ref[e] = base + ((n_e + (ALIGN - 1)) // ALIGN) * ALIGN
    most = functools.reduce(jnp.maximum, counts)

    for p in range(1, N_PASS):
        @pl.when(most > p * CHUNK)
        def _(p=p):
            _dispatch_pass(p, ranki, sel, gate, hi, xov_ref)
            for e in range(N_EXPERTS):
                _route_copy(xov_ref, xs_ref, sem_ref.at[2, e], e, bases[e] + p * CHUNK).start()
            for e in range(N_EXPERTS):
                _route_copy(xov_ref, xs_ref, sem_ref.at[2, e], e, 0).wait()

    @pl.when(s == N_TILES - 1)
    def _():
        for e in range(N_EXPERTS):
            _route_copy(xbuf_ref.at[slot], xs_ref, sem_ref.at[slot, e], e, 0).wait()
        for e in range(N_EXPERTS):
            used = cnt_ref[e]
            used_ref[e] = used
            _zero_copy(zbuf_ref, xs_ref, sem_ref.at[2, e], e, used).start()
        for e in range(N_EXPERTS):
            _zero_copy(zbuf_ref, xs_ref, sem_ref.at[2, e], e, 0).wait()


def _route(h, rsplit):
    return pl.pallas_call(
        _route_kernel,
        grid=(N_TILES,),
        in_specs=[pl.BlockSpec((TM, D_MODEL), lambda s: (s, 0)), _full((D_MODEL, 128))],
        out_specs=[
            pl.BlockSpec((TM, 128), lambda s: (s, 0)),
            _smem(), _smem(),
            pl.BlockSpec(memory_space=pl.ANY),
        ],
        out_shape=[
            jax.ShapeDtypeStruct((NT, 128), f32),
            jax.ShapeDtypeStruct((N_TILES, N_EXPERTS), i32),
            jax.ShapeDtypeStruct((N_EXPERTS,), i32),
            jax.ShapeDtypeStruct((N_EXPERTS * CAP, XS_W), bf16),
        ],
        scratch_shapes=[
            pltpu.VMEM((2, N_EXPERTS * CHUNK, XS_W), bf16),
            pltpu.VMEM((N_EXPERTS * CHUNK, XS_W), bf16),
            pltpu.VMEM((TM, XS_W), bf16),
            pltpu.SMEM((N_EXPERTS,), i32),
            pltpu.SemaphoreType.DMA((3, N_EXPERTS)),
        ],
        compiler_params=_cparams("arbitrary"),
        name="moe_route_dispatch",
    )(h, rsplit)


def _expert_kernel(te_ref, tb_ref, tv_ref, x_ref, wg_ref, wu_ref, wd_ref, y_ref):
    t = pl.program_id(0)

    @pl.when(tv_ref[t] > 0)
    def _():
        x = x_ref[...]
        gate = jnp.sum(x[:, D_MODEL:].astype(f32), axis=1, keepdims=True)
        y = _swiglu_chunks(x[:, :D_MODEL], wg_ref, wu_ref, wd_ref, FF_CHUNKS_MOE, row_scale=gate)
        y_ref[...] = y.astype(bf16)


def _expert_ffn(j, tile_e, tile_blk, tile_valid, xs, wg, wu, wd):
    per_expert = lambda r, c: pl.BlockSpec((None, None, r, c), lambda t, te, tb, tv: (j, te[t], 0, 0))
    grid_spec = pltpu.PrefetchScalarGridSpec(
        num_scalar_prefetch=3,
        grid=(FFN_GRID,),
        in_specs=[
            pl.BlockSpec((TM, XS_W), lambda t, te, tb, tv: (tb[t], 0)),
            per_expert(D_MODEL, D_EXPERT), per_expert(D_MODEL, D_EXPERT), per_expert(D_EXPERT, D_MODEL),
        ],
        out_specs=pl.BlockSpec((TM, D_MODEL), lambda t, te, tb, tv: (tb[t], 0)),
    )
    return pl.pallas_call(
        _expert_kernel,
        grid_spec=grid_spec,
        out_shape=jax.ShapeDtypeStruct((N_EXPERTS * CAP, D_MODEL), bf16),
        compiler_params=_cparams("arbitrary"),
        name="moe_expert_ffn",
    )(tile_e, tile_blk, tile_valid, xs, wg, wu, wd)


def _combine_copy(ys_ref, dst_ref, sem, e, row):
    return pltpu.make_async_copy(
        ys_ref.at[pl.ds(pl.multiple_of(e * CAP + row, ALIGN), CHUNK), :],
        dst_ref.at[pl.ds(e * CHUNK, CHUNK), :],
        sem)


def _window(base_ref, rows_ref, s, e, p):
    want = base_ref[s, e] + p * CHUNK
    start = jnp.minimum(want, rows_ref[e] - CHUNK)
    return start, want - start


def _onehot_cols(info, deltas, p):
    col = lax.broadcasted_iota(i32, (TM, N_EXPERTS * CHUNK), 1)
    hit = None
    for k in range(2):
        ek = info[:, k:k + 1].astype(i32)
        rk = info[:, 2 + k:3 + k].astype(i32) - p * CHUNK
        dk = functools.reduce(lambda a, b: a + b,
                              [jnp.where(ek == e, deltas[e], 0) for e in range(N_EXPERTS)])
        ck = jnp.where((rk >= 0) & (rk < CHUNK), ek * CHUNK + rk + dk, -1)
        hk = col == ck
        hit = hk if hit is None else hit | hk
    return jnp.where(hit, 1.0, 0.0).astype(bf16)


def _combine_kernel(base_ref, rows_ref, h_ref, info_ref, ys_ref, g_ref, b_ref, o_ref,
                    ybuf_ref, yov_ref, f_ref, sem_ref):
    i = pl.program_id(0)
    n = pl.num_programs(0)
    s = i + 1
    slot = i % 2

    def start_fetch(tile, to_slot):
        for e in range(N_EXPERTS):
            start, _ = _window(base_ref, rows_ref, tile, e, 0)
            _combine_copy(ys_ref, ybuf_ref.at[to_slot], sem_ref.at[to_slot, e], e, start).start()

    @pl.when(i == 0)
    def _():
        start_fetch(s, slot)

    @pl.when(i + 1 < n)
    def _():
        start_fetch(s + 1, 1 - slot)

    info = info_ref[...]
    deltas = [_window(base_ref, rows_ref, s, e, 0)[1] for e in range(N_EXPERTS)]
    qm = _onehot_cols(info, deltas, 0)
    for e in range(N_EXPERTS):
        _combine_copy(ys_ref, ybuf_ref.at[slot], sem_ref.at[slot, e], e, 0).wait()
    f_ref[...] = jnp.dot(qm, ybuf_ref[slot], preferred_element_type=f32)

    deepest = jnp.max(jnp.maximum(info[:, 2:3], info[:, 3:4])).astype(i32)
    for p in range(1, N_PASS):
        @pl.when(deepest >= p * CHUNK)
        def _(p=p):
            wins = [_window(base_ref, rows_ref, s, e, p) for e in range(N_EXPERTS)]
            for e in range(N_EXPERTS):
                _combine_copy(ys_ref, yov_ref, sem_ref.at[2, e], e, wins[e][0]).start()
            qp = _onehot_cols(info, [w[1] for w in wins], p)
            for e in range(N_EXPERTS):
                _combine_copy(ys_ref, yov_ref, sem_ref.at[2, e], e, 0).wait()
            f_ref[...] += jnp.dot(qp, yov_ref[...], preferred_element_type=f32)

    o_ref[...] = _ln(ALPHA * h_ref[...] + f_ref[...], g_ref[...], b_ref[...])


def _combine(base, rows, h, info, ys, g, b):
    n_out = N_TILES - 1
    grid_spec = pltpu.PrefetchScalarGridSpec(
        num_scalar_prefetch=2,
        grid=(n_out,),
        in_specs=[
            pl.BlockSpec((TM, D_MODEL), lambda i, base, rows: (i + 1, 0)),
            pl.BlockSpec((TM, 128), lambda i, base, rows: (i + 1, 0)),
            pl.BlockSpec(memory_space=pl.ANY),
            pl.BlockSpec((1, D_MODEL), lambda i, base, rows: (0, 0)),
            pl.BlockSpec((1, D_MODEL), lambda i, base, rows: (0, 0)),
        ],
        out_specs=pl.BlockSpec((TM, D_MODEL), lambda i, base, rows: (i, 0)),
        scratch_shapes=[
            pltpu.VMEM((2, N_EXPERTS * CHUNK, D_MODEL), bf16),
            pltpu.VMEM((N_EXPERTS * CHUNK, D_MODEL), bf16),
            pltpu.VMEM((TM, D_MODEL), f32),
            pltpu.SemaphoreType.DMA((3, N_EXPERTS)),
        ],
    )
    return pl.pallas_call(
        _combine_kernel,
        grid_spec=grid_spec,
        out_shape=jax.ShapeDtypeStruct((n_out * TM, D_MODEL), f32),
        compiler_params=_cparams("arbitrary"),
        name="moe_combine_ln2",
    )(base, rows, h, info, ys, g, b)


def _tile_map(used):
    n_tiles = jnp.maximum((used + TM - 1) // TM, 1)
    ends = jnp.cumsum(n_tiles)
    starts = ends - n_tiles
    t = jnp.arange(FFN_GRID, dtype=i32)
    e = jnp.minimum(jnp.sum((t[:, None] >= ends[None, :]).astype(i32), axis=1), N_EXPERTS - 1)
    valid = (t < ends[-1]).astype(i32)
    blk = e * (CAP // TM) + (t - starts[e])
    last = ends[-1] - 1
    e_last = jnp.minimum(jnp.sum((last >= ends).astype(i32)), N_EXPERTS - 1)
    blk_last = e_last * (CAP // TM) + (last - starts[e_last])
    e = jnp.where(valid > 0, e, e_last).astype(i32)
    blk = jnp.where(valid > 0, blk, blk_last).astype(i32)
    return e, blk, valid, (n_tiles * TM).astype(i32)


def _split2(x):
    hi = x.astype(bf16)
    lo = (x - hi.astype(f32)).astype(bf16)
    return hi, lo


def kernel(x, meta_tokens, emb_ln_g, emb_ln_b, rel_bias, w_in, conv_dw, conv_b, conv_ln_g, conv_ln_b, sinks, w_attn_proj, w_conv_proj, w_out, ln1_g, ln1_b, ffn_w_gate, ffn_w_up, ffn_w_down, router, moe_w_gate, moe_w_up, moe_w_down, ln2_g, ln2_b):
    row = lambda v: v.reshape(1, -1)
    h = None
    out = None
    for i in range(DEPTH):
        if i == 0:
            q, k, v, glu, h = _inproj(i, w_in, x2d=x.reshape(BATCH * SEQ, D_MODEL), meta=meta_tokens,
                                      eg=row(emb_ln_g), eb=row(emb_ln_b))
        else:
            q, k, v, glu = _inproj(i, w_in, h=h)
        a = _attention(rel_bias, sinks[i], q, k, v)
        h = _mix(i, h, a, glu, conv_dw[i], row(conv_b[i]), row(conv_ln_g[i]), row(conv_ln_b[i]),
                 w_in[i, :, GLU_END:], w_attn_proj, w_conv_proj, w_out, row(ln1_g[i]), row(ln1_b[i]))
        j = i // 2
        if i % 2 == 0:
            h = _dense_ffn(j, h, ffn_w_gate, ffn_w_up, ffn_w_down, row(ln2_g[i]), row(ln2_b[i]))
        else:
            r_hi, r_lo = _split2(router[j])
            rsplit = jnp.concatenate(
                [r_hi, r_lo, jnp.zeros((D_MODEL, 128 - 2 * N_EXPERTS), bf16)], axis=1)
            info, base, used, xs = _route(h, rsplit)
            tile_e, tile_blk, tile_valid, rows = _tile_map(used)
            ys = _expert_ffn(j, tile_e, tile_blk, tile_valid, xs, moe_w_gate, moe_w_up, moe_w_down)
            out = _combine(base, rows, h, info, ys, row(ln2_g[i]), row(ln2_b[i]))
    return out.reshape(BATCH, SEQ, D_MODEL)
```

```python
import functools
import math

import numpy as np
import jax
import jax.numpy as jnp
from jax import lax
from jax.experimental import pallas as pl
from jax.experimental.pallas import tpu as pltpu

f32 = jnp.float32
bf16 = jnp.bfloat16
i32 = jnp.int32

D_MODEL = 1024
BATCH = 8
SEQ = 2048
DEPTH = 2
N_META = 16
HEAD_DIM = 64
N_Q_HEADS = 8
N_KV_HEADS = 2
GROUP = N_Q_HEADS // N_KV_HEADS
ATTN_WIDTH = N_Q_HEADS * HEAD_DIM
KV_WIDTH = N_KV_HEADS * HEAD_DIM
WINDOW = 128
BLOCK = 128
CONV_CH = D_MODEL // 2
CONV_WIDTH = 31
N_BUCKETS = 32
MAX_DISTANCE = 128
D_FF = 2816
N_EXPERTS = 8
D_EXPERT = 1408
ALPHA = (2 * DEPTH) ** 0.25
LN_EPS = 1e-5
Q_END = ATTN_WIDTH
K_END = Q_END + KV_WIDTH
V_END = K_END + KV_WIDTH
GLU_END = V_END + 2 * CONV_CH
GA_END = GLU_END + D_MODEL
GC_END = GA_END + D_MODEL

TM = 512
N_META_ROWS = BATCH * N_META
NT = TM + BATCH * SEQ
N_TILES = NT // TM
TILES_PER_BATCH = SEQ // TM
BLOCKS_PER_TILE = TM // BLOCK
KSEG = 3 * BLOCK
CONV_HALO = 32
NEG = -1e30

CHUNK = 192
N_PASS = -(-TM // CHUNK)
ALIGN = 16
XS_W = D_MODEL + 128
CAP = 35 * TM
FFN_GRID = 85
MXU_COLS = 256
FF_CHUNKS_DENSE = (256,) * 11
FF_CHUNKS_MOE = (256,) * 5 + (128,)

VMEM_LIMIT = 56 * 1024 * 1024


def _cparams(sem="arbitrary", flags=None):
    return pltpu.CompilerParams(dimension_semantics=(sem,), vmem_limit_bytes=VMEM_LIMIT, flags=flags)


def _ln(x, g, b):
    mu = jnp.mean(x, -1, keepdims=True)
    xc = x - mu
    var = jnp.mean(xc * xc, -1, keepdims=True)
    return xc * lax.rsqrt(var + LN_EPS) * g + b


def _full(shape):
    return pl.BlockSpec(shape, lambda *_: (0,) * len(shape))


def _smem():
    return pl.BlockSpec(memory_space=pltpu.SMEM)


def _embed_tile(s, x_ref, meta_ref, g_ref, b_ref):
    m = _ln(meta_ref[...], g_ref[...], b_ref[...])
    meta_tile = jnp.concatenate([m] * BATCH + [jnp.zeros((TM - N_META_ROWS, D_MODEL), f32)], axis=0)
    return jnp.where(s == 0, meta_tile, _ln(x_ref[...], g_ref[...], b_ref[...]))


def _inproj_kernel(embed, *refs):
    s = pl.program_id(0)
    if embed:
        x_ref, meta_ref, eg_ref, eb_ref, w_ref, q_ref, k_ref, v_ref, glu_ref, h_ref, wb_ref = refs
        h = _embed_tile(s, x_ref, meta_ref, eg_ref, eb_ref)
        h_ref[...] = h
    else:
        hin_ref, w_ref, q_ref, k_ref, v_ref, glu_ref, wb_ref = refs
        h = hin_ref[...]

    @pl.when(s == 0)
    def _():
        wb_ref[...] = w_ref[...].astype(bf16)

    u = jnp.dot(h.astype(bf16), wb_ref[...], preferred_element_type=f32)
    q_ref[...] = (u[:, :Q_END] * (HEAD_DIM ** -0.5)).astype(bf16)
    k_ref[...] = u[:, Q_END:K_END].astype(bf16)
    v_ref[...] = u[:, K_END:V_END].astype(bf16)
    glu = u[:, V_END:V_END + CONV_CH] * jax.nn.sigmoid(u[:, V_END + CONV_CH:GLU_END])
    glu_ref[...] = glu.astype(bf16)


def _inproj(layer, w_in, h=None, x2d=None, meta=None, eg=None, eb=None):
    embed = h is None
    row = lambda w: pl.BlockSpec((TM, w), lambda s: (s, 0))
    w_spec = pl.BlockSpec((None, D_MODEL, GLU_END), lambda s: (layer, 0, 0), pipeline_mode=pl.Buffered(1))
    out_specs = [row(ATTN_WIDTH), row(KV_WIDTH), row(KV_WIDTH), row(CONV_CH)]
    out_shape = [
        jax.ShapeDtypeStruct((NT, ATTN_WIDTH), bf16),
        jax.ShapeDtypeStruct((NT, KV_WIDTH), bf16),
        jax.ShapeDtypeStruct((NT, KV_WIDTH), bf16),
        jax.ShapeDtypeStruct((NT, CONV_CH), bf16),
    ]
    if embed:
        in_specs = [
            pl.BlockSpec((TM, D_MODEL), lambda s: (jnp.maximum(s - 1, 0), 0)),
            _full((N_META, D_MODEL)), _full((1, D_MODEL)), _full((1, D_MODEL)), w_spec,
        ]
        args = (x2d, meta, eg, eb, w_in)
        out_specs.append(row(D_MODEL))
        out_shape.append(jax.ShapeDtypeStruct((NT, D_MODEL), f32))
    else:
        in_specs = [row(D_MODEL), w_spec]
        args = (h, w_in)
    return pl.pallas_call(
        functools.partial(_inproj_kernel, embed),
        grid=(N_TILES,),
        in_specs=in_specs,
        out_specs=out_specs,
        out_shape=out_shape,
        scratch_shapes=[pltpu.VMEM((D_MODEL, GLU_END), bf16)],
        compiler_params=_cparams("arbitrary"),
        name="inproj_glu",
    )(*args)


def _rel_bucket_np(dist):
    n = np.maximum(dist, 0)
    max_exact = N_BUCKETS // 2
    nf = np.maximum(n, 1).astype(np.float32)
    scaled = (np.log(nf / np.float32(max_exact)) / np.float32(math.log(MAX_DISTANCE / max_exact))
              * np.float32(N_BUCKETS - max_exact))
    large = np.minimum(max_exact + scaled.astype(np.int32), N_BUCKETS - 1)
    return np.where(n < max_exact, n, large).astype(np.int32)


def _bucket_tables():
    a = np.arange(BLOCK)[:, None]
    m = np.arange(BLOCK)[None, :]
    kk = np.arange(2 * BLOCK)[None, :]
    dist_band = BLOCK + a - kk
    band_ok = (dist_band >= 0) & (dist_band < WINDOW)
    band = np.where(band_ok, _rel_bucket_np(dist_band), -1)
    real = np.full((2, BLOCK, KSEG), -1, np.int32)
    for t, n in enumerate((0, 1)):
        dist_meta = N_META + n * BLOCK + a - m
        real[t, :, :BLOCK] = np.where(m < N_META, _rel_bucket_np(dist_meta), -1)
        real[t, :, BLOCK:] = band
    real[0, :, BLOCK:2 * BLOCK] = -1
    for n in range(2, SEQ // BLOCK):
        dist_meta = N_META + n * BLOCK + a - m
        assert np.array_equal(np.where(m < N_META, _rel_bucket_np(dist_meta), -1), real[1, :, :BLOCK])
    r = np.arange(N_META_ROWS)
    same = (r[:, None] // N_META) == (r[None, :] // N_META)
    dmm = (r[:, None] % N_META) - (r[None, :] % N_META)
    meta = np.where(same & (dmm >= 0), _rel_bucket_np(dmm), -1).astype(np.int32)
    return real, meta


_BK_REAL, _BK_META = _bucket_tables()


def _bias_from_buckets(bk, rb_ref, h):
    acc = jnp.full(bk.shape, NEG, f32)
    for bkt in range(N_BUCKETS):
        acc = jnp.where(bk == bkt, rb_ref[bkt, h], acc)
    return acc


def _expand_kv(x2):
    xf = x2.astype(f32)
    r = pltpu.roll(xf, HEAD_DIM, axis=1)
    low = lax.broadcasted_iota(i32, xf.shape, 1) < HEAD_DIM
    h0 = jnp.where(low, xf, r).astype(bf16)
    h1 = jnp.where(low, r, xf).astype(bf16)
    return (jnp.concatenate([h0, h0], axis=1), jnp.concatenate([h1, h1], axis=1))


def _block_diag(x, n_keys):
    lane_blk = lax.broadcasted_iota(i32, (n_keys, GROUP * HEAD_DIM), 1) // HEAD_DIM
    zero = jnp.zeros_like(x)
    return jnp.concatenate([jnp.where(lane_blk == h, x, zero) for h in range(GROUP)], axis=0)


def _attend(qg, kx, vx, bias_of_head, sink_of_head, n_keys):
    m_rows = qg.shape[0]
    kbd = _block_diag(kx, n_keys)
    vbd = _block_diag(vx, n_keys)
    s = lax.dot_general(qg, kbd, (((1,), (1,)), ((), ())), preferred_element_type=f32)
    ps, rs = [], []
    for h in range(GROUP):
        sh = s[:, h * n_keys:(h + 1) * n_keys] + bias_of_head(h)
        sink = sink_of_head(h)
        mx = jnp.maximum(jnp.max(sh, -1, keepdims=True), sink)
        p = jnp.exp(sh - mx)
        den = jnp.sum(p, -1, keepdims=True) + jnp.exp(sink - mx)
        ps.append(p.astype(bf16))
        rs.append(1.0 / den)
    pm = jnp.concatenate(ps, axis=1)
    o = jnp.dot(pm, vbd, preferred_element_type=f32)
    ol = lax.broadcasted_iota(i32, (m_rows, GROUP * HEAD_DIM), 1) // HEAD_DIM
    scale = jnp.where(ol == 0, rs[0], jnp.where(ol == 1, rs[1], jnp.where(ol == 2, rs[2], rs[3])))
    return o * scale


def _attn_kernel(rb_ref, sink_ref, q_ref, k_ref, v_ref, kp_ref, vp_ref, km_ref, vm_ref,
                 bkr_ref, bkm_ref, o_ref, br_ref, bm_ref):
    s = pl.program_id(0)
    gw = GROUP * HEAD_DIM

    @pl.when(s == 0)
    def _():
        for h in range(N_Q_HEADS):
            for t in range(2):
                br_ref[t, h] = _bias_from_buckets(bkr_ref[t], rb_ref, h)
            bm_ref[h] = _bias_from_buckets(bkm_ref[...], rb_ref, h)
        kxs = _expand_kv(k_ref[:N_META_ROWS, :])
        vxs = _expand_kv(v_ref[:N_META_ROWS, :])
        for g in range(N_KV_HEADS):
            o = _attend(q_ref[:N_META_ROWS, g * gw:(g + 1) * gw], kxs[g], vxs[g],
                        lambda h, g=g: bm_ref[g * GROUP + h],
                        lambda h, g=g: sink_ref[g * GROUP + h], N_META_ROWS)
            o_ref[:N_META_ROWS, g * gw:(g + 1) * gw] = o.astype(bf16)
        o_ref[N_META_ROWS:, :] = jnp.zeros((TM - N_META_ROWS, ATTN_WIDTH), bf16)

    @pl.when(s > 0)
    def _():
        first = ((s - 1) % TILES_PER_BATCH) == 0
        tbl0 = jnp.where(first, 0, 1)
        zpad = jnp.zeros((BLOCK - N_META, KV_WIDTH), bf16)
        kmeta = jnp.concatenate([km_ref[...], zpad], axis=0)
        vmeta = jnp.concatenate([vm_ref[...], zpad], axis=0)
        for blk in range(BLOCKS_PER_TILE):
            rows = slice(blk * BLOCK, (blk + 1) * BLOCK)
            if blk == 0:
                kprev, vprev = kp_ref[...], vp_ref[...]
            else:
                prows = slice((blk - 1) * BLOCK, blk * BLOCK)
                kprev, vprev = k_ref[prows, :], v_ref[prows, :]
            kxs = _expand_kv(jnp.concatenate([kmeta, kprev, k_ref[rows, :]], axis=0))
            vxs = _expand_kv(jnp.concatenate([vmeta, vprev, v_ref[rows, :]], axis=0))
            for g in range(N_KV_HEADS):
                if blk == 0:
                    bias = lambda h, g=g: br_ref[tbl0, g * GROUP + h]
                else:
                    bias = lambda h, g=g: br_ref[1, g * GROUP + h]
                o = _attend(q_ref[rows, g * gw:(g + 1) * gw], kxs[g], vxs[g], bias,
                            lambda h, g=g: sink_ref[g * GROUP + h], KSEG)
                o_ref[rows, g * gw:(g + 1) * gw] = o.astype(bf16)


def _attention(rel_bias, sinks, q, k, v):
    row = lambda w: pl.BlockSpec((TM, w), lambda s: (s, 0))
    prev = pl.BlockSpec((BLOCK, KV_WIDTH), lambda s: (jnp.maximum(s * BLOCKS_PER_TILE - 1, 0), 0))
    meta = pl.BlockSpec((N_META, KV_WIDTH), lambda s: (jnp.maximum(s - 1, 0) // TILES_PER_BATCH, 0))
    return pl.pallas_call(
        _attn_kernel,
        grid=(N_TILES,),
        in_specs=[
            _smem(), _smem(),
            row(ATTN_WIDTH), row(KV_WIDTH), row(KV_WIDTH),
            prev, prev, meta, meta,
            _full((2, BLOCK, KSEG)), _full((N_META_ROWS, N_META_ROWS)),
        ],
        out_specs=row(ATTN_WIDTH),
        out_shape=jax.ShapeDtypeStruct((NT, ATTN_WIDTH), bf16),
        scratch_shapes=[
            pltpu.VMEM((2, N_Q_HEADS, BLOCK, KSEG), f32),
            pltpu.VMEM((N_Q_HEADS, N_META_ROWS, N_META_ROWS), f32),
        ],
        compiler_params=_cparams("arbitrary"),
        name="swa_attention",
    )(rel_bias, sinks, q, k, v, k, v, k, v, jnp.asarray(_BK_REAL), jnp.asarray(_BK_META))


SPREAD = TM // BATCH
SH_ROWS = CONV_HALO + TM - 8
GATE_BLOCKS = 2 * D_MODEL // MXU_COLS


def _conv_window(s, glu_ref, gprev_ref, gmeta_ref, win_ref, sh_ref):
    is_meta = s == 0
    first = ((s - 1) % TILES_PER_BATCH) == 0
    glu = glu_ref[...].astype(f32)
    zgap = jnp.zeros((SPREAD - N_META, CONV_CH), f32)
    spread = jnp.concatenate(
        [p for bb in range(BATCH) for p in (zgap, glu[bb * N_META:(bb + 1) * N_META])], axis=0)
    win_ref[CONV_HALO:, :] = jnp.where(is_meta, spread, glu)
    halo_first = jnp.concatenate(
        [jnp.zeros((CONV_HALO - N_META, CONV_CH), f32), gmeta_ref[...].astype(f32)], axis=0)
    halo = jnp.where(first, halo_first, gprev_ref[...].astype(f32))
    win_ref[:CONV_HALO, :] = jnp.where(is_meta, 0.0, halo)

    for r in range(1, 8):
        sh_ref[r - 1] = win_ref[r:r + SH_ROWS, :]


def _conv_finish(s, dw_ref, cb_ref, cg_ref, cbeta_ref, win_ref, sh_ref, cv_ref):
    off = CONV_HALO - (CONV_WIDTH - 1)
    for rc in range(TM // 128):
        for lc in range(CONV_CH // 128):
            ls = slice(lc * 128, (lc + 1) * 128)
            acc = None
            for t in range(CONV_WIDTH):
                r = (off + t) % 8
                r0 = rc * 128 + (off + t) - r
                x = win_ref[r0:r0 + 128, ls] if r == 0 else sh_ref[r - 1, r0:r0 + 128, ls]
                term = dw_ref[t:t + 1, ls] * x
                acc = term if acc is None else acc + term
            cv_ref[rc * 128:(rc + 1) * 128, ls] = acc
    y = _ln(cv_ref[...] + cb_ref[...], cg_ref[...], cbeta_ref[...])
    c = y * jax.nn.sigmoid(y)
    gathered = jnp.concatenate(
        [c[bb * SPREAD + SPREAD - N_META:(bb + 1) * SPREAD] for bb in range(BATCH)]
        + [jnp.zeros((TM - N_META_ROWS, CONV_CH), f32)], axis=0)
    return jnp.where(s == 0, gathered, c).astype(bf16)


def _mix_kernel(h_ref, a_ref, glu_ref, gprev_ref, gmeta_ref, dw_ref, cb_ref, cg_ref, cbeta_ref, *rest):
    wg_refs = rest[:GATE_BLOCKS]
    (wap_ref, wcp_ref, wo_ref, g_ref, b_ref, o_ref,
     win_ref, sh_ref, cv_ref, wgb_ref, wapb_ref, wcpb_ref, wob_ref) = rest[GATE_BLOCKS:]
    s = pl.program_id(0)

    @pl.when(s == 0)
    def _():
        for k in range(GATE_BLOCKS):
            wgb_ref[:, k * MXU_COLS:(k + 1) * MXU_COLS] = wg_refs[k][...].astype(bf16)
        wapb_ref[...] = wap_ref[...].astype(bf16)
        wcpb_ref[...] = wcp_ref[...].astype(bf16)
        wob_ref[...] = wo_ref[...].astype(bf16)

    h = h_ref[...]
    gates = jnp.dot(h.astype(bf16), wgb_ref[...], preferred_element_type=f32)
    ya = jnp.dot(a_ref[...], wapb_ref[...], preferred_element_type=f32)
    _conv_window(s, glu_ref, gprev_ref, gmeta_ref, win_ref, sh_ref)
    c = _conv_finish(s, dw_ref, cb_ref, cg_ref, cbeta_ref, win_ref, sh_ref, cv_ref)
    yc = jnp.dot(c, wcpb_ref[...], preferred_element_type=f32)
    m = jax.nn.sigmoid(gates[:, :D_MODEL]) * ya + jax.nn.sigmoid(gates[:, D_MODEL:]) * yc
    mix = jnp.dot(m.astype(bf16), wob_ref[...], preferred_element_type=f32)
    o_ref[...] = _ln(ALPHA * h + mix, g_ref[...], b_ref[...])


def _mix(layer, h, a, glu, dw, cb, cg, cbeta, w_in, wap, wcp, wo, g, b):
    row = lambda w: pl.BlockSpec((TM, w), lambda s: (s, 0))
    gate_block = lambda k: pl.BlockSpec((None, D_MODEL, MXU_COLS),
                                        lambda s: (layer, 0, GLU_END // MXU_COLS + k),
                                        pipeline_mode=pl.Buffered(1))
    gprev = pl.BlockSpec((CONV_HALO, CONV_CH),
                         lambda s: (jnp.maximum(s * (TM // CONV_HALO) - 1, 0), 0))
    gmeta = pl.BlockSpec((N_META, CONV_CH), lambda s: (jnp.maximum(s - 1, 0) // TILES_PER_BATCH, 0))
    per_layer = lambda r, c: pl.BlockSpec((None, r, c), lambda s: (layer, 0, 0), pipeline_mode=pl.Buffered(1))
    return pl.pallas_call(
        _mix_kernel,
        grid=(N_TILES,),
        in_specs=[
            row(D_MODEL), row(ATTN_WIDTH), row(CONV_CH), gprev, gmeta,
            _full((CONV_WIDTH, CONV_CH)), _full((1, CONV_CH)), _full((1, CONV_CH)), _full((1, CONV_CH)),
            *[gate_block(k) for k in range(GATE_BLOCKS)], per_layer(ATTN_WIDTH, D_MODEL),
            per_layer(CONV_CH, D_MODEL), per_layer(D_MODEL, D_MODEL),
            _full((1, D_MODEL)), _full((1, D_MODEL)),
        ],
        out_specs=row(D_MODEL),
        out_shape=jax.ShapeDtypeStruct((NT, D_MODEL), f32),
        scratch_shapes=[
            pltpu.VMEM((CONV_HALO + TM, CONV_CH), f32),
            pltpu.VMEM((7, SH_ROWS, CONV_CH), f32),
            pltpu.VMEM((TM, CONV_CH), f32),
            pltpu.VMEM((D_MODEL, 2 * D_MODEL), bf16),
            pltpu.VMEM((ATTN_WIDTH, D_MODEL), bf16),
            pltpu.VMEM((CONV_CH, D_MODEL), bf16),
            pltpu.VMEM((D_MODEL, D_MODEL), bf16),
        ],
        compiler_params=_cparams("arbitrary"),
        name="conv_mix_ln1",
    )(h, a, glu, glu, glu, dw, cb, cg, cbeta, *([w_in] * GATE_BLOCKS), wap, wcp, wo, g, b)


def _swiglu_chunks(xb, wg_ref, wu_ref, wd_ref, chunks, row_scale=None):
    acc = None
    c0 = 0
    for w in chunks:
        wg_c = wg_ref[:, c0:c0 + w].astype(bf16)
        wu_c = wu_ref[:, c0:c0 + w].astype(bf16)
        if 2 * w <= MXU_COLS:
            gu = jnp.dot(xb, jnp.concatenate([wg_c, wu_c], axis=1), preferred_element_type=f32)
            gt, up = gu[:, :w], gu[:, w:]
        else:
            gt = jnp.dot(xb, wg_c, preferred_element_type=f32)
            up = jnp.dot(xb, wu_c, preferred_element_type=f32)
        act = gt * jax.nn.sigmoid(gt) * up
        if row_scale is not None:
            act = act * row_scale
        part = jnp.dot(act.astype(bf16), wd_ref[c0:c0 + w, :].astype(bf16), preferred_element_type=f32)
        acc = part if acc is None else acc + part
        c0 += w
    return acc


def _ffn_kernel(h_ref, wg_ref, wu_ref, wd_ref, g_ref, b_ref, o_ref):
    h = h_ref[...]
    f = _swiglu_chunks(h.astype(bf16), wg_ref, wu_ref, wd_ref, FF_CHUNKS_DENSE)
    o_ref[...] = _ln(ALPHA * h + f, g_ref[...], b_ref[...])


def _dense_ffn(j, h, wg, wu, wd, g, b):
    row = pl.BlockSpec((TM, D_MODEL), lambda s: (s, 0))
    per_layer = lambda r, c: pl.BlockSpec((None, r, c), lambda s: (j, 0, 0), pipeline_mode=pl.Buffered(1))
    return pl.pallas_call(
        _ffn_kernel,
        grid=(N_TILES,),
        in_specs=[row, per_layer(D_MODEL, D_FF), per_layer(D_MODEL, D_FF), per_layer(D_FF, D_MODEL),
                  _full((1, D_MODEL)), _full((1, D_MODEL))],
        out_specs=row,
        out_shape=jax.ShapeDtypeStruct((NT, D_MODEL), f32),
        compiler_params=_cparams("parallel"),
        name="dense_ffn_ln2",
    )(h, wg, wu, wd, g, b)


def _route_copy(src_ref, xs_ref, sem, e, row):
    return pltpu.make_async_copy(
        src_ref.at[pl.ds(e * CHUNK, CHUNK), :],
        xs_ref.at[pl.ds(pl.multiple_of(e * CAP + row, ALIGN), CHUNK), :],
        sem)


def _zero_copy(zbuf_ref, xs_ref, sem, e, row):
    return pltpu.make_async_copy(
        zbuf_ref,
        xs_ref.at[pl.ds(pl.multiple_of(e * CAP + row, ALIGN), TM), :],
        sem)


def _dispatch_pass(p, ranki, sel, gate, hi, dst_ref):
    slot_id = lax.broadcasted_iota(i32, (CHUNK, TM), 0) + p * CHUNK
    lane = lax.broadcasted_iota(i32, (CHUNK, 128), 1)
    ps = []
    for e in range(N_EXPERTS):
        pe = (slot_id == ranki[e:e + 1, :]) & sel[e:e + 1, :]
        ps.append(jnp.where(pe, 1.0, 0.0).astype(bf16))
        gs = jnp.sum(jnp.where(pe, gate[e:e + 1, :], 0.0), axis=1, keepdims=True)
        g_hi = gs.astype(bf16).astype(f32)
        g_mid = (gs - g_hi).astype(bf16).astype(f32)
        g_lo = gs - g_hi - g_mid
        parts = jnp.where(lane == 0, g_hi, jnp.where(lane == 1, g_mid, jnp.where(lane == 2, g_lo, 0.0)))
        dst_ref[e * CHUNK:(e + 1) * CHUNK, D_MODEL:] = parts.astype(bf16)
    pall = jnp.concatenate(ps, axis=0)
    dst_ref[:, :D_MODEL] = jnp.dot(pall, hi, preferred_element_type=f32).astype(bf16)


def _route_kernel(h_ref, r_ref, info_ref, base_ref, rows_ref, te_ref, tb_ref, tv_ref, xs_ref,
                  xbuf_ref, xov_ref, zbuf_ref, cnt_ref, sem_ref):
    s = pl.program_id(0)
    slot = s % 2

    @pl.when(s == 0)
    def _():
        for e in range(N_EXPERTS):
            cnt_ref[e] = 0
        zbuf_ref[...] = jnp.zeros((TM, XS_W), bf16)

    h = h_ref[...]
    hi = h.astype(bf16)
    lo = (h - hi.astype(f32)).astype(bf16)
    pa = jnp.dot(hi, r_ref[...], preferred_element_type=f32)
    pb = jnp.dot(lo, r_ref[...], preferred_element_type=f32)
    logits = pa + pltpu.roll(pa, 128 - N_EXPERTS, axis=1) + pb
    lt = logits.T[:N_EXPERTS, :]

    eid = lax.broadcasted_iota(i32, (N_EXPERTS, TM), 0)
    m1 = jnp.max(lt, axis=0, keepdims=True)
    i1 = jnp.min(jnp.where(lt == m1, eid, N_EXPERTS), axis=0, keepdims=True)
    lt2 = jnp.where(eid == i1, NEG, lt)
    m2 = jnp.max(lt2, axis=0, keepdims=True)
    i2 = jnp.min(jnp.where(lt2 == m2, eid, N_EXPERTS), axis=0, keepdims=True)
    ex = jnp.exp(m2 - m1)
    g1 = 1.0 / (1.0 + ex)
    g2 = ex * g1
    tok = lax.broadcasted_iota(i32, (1, TM), 1)
    valid = (s > 0) | (tok < N_META_ROWS)
    sel1 = (eid == i1) & valid
    sel2 = (eid == i2) & valid
    sel = sel1 | sel2
    onehot = jnp.where(sel, 1.0, 0.0)
    tri = jnp.where(lax.broadcasted_iota(i32, (TM, TM), 0) < lax.broadcasted_iota(i32, (TM, TM), 1), 1.0, 0.0)
    rank = jnp.dot(onehot, tri, preferred_element_type=f32)
    gate = jnp.where(sel1, g1, jnp.where(sel2, g2, 0.0))
    rank1 = jnp.sum(jnp.where(sel1, rank, 0.0), axis=0, keepdims=True)
    rank2 = jnp.sum(jnp.where(sel2, rank, 0.0), axis=0, keepdims=True)
    info = jnp.concatenate([i1.astype(f32), i2.astype(f32), rank1, rank2,
                            jnp.zeros((128 - 4, TM), f32)], axis=0)
    info_ref[...] = info.T
    ranki = rank.astype(i32)

    _dispatch_pass(0, ranki, sel, gate, hi, xbuf_ref.at[slot])

    @pl.when(s > 0)
    def _():
        for e in range(N_EXPERTS):
            _route_copy(xbuf_ref.at[1 - slot], xs_ref, sem_ref.at[1 - slot, e], e, 0).wait()

    bases, counts = [], []
    for e in range(N_EXPERTS):
        base = cnt_ref[e]
        n_e = jnp.sum(onehot[e:e + 1, :]).astype(i32)
        bases.append(base)
        counts.append(n_e)
        base_ref[s, e] = base
        _route_copy(xbuf_ref.at[slot], xs_ref, sem_ref.at[slot, e], e, base).start()
        cnt_ref[e] = base + ((n_e + (ALIGN - 1)) // ALIGN) * ALIGN
    most = functools.reduce(jnp.maximum, counts)

    for p in range(1, N_PASS):
        @pl.when(most > p * CHUNK)
        def _(p=p):
            _dispatch_pass(p, ranki, sel, gate, hi, xov_ref)
            for e in range(N_EXPERTS):
                _route_copy(xov_ref, xs_ref, sem_ref.at[2, e], e, bases[e] + p * CHUNK).start()
            for e in range(N_EXPERTS):
                _route_copy(xov_ref, xs_ref, sem_ref.at[2, e], e, 0).wait()

    @pl.when(s == N_TILES - 1)
    def _():
        for e in range(N_EXPERTS):
            _route_copy(xbuf_ref.at[slot], xs_ref, sem_ref.at[slot, e], e, 0).wait()
        used = [cnt_ref[e] for e in range(N_EXPERTS)]
        for e in range(N_EXPERTS):
            _zero_copy(zbuf_ref, xs_ref, sem_ref.at[2, e], e, used[e]).start()
        starts, ends, total = [], [], 0
        for e in range(N_EXPERTS):
            n_t = jnp.maximum(lax.shift_right_logical(used[e] + (TM - 1), TM.bit_length() - 1), 1)
            rows_ref[e] = n_t * TM
            starts.append(total)
            total = total + n_t
            ends.append(total)

        def tile_entry(t):
            e_t = sum(jnp.where(t >= ends[e], 1, 0) for e in range(N_EXPERTS - 1))
            first = sum(jnp.where(e_t == e, starts[e], 0) for e in range(N_EXPERTS))
            return e_t, e_t * (CAP // TM) + (t - first)

        e_last, blk_last = tile_entry(total - 1)
        for t in range(FFN_GRID):
            e_t, blk_t = tile_entry(t)
            live = t < total
            te_ref[t] = jnp.where(live, e_t, e_last)
            tb_ref[t] = jnp.where(live, blk_t, blk_last)
            tv_ref[t] = jnp.where(live, 1, 0)
        for e in range(N_EXPERTS):
            _zero_copy(zbuf_ref, xs_ref, sem_ref.at[2, e], e, 0).wait()


def _route(h, rsplit):
    return pl.pallas_call(
        _route_kernel,
        grid=(N_TILES,),
        in_specs=[pl.BlockSpec((TM, D_MODEL), lambda s: (s, 0)), _full((D_MODEL, 128))],
        out_specs=[
            pl.BlockSpec((TM, 128), lambda s: (s, 0)),
            _smem(), _smem(), _smem(), _smem(), _smem(),
            pl.BlockSpec(memory_space=pl.ANY),
        ],
        out_shape=[
            jax.ShapeDtypeStruct((NT, 128), f32),
            jax.ShapeDtypeStruct((N_TILES, N_EXPERTS), i32),
            jax.ShapeDtypeStruct((N_EXPERTS,), i32),
            jax.ShapeDtypeStruct((FFN_GRID,), i32),
            jax.ShapeDtypeStruct((FFN_GRID,), i32),
            jax.ShapeDtypeStruct((FFN_GRID,), i32),
            jax.ShapeDtypeStruct((N_EXPERTS * CAP, XS_W), bf16),
        ],
        scratch_shapes=[
            pltpu.VMEM((2, N_EXPERTS * CHUNK, XS_W), bf16),
            pltpu.VMEM((N_EXPERTS * CHUNK, XS_W), bf16),
            pltpu.VMEM((TM, XS_W), bf16),
            pltpu.SMEM((N_EXPERTS,), i32),
            pltpu.SemaphoreType.DMA((3, N_EXPERTS)),
        ],
        compiler_params=_cparams("arbitrary"),
        name="moe_route_dispatch",
    )(h, rsplit)


def _expert_kernel(te_ref, tb_ref, tv_ref, x_ref, wg_ref, wu_ref, wd_ref, y_ref):
    t = pl.program_id(0)

    @pl.when(tv_ref[t] > 0)
    def _():
        x = x_ref[...]
        gate = jnp.sum(x[:, D_MODEL:].astype(f32), axis=1, keepdims=True)
        y = _swiglu_chunks(x[:, :D_MODEL], wg_ref, wu_ref, wd_ref, FF_CHUNKS_MOE, row_scale=gate)
        y_ref[...] = y.astype(bf16)


def _expert_ffn(j, tile_e, tile_blk, tile_valid, xs, wg, wu, wd):
    per_expert = lambda r, c: pl.BlockSpec((None, None, r, c), lambda t, te, tb, tv: (j, te[t], 0, 0))
    grid_spec = pltpu.PrefetchScalarGridSpec(
        num_scalar_prefetch=3,
        grid=(FFN_GRID,),
        in_specs=[
            pl.BlockSpec((TM, XS_W), lambda t, te, tb, tv: (tb[t], 0)),
            per_expert(D_MODEL, D_EXPERT), per_expert(D_MODEL, D_EXPERT), per_expert(D_EXPERT, D_MODEL),
        ],
        out_specs=pl.BlockSpec((TM, D_MODEL), lambda t, te, tb, tv: (tb[t], 0)),
    )
    return pl.pallas_call(
        _expert_kernel,
        grid_spec=grid_spec,
        out_shape=jax.ShapeDtypeStruct((N_EXPERTS * CAP, D_MODEL), bf16),
        compiler_params=_cparams("arbitrary"),
        name="moe_expert_ffn",
    )(tile_e, tile_blk, tile_valid, xs, wg, wu, wd)


def _combine_copy(ys_ref, dst_ref, sem, e, row):
    return pltpu.make_async_copy(
        ys_ref.at[pl.ds(pl.multiple_of(e * CAP + row, ALIGN), CHUNK), :],
        dst_ref.at[pl.ds(e * CHUNK, CHUNK), :],
        sem)


def _window(base_ref, rows_ref, s, e, p):
    want = base_ref[s, e] + p * CHUNK
    start = jnp.minimum(want, rows_ref[e] - CHUNK)
    return start, want - start


def _onehot_cols(info, deltas, p):
    col = lax.broadcasted_iota(i32, (TM, N_EXPERTS * CHUNK), 1)
    hit = None
    for k in range(2):
        ek = info[:, k:k + 1].astype(i32)
        rk = info[:, 2 + k:3 + k].astype(i32) - p * CHUNK
        dk = functools.reduce(lambda a, b: a + b,
                              [jnp.where(ek == e, deltas[e], 0) for e in range(N_EXPERTS)])
        ck = jnp.where((rk >= 0) & (rk < CHUNK), ek * CHUNK + rk + dk, -1)
        hk = col == ck
        hit = hk if hit is None else hit | hk
    return jnp.where(hit, 1.0, 0.0).astype(bf16)


def _combine_kernel(base_ref, rows_ref, h_ref, info_ref, ys_ref, g_ref, b_ref, o_ref,
                    ybuf_ref, yov_ref, f_ref, sem_ref):
    i = pl.program_id(0)
    n = pl.num_programs(0)
    s = i + 1
    slot = i % 2

    def start_fetch(tile, to_slot):
        for e in range(N_EXPERTS):
            start, _ = _window(base_ref, rows_ref, tile, e, 0)
            _combine_copy(ys_ref, ybuf_ref.at[to_slot], sem_ref.at[to_slot, e], e, start).start()

    @pl.when(i == 0)
    def _():
        start_fetch(s, slot)

    @pl.when(i + 1 < n)
    def _():
        start_fetch(s + 1, 1 - slot)

    info = info_ref[...]
    deltas = [_window(base_ref, rows_ref, s, e, 0)[1] for e in range(N_EXPERTS)]
    qm = _onehot_cols(info, deltas, 0)
    for e in range(N_EXPERTS):
        _combine_copy(ys_ref, ybuf_ref.at[slot], sem_ref.at[slot, e], e, 0).wait()
    f_ref[...] = jnp.dot(qm, ybuf_ref[slot], preferred_element_type=f32)

    deepest = jnp.max(jnp.maximum(info[:, 2:3], info[:, 3:4])).astype(i32)
    for p in range(1, N_PASS):
        @pl.when(deepest >= p * CHUNK)
        def _(p=p):
            wins = [_window(base_ref, rows_ref, s, e, p) for e in range(N_EXPERTS)]
            for e in range(N_EXPERTS):
                _combine_copy(ys_ref, yov_ref, sem_ref.at[2, e], e, wins[e][0]).start()
            qp = _onehot_cols(info, [w[1] for w in wins], p)
            for e in range(N_EXPERTS):
                _combine_copy(ys_ref, yov_ref, sem_ref.at[2, e], e, 0).wait()
            f_ref[...] += jnp.dot(qp, yov_ref[...], preferred_element_type=f32)

    o_ref[...] = _ln(ALPHA * h_ref[...] + f_ref[...], g_ref[...], b_ref[...])


def _combine(base, rows, h, info, ys, g, b):
    n_out = N_TILES - 1
    grid_spec = pltpu.PrefetchScalarGridSpec(
        num_scalar_prefetch=2,
        grid=(n_out,),
        in_specs=[
            pl.BlockSpec((TM, D_MODEL), lambda i, base, rows: (i + 1, 0)),
            pl.BlockSpec((TM, 128), lambda i, base, rows: (i + 1, 0)),
            pl.BlockSpec(memory_space=pl.ANY),
            pl.BlockSpec((1, D_MODEL), lambda i, base, rows: (0, 0)),
            pl.BlockSpec((1, D_MODEL), lambda i, base, rows: (0, 0)),
        ],
        out_specs=pl.BlockSpec((TM, D_MODEL), lambda i, base, rows: (i, 0)),
        scratch_shapes=[
            pltpu.VMEM((2, N_EXPERTS * CHUNK, D_MODEL), bf16),
            pltpu.VMEM((N_EXPERTS * CHUNK, D_MODEL), bf16),
            pltpu.VMEM((TM, D_MODEL), f32),
            pltpu.SemaphoreType.DMA((3, N_EXPERTS)),
        ],
    )
    return pl.pallas_call(
        _combine_kernel,
        grid_spec=grid_spec,
        out_shape=jax.ShapeDtypeStruct((n_out * TM, D_MODEL), f32),
        compiler_params=_cparams("arbitrary"),
        name="moe_combine_ln2",
    )(base, rows, h, info, ys, g, b)


def _split2(x):
    hi = x.astype(bf16)
    lo = (x - hi.astype(f32)).astype(bf16)
    return hi, lo


def kernel(x, meta_tokens, emb_ln_g, emb_ln_b, rel_bias, w_in, conv_dw, conv_b, conv_ln_g, conv_ln_b, sinks, w_attn_proj, w_conv_proj, w_out, ln1_g, ln1_b, ffn_w_gate, ffn_w_up, ffn_w_down, router, moe_w_gate, moe_w_up, moe_w_down, ln2_g, ln2_b):
    row = lambda v: v.reshape(1, -1)
    h = None
    out = None
    for i in range(DEPTH):
        if i == 0:
            q, k, v, glu, h = _inproj(i, w_in, x2d=x.reshape(BATCH * SEQ, D_MODEL), meta=meta_tokens,
                                      eg=row(emb_ln_g), eb=row(emb_ln_b))
        else:
            q, k, v, glu = _inproj(i, w_in, h=h)
        a = _attention(rel_bias, sinks[i], q, k, v)
        h = _mix(i, h, a, glu, conv_dw[i], row(conv_b[i]), row(conv_ln_g[i]), row(conv_ln_b[i]),
                 w_in, w_attn_proj, w_conv_proj, w_out, row(ln1_g[i]), row(ln1_b[i]))
        j = i // 2
        if i % 2 == 0:
            h = _dense_ffn(j, h, ffn_w_gate, ffn_w_up, ffn_w_down, row(ln2_g[i]), row(ln2_b[i]))
        else:
            r_hi, r_lo = _split2(router[j])
            rsplit = jnp.concatenate(
                [r_hi, r_lo, jnp.zeros((D_MODEL, 128 - 2 * N_EXPERTS), bf16)], axis=1)
            info, base, rows, tile_e, tile_blk, tile_valid, xs = _route(h, rsplit)
            ys = _expert_ffn(j, tile_e, tile_blk, tile_valid, xs, moe_w_gate, moe_w_up, moe_w_down)
            out = _combine(base, rows, h, info, ys, row(ln2_g[i]), row(ln2_b[i]))
    return out.reshape(BATCH, SEQ, D_MODEL)
```

```python
import functools
import math

import numpy as np
import jax
import jax.numpy as jnp
from jax import lax
from jax.experimental import pallas as pl
from jax.experimental.pallas import tpu as pltpu

f32 = jnp.float32
bf16 = jnp.bfloat16
i32 = jnp.int32

D_MODEL = 1024
BATCH = 8
SEQ = 2048
DEPTH = 2
N_META = 16
HEAD_DIM = 64
N_Q_HEADS = 8
N_KV_HEADS = 2
GROUP = N_Q_HEADS // N_KV_HEADS
ATTN_WIDTH = N_Q_HEADS * HEAD_DIM
KV_WIDTH = N_KV_HEADS * HEAD_DIM
WINDOW = 128
BLOCK = 128
CONV_CH = D_MODEL // 2
CONV_WIDTH = 31
N_BUCKETS = 32
MAX_DISTANCE = 128
D_FF = 2816
N_EXPERTS = 8
D_EXPERT = 1408
ALPHA = (2 * DEPTH) ** 0.25
LN_EPS = 1e-5
Q_END = ATTN_WIDTH
K_END = Q_END + KV_WIDTH
V_END = K_END + KV_WIDTH
GLU_END = V_END + 2 * CONV_CH
GA_END = GLU_END + D_MODEL
GC_END = GA_END + D_MODEL

TM = 512
N_META_ROWS = BATCH * N_META
NT = TM + BATCH * SEQ
N_TILES = NT // TM
TILES_PER_BATCH = SEQ // TM
BLOCKS_PER_TILE = TM // BLOCK
KSEG = 3 * BLOCK
CONV_HALO = 32
NEG = -1e30

CHUNK = 192
N_PASS = -(-TM // CHUNK)
ALIGN = 16
XS_W = D_MODEL + 128
CAP = 35 * TM
FFN_GRID = 85
MXU_COLS = 256
FF_CHUNKS_DENSE = (256,) * 11
FF_CHUNKS_MOE = (256,) * 5 + (128,)

VMEM_LIMIT = 56 * 1024 * 1024


def _cparams(sem="arbitrary", flags=None):
    return pltpu.CompilerParams(dimension_semantics=(sem,), vmem_limit_bytes=VMEM_LIMIT, flags=flags)


def _ln(x, g, b):
    mu = jnp.mean(x, -1, keepdims=True)
    xc = x - mu
    var = jnp.mean(xc * xc, -1, keepdims=True)
    return xc * lax.rsqrt(var + LN_EPS) * g + b


def _full(shape):
    return pl.BlockSpec(shape, lambda *_: (0,) * len(shape))


def _smem():
    return pl.BlockSpec(memory_space=pltpu.SMEM)


def _embed_tile(s, x_ref, meta_ref, g_ref, b_ref):
    m = _ln(meta_ref[...], g_ref[...], b_ref[...])
    meta_tile = jnp.concatenate([m] * BATCH + [jnp.zeros((TM - N_META_ROWS, D_MODEL), f32)], axis=0)
    return jnp.where(s == 0, meta_tile, _ln(x_ref[...], g_ref[...], b_ref[...]))


def _inproj_kernel(embed, *refs):
    s = pl.program_id(0)
    if embed:
        x_ref, meta_ref, eg_ref, eb_ref, w_ref, q_ref, k_ref, v_ref, glu_ref, h_ref, wb_ref = refs
        h = _embed_tile(s, x_ref, meta_ref, eg_ref, eb_ref)
        h_ref[...] = h
    else:
        hin_ref, w_ref, q_ref, k_ref, v_ref, glu_ref, wb_ref = refs
        h = hin_ref[...]

    @pl.when(s == 0)
    def _():
        wb_ref[...] = w_ref[...].astype(bf16)

    u = jnp.dot(h.astype(bf16), wb_ref[...], preferred_element_type=f32)
    q_ref[...] = (u[:, :Q_END] * (HEAD_DIM ** -0.5)).astype(bf16)
    k_ref[...] = u[:, Q_END:K_END].astype(bf16)
    v_ref[...] = u[:, K_END:V_END].astype(bf16)
    glu = u[:, V_END:V_END + CONV_CH] * jax.nn.sigmoid(u[:, V_END + CONV_CH:GLU_END])
    glu_ref[...] = glu.astype(bf16)


def _inproj(layer, w_in, h=None, x2d=None, meta=None, eg=None, eb=None):
    embed = h is None
    row = lambda w: pl.BlockSpec((TM, w), lambda s: (s, 0))
    w_spec = pl.BlockSpec((None, D_MODEL, GLU_END), lambda s: (layer, 0, 0), pipeline_mode=pl.Buffered(1))
    out_specs = [row(ATTN_WIDTH), row(KV_WIDTH), row(KV_WIDTH), row(CONV_CH)]
    out_shape = [
        jax.ShapeDtypeStruct((NT, ATTN_WIDTH), bf16),
        jax.ShapeDtypeStruct((NT, KV_WIDTH), bf16),
        jax.ShapeDtypeStruct((NT, KV_WIDTH), bf16),
        jax.ShapeDtypeStruct((NT, CONV_CH), bf16),
    ]
    if embed:
        in_specs = [
            pl.BlockSpec((TM, D_MODEL), lambda s: (jnp.maximum(s - 1, 0), 0)),
            _full((N_META, D_MODEL)), _full((1, D_MODEL)), _full((1, D_MODEL)), w_spec,
        ]
        args = (x2d, meta, eg, eb, w_in)
        out_specs.append(row(D_MODEL))
        out_shape.append(jax.ShapeDtypeStruct((NT, D_MODEL), f32))
    else:
        in_specs = [row(D_MODEL), w_spec]
        args = (h, w_in)
    return pl.pallas_call(
        functools.partial(_inproj_kernel, embed),
        grid=(N_TILES,),
        in_specs=in_specs,
        out_specs=out_specs,
        out_shape=out_shape,
        scratch_shapes=[pltpu.VMEM((D_MODEL, GLU_END), bf16)],
        compiler_params=_cparams("arbitrary"),
        name="inproj_glu",
    )(*args)


def _rel_bucket_np(dist):
    n = np.maximum(dist, 0)
    max_exact = N_BUCKETS // 2
    nf = np.maximum(n, 1).astype(np.float32)
    scaled = (np.log(nf / np.float32(max_exact)) / np.float32(math.log(MAX_DISTANCE / max_exact))
              * np.float32(N_BUCKETS - max_exact))
    large = np.minimum(max_exact + scaled.astype(np.int32), N_BUCKETS - 1)
    return np.where(n < max_exact, n, large).astype(np.int32)


def _bucket_tables():
    a = np.arange(BLOCK)[:, None]
    m = np.arange(BLOCK)[None, :]
    kk = np.arange(2 * BLOCK)[None, :]
    dist_band = BLOCK + a - kk
    band_ok = (dist_band >= 0) & (dist_band < WINDOW)
    band = np.where(band_ok, _rel_bucket_np(dist_band), -1)
    real = np.full((2, BLOCK, KSEG), -1, np.int32)
    for t, n in enumerate((0, 1)):
        dist_meta = N_META + n * BLOCK + a - m
        real[t, :, :BLOCK] = np.where(m < N_META, _rel_bucket_np(dist_meta), -1)
        real[t, :, BLOCK:] = band
    real[0, :, BLOCK:2 * BLOCK] = -1
    for n in range(2, SEQ // BLOCK):
        dist_meta = N_META + n * BLOCK + a - m
        assert np.array_equal(np.where(m < N_META, _rel_bucket_np(dist_meta), -1), real[1, :, :BLOCK])
    r = np.arange(N_META_ROWS)
    same = (r[:, None] // N_META) == (r[None, :] // N_META)
    dmm = (r[:, None] % N_META) - (r[None, :] % N_META)
    meta = np.where(same & (dmm >= 0), _rel_bucket_np(dmm), -1).astype(np.int32)
    return real, meta


_BK_REAL, _BK_META = _bucket_tables()


def _bias_from_buckets(bk, rb_ref, h):
    acc = jnp.full(bk.shape, NEG, f32)
    for bkt in range(N_BUCKETS):
        acc = jnp.where(bk == bkt, rb_ref[bkt, h], acc)
    return acc


def _expand_kv(x2):
    xf = x2.astype(f32)
    r = pltpu.roll(xf, HEAD_DIM, axis=1)
    low = lax.broadcasted_iota(i32, xf.shape, 1) < HEAD_DIM
    h0 = jnp.where(low, xf, r).astype(bf16)
    h1 = jnp.where(low, r, xf).astype(bf16)
    return (jnp.concatenate([h0, h0], axis=1), jnp.concatenate([h1, h1], axis=1))


def _block_diag(x, n_keys):
    lane_blk = lax.broadcasted_iota(i32, (n_keys, GROUP * HEAD_DIM), 1) // HEAD_DIM
    zero = jnp.zeros_like(x)
    return jnp.concatenate([jnp.where(lane_blk == h, x, zero) for h in range(GROUP)], axis=0)


def _attend(qg, kx, vx, bias_of_head, sink_of_head, n_keys):
    m_rows = qg.shape[0]
    kbd = _block_diag(kx, n_keys)
    vbd = _block_diag(vx, n_keys)
    s = lax.dot_general(qg, kbd, (((1,), (1,)), ((), ())), preferred_element_type=f32)
    ps, rs = [], []
    for h in range(GROUP):
        sh = s[:, h * n_keys:(h + 1) * n_keys] + bias_of_head(h)
        sink = sink_of_head(h)
        mx = jnp.maximum(jnp.max(sh, -1, keepdims=True), sink)
        p = jnp.exp(sh - mx)
        den = jnp.sum(p, -1, keepdims=True) + jnp.exp(sink - mx)
        ps.append(p.astype(bf16))
        rs.append(1.0 / den)
    pm = jnp.concatenate(ps, axis=1)
    o = jnp.dot(pm, vbd, preferred_element_type=f32)
    ol = lax.broadcasted_iota(i32, (m_rows, GROUP * HEAD_DIM), 1) // HEAD_DIM
    scale = jnp.where(ol == 0, rs[0], jnp.where(ol == 1, rs[1], jnp.where(ol == 2, rs[2], rs[3])))
    return o * scale


def _bias_kernel(rb_ref, bkr_ref, bkm_ref, br_ref, bm_ref):
    for h in range(N_Q_HEADS):
        for t in range(2):
            br_ref[t, h] = _bias_from_buckets(bkr_ref[t], rb_ref, h)
        bm_ref[h] = _bias_from_buckets(bkm_ref[...], rb_ref, h)


def _bias_tables(rel_bias):
    return pl.pallas_call(
        _bias_kernel,
        in_specs=[_smem(), pl.BlockSpec(memory_space=pltpu.VMEM), pl.BlockSpec(memory_space=pltpu.VMEM)],
        out_specs=[pl.BlockSpec(memory_space=pltpu.VMEM), pl.BlockSpec(memory_space=pltpu.VMEM)],
        out_shape=[
            jax.ShapeDtypeStruct((2, N_Q_HEADS, BLOCK, KSEG), f32),
            jax.ShapeDtypeStruct((N_Q_HEADS, N_META_ROWS, N_META_ROWS), f32),
        ],
        name="rel_bias_tables",
    )(rel_bias, jnp.asarray(_BK_REAL), jnp.asarray(_BK_META))


def _attn_kernel(sink_ref, q_ref, k_ref, v_ref, kp_ref, vp_ref, km_ref, vm_ref,
                 br_ref, bm_ref, o_ref):
    s = pl.program_id(0)
    gw = GROUP * HEAD_DIM

    @pl.when(s == 0)
    def _():
        kxs = _expand_kv(k_ref[:N_META_ROWS, :])
        vxs = _expand_kv(v_ref[:N_META_ROWS, :])
        for g in range(N_KV_HEADS):
            o = _attend(q_ref[:N_META_ROWS, g * gw:(g + 1) * gw], kxs[g], vxs[g],
                        lambda h, g=g: bm_ref[g * GROUP + h],
                        lambda h, g=g: sink_ref[g * GROUP + h], N_META_ROWS)
            o_ref[:N_META_ROWS, g * gw:(g + 1) * gw] = o.astype(bf16)
        o_ref[N_META_ROWS:, :] = jnp.zeros((TM - N_META_ROWS, ATTN_WIDTH), bf16)

    @pl.when(s > 0)
    def _():
        first = ((s - 1) % TILES_PER_BATCH) == 0
        tbl0 = jnp.where(first, 0, 1)
        zpad = jnp.zeros((BLOCK - N_META, KV_WIDTH), bf16)
        kmeta = jnp.concatenate([km_ref[...], zpad], axis=0)
        vmeta = jnp.concatenate([vm_ref[...], zpad], axis=0)
        for blk in range(BLOCKS_PER_TILE):
            rows = slice(blk * BLOCK, (blk + 1) * BLOCK)
            if blk == 0:
                kprev, vprev = kp_ref[...], vp_ref[...]
            else:
                prows = slice((blk - 1) * BLOCK, blk * BLOCK)
                kprev, vprev = k_ref[prows, :], v_ref[prows, :]
            kxs = _expand_kv(jnp.concatenate([kmeta, kprev, k_ref[rows, :]], axis=0))
            vxs = _expand_kv(jnp.concatenate([vmeta, vprev, v_ref[rows, :]], axis=0))
            for g in range(N_KV_HEADS):
                if blk == 0:
                    bias = lambda h, g=g: br_ref[tbl0, g * GROUP + h]
                else:
                    bias = lambda h, g=g: br_ref[1, g * GROUP + h]
                o = _attend(q_ref[rows, g * gw:(g + 1) * gw], kxs[g], vxs[g], bias,
                            lambda h, g=g: sink_ref[g * GROUP + h], KSEG)
                o_ref[rows, g * gw:(g + 1) * gw] = o.astype(bf16)


def _attention(bias_real, bias_meta, sinks, q, k, v):
    row = lambda w: pl.BlockSpec((TM, w), lambda s: (s, 0))
    prev = pl.BlockSpec((BLOCK, KV_WIDTH), lambda s: (jnp.maximum(s * BLOCKS_PER_TILE - 1, 0), 0))
    meta = pl.BlockSpec((N_META, KV_WIDTH), lambda s: (jnp.maximum(s - 1, 0) // TILES_PER_BATCH, 0))
    return pl.pallas_call(
        _attn_kernel,
        grid=(N_TILES,),
        in_specs=[
            _smem(),
            row(ATTN_WIDTH), row(KV_WIDTH), row(KV_WIDTH),
            prev, prev, meta, meta,
            _full((2, N_Q_HEADS, BLOCK, KSEG)), _full((N_Q_HEADS, N_META_ROWS, N_META_ROWS)),
        ],
        out_specs=row(ATTN_WIDTH),
        out_shape=jax.ShapeDtypeStruct((NT, ATTN_WIDTH), bf16),
        compiler_params=_cparams("arbitrary"),
        name="swa_attention",
    )(sinks, q, k, v, k, v, k, v, bias_real, bias_meta)


SPREAD = TM // BATCH
SH_ROWS = CONV_HALO + TM - 8
GATE_BLOCKS = 2 * D_MODEL // MXU_COLS


def _conv_window(s, glu_ref, gprev_ref, gmeta_ref, win_ref, sh_ref):
    is_meta = s == 0
    first = ((s - 1) % TILES_PER_BATCH) == 0
    glu = glu_ref[...].astype(f32)
    zgap = jnp.zeros((SPREAD - N_META, CONV_CH), f32)
    spread = jnp.concatenate(
        [p for bb in range(BATCH) for p in (zgap, glu[bb * N_META:(bb + 1) * N_META])], axis=0)
    win_ref[CONV_HALO:, :] = jnp.where(is_meta, spread, glu)
    halo_first = jnp.concatenate(
        [jnp.zeros((CONV_HALO - N_META, CONV_CH), f32), gmeta_ref[...].astype(f32)], axis=0)
    halo = jnp.where(first, halo_first, gprev_ref[...].astype(f32))
    win_ref[:CONV_HALO, :] = jnp.where(is_meta, 0.0, halo)

    for r in range(1, 8):
        sh_ref[r - 1] = win_ref[r:r + SH_ROWS, :]


def _conv_finish(s, dw_ref, cb_ref, cg_ref, cbeta_ref, win_ref, sh_ref, cv_ref):
    off = CONV_HALO - (CONV_WIDTH - 1)
    for rc in range(TM // 128):
        for lc in range(CONV_CH // 128):
            ls = slice(lc * 128, (lc + 1) * 128)
            acc = None
            for t in range(CONV_WIDTH):
                r = (off + t) % 8
                r0 = rc * 128 + (off + t) - r
                x = win_ref[r0:r0 + 128, ls] if r == 0 else sh_ref[r - 1, r0:r0 + 128, ls]
                term = dw_ref[t:t + 1, ls] * x
                acc = term if acc is None else acc + term
            cv_ref[rc * 128:(rc + 1) * 128, ls] = acc
    y = _ln(cv_ref[...] + cb_ref[...], cg_ref[...], cbeta_ref[...])
    c = y * jax.nn.sigmoid(y)
    gathered = jnp.concatenate(
        [c[bb * SPREAD + SPREAD - N_META:(bb + 1) * SPREAD] for bb in range(BATCH)]
        + [jnp.zeros((TM - N_META_ROWS, CONV_CH), f32)], axis=0)
    return jnp.where(s == 0, gathered, c).astype(bf16)


def _mix_kernel(h_ref, a_ref, glu_ref, gprev_ref, gmeta_ref, dw_ref, cb_ref, cg_ref, cbeta_ref, *rest):
    wg_refs = rest[:GATE_BLOCKS]
    (wap_ref, wcp_ref, wo_ref, g_ref, b_ref, o_ref,
     win_ref, sh_ref, cv_ref, wgb_ref, wapb_ref, wcpb_ref, wob_ref) = rest[GATE_BLOCKS:]
    s = pl.program_id(0)

    @pl.when(s == 0)
    def _():
        for k in range(GATE_BLOCKS):
            wgb_ref[:, k * MXU_COLS:(k + 1) * MXU_COLS] = wg_refs[k][...].astype(bf16)
        wapb_ref[...] = wap_ref[...].astype(bf16)
        wcpb_ref[...] = wcp_ref[...].astype(bf16)
        wob_ref[...] = wo_ref[...].astype(bf16)

    h = h_ref[...]
    gates = jnp.dot(h.astype(bf16), wgb_ref[...], preferred_element_type=f32)
    ya = jnp.dot(a_ref[...], wapb_ref[...], preferred_element_type=f32)
    _conv_window(s, glu_ref, gprev_ref, gmeta_ref, win_ref, sh_ref)
    c = _conv_finish(s, dw_ref, cb_ref, cg_ref, cbeta_ref, win_ref, sh_ref, cv_ref)
    yc = jnp.dot(c, wcpb_ref[...], preferred_element_type=f32)
    m = jax.nn.sigmoid(gates[:, :D_MODEL]) * ya + jax.nn.sigmoid(gates[:, D_MODEL:]) * yc
    mix = jnp.dot(m.astype(bf16), wob_ref[...], preferred_element_type=f32)
    o_ref[...] = _ln(ALPHA * h + mix, g_ref[...], b_ref[...])


def _mix(layer, h, a, glu, dw, cb, cg, cbeta, w_in, wap, wcp, wo, g, b):
    row = lambda w: pl.BlockSpec((TM, w), lambda s: (s, 0))
    gate_block = lambda k: pl.BlockSpec((None, D_MODEL, MXU_COLS),
                                        lambda s: (layer, 0, GLU_END // MXU_COLS + k),
                                        pipeline_mode=pl.Buffered(1))
    gprev = pl.BlockSpec((CONV_HALO, CONV_CH),
                         lambda s: (jnp.maximum(s * (TM // CONV_HALO) - 1, 0), 0))
    gmeta = pl.BlockSpec((N_META, CONV_CH), lambda s: (jnp.maximum(s - 1, 0) // TILES_PER_BATCH, 0))
    per_layer = lambda r, c: pl.BlockSpec((None, r, c), lambda s: (layer, 0, 0), pipeline_mode=pl.Buffered(1))
    return pl.pallas_call(
        _mix_kernel,
        grid=(N_TILES,),
        in_specs=[
            row(D_MODEL), row(ATTN_WIDTH), row(CONV_CH), gprev, gmeta,
            _full((CONV_WIDTH, CONV_CH)), _full((1, CONV_CH)), _full((1, CONV_CH)), _full((1, CONV_CH)),
            *[gate_block(k) for k in range(GATE_BLOCKS)], per_layer(ATTN_WIDTH, D_MODEL),
            per_layer(CONV_CH, D_MODEL), per_layer(D_MODEL, D_MODEL),
            _full((1, D_MODEL)), _full((1, D_MODEL)),
        ],
        out_specs=row(D_MODEL),
        out_shape=jax.ShapeDtypeStruct((NT, D_MODEL), f32),
        scratch_shapes=[
            pltpu.VMEM((CONV_HALO + TM, CONV_CH), f32),
            pltpu.VMEM((7, SH_ROWS, CONV_CH), f32),
            pltpu.VMEM((TM, CONV_CH), f32),
            pltpu.VMEM((D_MODEL, 2 * D_MODEL), bf16),
            pltpu.VMEM((ATTN_WIDTH, D_MODEL), bf16),
            pltpu.VMEM((CONV_CH, D_MODEL), bf16),
            pltpu.VMEM((D_MODEL, D_MODEL), bf16),
        ],
        compiler_params=_cparams("arbitrary"),
        name="conv_mix_ln1",
    )(h, a, glu, glu, glu, dw, cb, cg, cbeta, *([w_in] * GATE_BLOCKS), wap, wcp, wo, g, b)


def _swiglu_chunks(xb, wg_ref, wu_ref, wd_ref, chunks, row_scale=None):
    acc = None
    c0 = 0
    for w in chunks:
        gt = jnp.dot(xb, wg_ref[:, c0:c0 + w].astype(bf16), preferred_element_type=f32)
        up = jnp.dot(xb, wu_ref[:, c0:c0 + w].astype(bf16), preferred_element_type=f32)
        act = gt * jax.nn.sigmoid(gt) * up
        if row_scale is not None:
            act = act * row_scale
        part = jnp.dot(act.astype(bf16), wd_ref[c0:c0 + w, :].astype(bf16), preferred_element_type=f32)
        acc = part if acc is None else acc + part
        c0 += w
    return acc


def _ffn_kernel(h_ref, wg_ref, wu_ref, wd_ref, g_ref, b_ref, o_ref):
    h = h_ref[...]
    f = _swiglu_chunks(h.astype(bf16), wg_ref, wu_ref, wd_ref, FF_CHUNKS_DENSE)
    o_ref[...] = _ln(ALPHA * h + f, g_ref[...], b_ref[...])


def _dense_ffn(j, h, wg, wu, wd, g, b):
    row = pl.BlockSpec((TM, D_MODEL), lambda s: (s, 0))
    per_layer = lambda r, c: pl.BlockSpec((None, r, c), lambda s: (j, 0, 0), pipeline_mode=pl.Buffered(1))
    return pl.pallas_call(
        _ffn_kernel,
        grid=(N_TILES,),
        in_specs=[row, per_layer(D_MODEL, D_FF), per_layer(D_MODEL, D_FF), per_layer(D_FF, D_MODEL),
                  _full((1, D_MODEL)), _full((1, D_MODEL))],
        out_specs=row,
        out_shape=jax.ShapeDtypeStruct((NT, D_MODEL), f32),
        compiler_params=_cparams("parallel"),
        name="dense_ffn_ln2",
    )(h, wg, wu, wd, g, b)


def _route_copy(src_ref, xs_ref, sem, e, row):
    return pltpu.make_async_copy(
        src_ref.at[pl.ds(e * CHUNK, CHUNK), :],
        xs_ref.at[pl.ds(pl.multiple_of(e * CAP + row, ALIGN), CHUNK), :],
        sem)


def _zero_copy(zbuf_ref, xs_ref, sem, e, row):
    return pltpu.make_async_copy(
        zbuf_ref,
        xs_ref.at[pl.ds(pl.multiple_of(e * CAP + row, ALIGN), TM), :],
        sem)


def _dispatch_pass(p, ranki, sel, gate, hi, dst_ref):
    slot_id = lax.broadcasted_iota(i32, (CHUNK, TM), 0) + p * CHUNK
    lane = lax.broadcasted_iota(i32, (CHUNK, 128), 1)
    ps = []
    for e in range(N_EXPERTS):
        pe = (slot_id == ranki[e:e + 1, :]) & sel[e:e + 1, :]
        ps.append(jnp.where(pe, 1.0, 0.0).astype(bf16))
        gs = jnp.sum(jnp.where(pe, gate[e:e + 1, :], 0.0), axis=1, keepdims=True)
        g_hi = gs.astype(bf16).astype(f32)
        g_mid = (gs - g_hi).astype(bf16).astype(f32)
        g_lo = gs - g_hi - g_mid
        parts = jnp.where(lane == 0, g_hi, jnp.where(lane == 1, g_mid, jnp.where(lane == 2, g_lo, 0.0)))
        dst_ref[e * CHUNK:(e + 1) * CHUNK, D_MODEL:] = parts.astype(bf16)
    pall = jnp.concatenate(ps, axis=0)
    dst_ref[:, :D_MODEL] = jnp.dot(pall, hi, preferred_element_type=f32).astype(bf16)


def _route_kernel(h_ref, r_ref, info_ref, base_ref, rows_ref, te_ref, tb_ref, tv_ref, xs_ref,
                  xbuf_ref, xov_ref, zbuf_ref, cnt_ref, sem_ref):
    s = pl.program_id(0)
    slot = s % 2

    @pl.when(s == 0)
    def _():
        for e in range(N_EXPERTS):
            cnt_ref[e] = 0
        zbuf_ref[...] = jnp.zeros((TM, XS_W), bf16)

    h = h_ref[...]
    hi = h.astype(bf16)
    lo = (h - hi.astype(f32)).astype(bf16)
    pa = jnp.dot(hi, r_ref[...], preferred_element_type=f32)
    pb = jnp.dot(lo, r_ref[...], preferred_element_type=f32)
    logits = pa + pltpu.roll(pa, 128 - N_EXPERTS, axis=1) + pb
    lt = logits.T[:N_EXPERTS, :]

    eid = lax.broadcasted_iota(i32, (N_EXPERTS, TM), 0)
    m1 = jnp.max(lt, axis=0, keepdims=True)
    i1 = jnp.min(jnp.where(lt == m1, eid, N_EXPERTS), axis=0, keepdims=True)
    lt2 = jnp.where(eid == i1, NEG, lt)
    m2 = jnp.max(lt2, axis=0, keepdims=True)
    i2 = jnp.min(jnp.where(lt2 == m2, eid, N_EXPERTS), axis=0, keepdims=True)
    ex = jnp.exp(m2 - m1)
    g1 = 1.0 / (1.0 + ex)
    g2 = ex * g1
    tok = lax.broadcasted_iota(i32, (1, TM), 1)
    valid = (s > 0) | (tok < N_META_ROWS)
    sel1 = (eid == i1) & valid
    sel2 = (eid == i2) & valid
    sel = sel1 | sel2
    onehot = jnp.where(sel, 1.0, 0.0)
    tri = jnp.where(lax.broadcasted_iota(i32, (TM, TM), 0) < lax.broadcasted_iota(i32, (TM, TM), 1), 1.0, 0.0)
    rank = jnp.dot(onehot, tri, preferred_element_type=f32)
    gate = jnp.where(sel1, g1, jnp.where(sel2, g2, 0.0))
    rank1 = jnp.sum(jnp.where(sel1, rank, 0.0), axis=0, keepdims=True)
    rank2 = jnp.sum(jnp.where(sel2, rank, 0.0), axis=0, keepdims=True)
    info = jnp.concatenate([i1.astype(f32), i2.astype(f32), rank1, rank2,
                            jnp.zeros((128 - 4, TM), f32)], axis=0)
    info_ref[...] = info.T
    ranki = rank.astype(i32)

    _dispatch_pass(0, ranki, sel, gate, hi, xbuf_ref.at[slot])

    @pl.when(s > 0)
    def _():
        for e in range(N_EXPERTS):
            _route_copy(xbuf_ref.at[1 - slot], xs_ref, sem_ref.at[1 - slot, e], e, 0).wait()

    bases, counts = [], []
    for e in range(N_EXPERTS):
        base = cnt_ref[e]
        n_e = jnp.sum(onehot[e:e + 1, :]).astype(i32)
        bases.append(base)
        counts.append(n_e)
        base_ref[s, e] = base
        _route_copy(xbuf_ref.at[slot], xs_ref, sem_ref.at[slot, e], e, base).start()
        cnt_ref[e] = base + ((n_e + (ALIGN - 1)) // ALIGN) * ALIGN
    most = functools.reduce(jnp.maximum, counts)

    for p in range(1, N_PASS):
        @pl.when(most > p * CHUNK)
        def _(p=p):
            _dispatch_pass(p, ranki, sel, gate, hi, xov_ref)
            for e in range(N_EXPERTS):
                _route_copy(xov_ref, xs_ref, sem_ref.at[2, e], e, bases[e] + p * CHUNK).start()
            for e in range(N_EXPERTS):
                _route_copy(xov_ref, xs_ref, sem_ref.at[2, e], e, 0).wait()

    @pl.when(s == N_TILES - 1)
    def _():
        for e in range(N_EXPERTS):
            _route_copy(xbuf_ref.at[slot], xs_ref, sem_ref.at[slot, e], e, 0).wait()
        used = [cnt_ref[e] for e in range(N_EXPERTS)]
        for e in range(N_EXPERTS):
            _zero_copy(zbuf_ref, xs_ref, sem_ref.at[2, e], e, used[e]).start()
        starts, ends, total = [], [], 0
        for e in range(N_EXPERTS):
            n_t = jnp.maximum(lax.shift_right_logical(used[e] + (TM - 1), TM.bit_length() - 1), 1)
            rows_ref[e] = n_t * TM
            starts.append(total)
            total = total + n_t
            ends.append(total)

        def tile_entry(t):
            e_t = sum(jnp.where(t >= ends[e], 1, 0) for e in range(N_EXPERTS - 1))
            first = sum(jnp.where(e_t == e, starts[e], 0) for e in range(N_EXPERTS))
            return e_t, e_t * (CAP // TM) + (t - first)

        e_last, blk_last = tile_entry(total - 1)
        for t in range(FFN_GRID):
            e_t, blk_t = tile_entry(t)
            live = t < total
            te_ref[t] = jnp.where(live, e_t, e_last)
            tb_ref[t] = jnp.where(live, blk_t, blk_last)
            tv_ref[t] = jnp.where(live, 1, 0)
        for e in range(N_EXPERTS):
            _zero_copy(zbuf_ref, xs_ref, sem_ref.at[2, e], e, 0).wait()


def _route(h, rsplit):
    return pl.pallas_call(
        _route_kernel,
        grid=(N_TILES,),
        in_specs=[pl.BlockSpec((TM, D_MODEL), lambda s: (s, 0)), _full((D_MODEL, 128))],
        out_specs=[
            pl.BlockSpec((TM, 128), lambda s: (s, 0)),
            _smem(), _smem(), _smem(), _smem(), _smem(),
            pl.BlockSpec(memory_space=pl.ANY),
        ],
        out_shape=[
            jax.ShapeDtypeStruct((NT, 128), f32),
            jax.ShapeDtypeStruct((N_TILES, N_EXPERTS), i32),
            jax.ShapeDtypeStruct((N_EXPERTS,), i32),
            jax.ShapeDtypeStruct((FFN_GRID,), i32),
            jax.ShapeDtypeStruct((FFN_GRID,), i32),
            jax.ShapeDtypeStruct((FFN_GRID,), i32),
            jax.ShapeDtypeStruct((N_EXPERTS * CAP, XS_W), bf16),
        ],
        scratch_shapes=[
            pltpu.VMEM((2, N_EXPERTS * CHUNK, XS_W), bf16),
            pltpu.VMEM((N_EXPERTS * CHUNK, XS_W), bf16),
            pltpu.VMEM((TM, XS_W), bf16),
            pltpu.SMEM((N_EXPERTS,), i32),
            pltpu.SemaphoreType.DMA((3, N_EXPERTS)),
        ],
        compiler_params=_cparams("arbitrary"),
        name="moe_route_dispatch",
    )(h, rsplit)


def _expert_kernel(te_ref, tb_ref, tv_ref, x_ref, wg_ref, wu_ref, wd_ref, y_ref):
    t = pl.program_id(0)

    @pl.when(tv_ref[t] > 0)
    def _():
        x = x_ref[...]
        gate = jnp.sum(x[:, D_MODEL:].astype(f32), axis=1, keepdims=True)
        y = _swiglu_chunks(x[:, :D_MODEL], wg_ref, wu_ref, wd_ref, FF_CHUNKS_MOE, row_scale=gate)
        y_ref[...] = y.astype(bf16)


def _expert_ffn(j, tile_e, tile_blk, tile_valid, xs, wg, wu, wd):
    per_expert = lambda r, c: pl.BlockSpec((None, None, r, c), lambda t, te, tb, tv: (j, te[t], 0, 0))
    grid_spec = pltpu.PrefetchScalarGridSpec(
        num_scalar_prefetch=3,
        grid=(FFN_GRID,),
        in_specs=[
            pl.BlockSpec((TM, XS_W), lambda t, te, tb, tv: (tb[t], 0)),
            per_expert(D_MODEL, D_EXPERT), per_expert(D_MODEL, D_EXPERT), per_expert(D_EXPERT, D_MODEL),
        ],
        out_specs=pl.BlockSpec((TM, D_MODEL), lambda t, te, tb, tv: (tb[t], 0)),
    )
    return pl.pallas_call(
        _expert_kernel,
        grid_spec=grid_spec,
        out_shape=jax.ShapeDtypeStruct((N_EXPERTS * CAP, D_MODEL), bf16),
        compiler_params=_cparams("arbitrary"),
        name="moe_expert_ffn",
    )(tile_e, tile_blk, tile_valid, xs, wg, wu, wd)


def _combine_copy(ys_ref, dst_ref, sem, e, row):
    return pltpu.make_async_copy(
        ys_ref.at[pl.ds(pl.multiple_of(e * CAP + row, ALIGN), CHUNK), :],
        dst_ref.at[pl.ds(e * CHUNK, CHUNK), :],
        sem)


def _window(base_ref, rows_ref, s, e, p):
    want = base_ref[s, e] + p * CHUNK
    start = jnp.minimum(want, rows_ref[e] - CHUNK)
    return start, want - start


def _onehot_cols(info, deltas, p):
    col = lax.broadcasted_iota(i32, (TM, N_EXPERTS * CHUNK), 1)
    hit = None
    for k in range(2):
        ek = info[:, k:k + 1].astype(i32)
        rk = info[:, 2 + k:3 + k].astype(i32) - p * CHUNK
        dk = functools.reduce(lambda a, b: a + b,
                              [jnp.where(ek == e, deltas[e], 0) for e in range(N_EXPERTS)])
        ck = jnp.where((rk >= 0) & (rk < CHUNK), ek * CHUNK + rk + dk, -1)
        hk = col == ck
        hit = hk if hit is None else hit | hk
    return jnp.where(hit, 1.0, 0.0).astype(bf16)


def _combine_kernel(base_ref, rows_ref, h_ref, info_ref, ys_ref, g_ref, b_ref, o_ref,
                    ybuf_ref, yov_ref, f_ref, sem_ref):
    i = pl.program_id(0)
    n = pl.num_programs(0)
    s = i + 1
    slot = i % 2

    def start_fetch(tile, to_slot):
        for e in range(N_EXPERTS):
            start, _ = _window(base_ref, rows_ref, tile, e, 0)
            _combine_copy(ys_ref, ybuf_ref.at[to_slot], sem_ref.at[to_slot, e], e, start).start()

    @pl.when(i == 0)
    def _():
        start_fetch(s, slot)

    @pl.when(i + 1 < n)
    def _():
        start_fetch(s + 1, 1 - slot)

    info = info_ref[...]
    deltas = [_window(base_ref, rows_ref, s, e, 0)[1] for e in range(N_EXPERTS)]
    qm = _onehot_cols(info, deltas, 0)
    for e in range(N_EXPERTS):
        _combine_copy(ys_ref, ybuf_ref.at[slot], sem_ref.at[slot, e], e, 0).wait()
    f_ref[...] = jnp.dot(qm, ybuf_ref[slot], preferred_element_type=f32)

    deepest = jnp.max(jnp.maximum(info[:, 2:3], info[:, 3:4])).astype(i32)
    for p in range(1, N_PASS):
        @pl.when(deepest >= p * CHUNK)
        def _(p=p):
            wins = [_window(base_ref, rows_ref, s, e, p) for e in range(N_EXPERTS)]
            for e in range(N_EXPERTS):
                _combine_copy(ys_ref, yov_ref, sem_ref.at[2, e], e, wins[e][0]).start()
            qp = _onehot_cols(info, [w[1] for w in wins], p)
            for e in range(N_EXPERTS):
                _combine_copy(ys_ref, yov_ref, sem_ref.at[2, e], e, 0).wait()
            f_ref[...] += jnp.dot(qp, yov_ref[...], preferred_element_type=f32)

    o_ref[...] = _ln(ALPHA * h_ref[...] + f_ref[...], g_ref[...], b_ref[...])


def _combine(base, rows, h, info, ys, g, b):
    n_out = N_TILES - 1
    grid_spec = pltpu.PrefetchScalarGridSpec(
        num_scalar_prefetch=2,
        grid=(n_out,),
        in_specs=[
            pl.BlockSpec((TM, D_MODEL), lambda i, base, rows: (i + 1, 0)),
            pl.BlockSpec((TM, 128), lambda i, base, rows: (i + 1, 0)),
            pl.BlockSpec(memory_space=pl.ANY),
            pl.BlockSpec((1, D_MODEL), lambda i, base, rows: (0, 0)),
            pl.BlockSpec((1, D_MODEL), lambda i, base, rows: (0, 0)),
        ],
        out_specs=pl.BlockSpec((TM, D_MODEL), lambda i, base, rows: (i, 0)),
        scratch_shapes=[
            pltpu.VMEM((2, N_EXPERTS * CHUNK, D_MODEL), bf16),
            pltpu.VMEM((N_EXPERTS * CHUNK, D_MODEL), bf16),
            pltpu.VMEM((TM, D_MODEL), f32),
            pltpu.SemaphoreType.DMA((3, N_EXPERTS)),
        ],
    )
    return pl.pallas_call(
        _combine_kernel,
        grid_spec=grid_spec,
        out_shape=jax.ShapeDtypeStruct((n_out * TM, D_MODEL), f32),
        compiler_params=_cparams("arbitrary"),
        name="moe_combine_ln2",
    )(base, rows, h, info, ys, g, b)


def _split2(x):
    hi = x.astype(bf16)
    lo = (x - hi.astype(f32)).astype(bf16)
    return hi, lo


def kernel(x, meta_tokens, emb_ln_g, emb_ln_b, rel_bias, w_in, conv_dw, conv_b, conv_ln_g, conv_ln_b, sinks, w_attn_proj, w_conv_proj, w_out, ln1_g, ln1_b, ffn_w_gate, ffn_w_up, ffn_w_down, router, moe_w_gate, moe_w_up, moe_w_down, ln2_g, ln2_b):
    row = lambda v: v.reshape(1, -1)
    h = None
    out = None
    bias_real, bias_meta = _bias_tables(rel_bias)
    for i in range(DEPTH):
        if i == 0:
            q, k, v, glu, h = _inproj(i, w_in, x2d=x.reshape(BATCH * SEQ, D_MODEL), meta=meta_tokens,
                                      eg=row(emb_ln_g), eb=row(emb_ln_b))
        else:
            q, k, v, glu = _inproj(i, w_in, h=h)
        a = _attention(bias_real, bias_meta, sinks[i], q, k, v)
        h = _mix(i, h, a, glu, conv_dw[i], row(conv_b[i]), row(conv_ln_g[i]), row(conv_ln_b[i]),
                 w_in, w_attn_proj, w_conv_proj, w_out, row(ln1_g[i]), row(ln1_b[i]))
        j = i // 2
        if i % 2 == 0:
            h = _dense_ffn(j, h, ffn_w_gate, ffn_w_up, ffn_w_down, row(ln2_g[i]), row(ln2_b[i]))
        else:
            r_hi, r_lo = _split2(router[j])
            rsplit = jnp.concatenate(
                [r_hi, r_lo, jnp.zeros((D_MODEL, 128 - 2 * N_EXPERTS), bf16)], axis=1)
            info, base, rows, tile_e, tile_blk, tile_valid, xs = _route(h, rsplit)
            ys = _expert_ffn(j, tile_e, tile_blk, tile_valid, xs, moe_w_gate, moe_w_up, moe_w_down)
            out = _combine(base, rows, h, info, ys, row(ln2_g[i]), row(ln2_b[i]))
    return out.reshape(BATCH, SEQ, D_MODEL)
```

```python
import functools
import math

import numpy as np
import jax
import jax.numpy as jnp
from jax import lax
from jax.experimental import pallas as pl
from jax.experimental.pallas import tpu as pltpu

f32 = jnp.float32
bf16 = jnp.bfloat16
i32 = jnp.int32

D_MODEL = 1024
BATCH = 8
SEQ = 2048
DEPTH = 2
N_META = 16
HEAD_DIM = 64
N_Q_HEADS = 8
N_KV_HEADS = 2
GROUP = N_Q_HEADS // N_KV_HEADS
ATTN_WIDTH = N_Q_HEADS * HEAD_DIM
KV_WIDTH = N_KV_HEADS * HEAD_DIM
WINDOW = 128
BLOCK = 128
CONV_CH = D_MODEL // 2
CONV_WIDTH = 31
N_BUCKETS = 32
MAX_DISTANCE = 128
D_FF = 2816
N_EXPERTS = 8
D_EXPERT = 1408
ALPHA = (2 * DEPTH) ** 0.25
LN_EPS = 1e-5
Q_END = ATTN_WIDTH
K_END = Q_END + KV_WIDTH
V_END = K_END + KV_WIDTH
GLU_END = V_END + 2 * CONV_CH
GA_END = GLU_END + D_MODEL
GC_END = GA_END + D_MODEL

TM = 512
N_META_ROWS = BATCH * N_META
NT = TM + BATCH * SEQ
N_TILES = NT // TM
TILES_PER_BATCH = SEQ // TM
BLOCKS_PER_TILE = TM // BLOCK
KSEG = 3 * BLOCK
CONV_HALO = 32
NEG = -1e30

CHUNK = 192
N_PASS = -(-TM // CHUNK)
ALIGN = 16
XS_W = D_MODEL + 128
CAP = 35 * TM
FFN_GRID = 85
MXU_COLS = 256
FF_CHUNKS_DENSE = (256,) * 11
FF_CHUNKS_MOE = (256,) * 5 + (128,)

VMEM_LIMIT = 56 * 1024 * 1024


def _cparams(sem="arbitrary", flags=None):
    return pltpu.CompilerParams(dimension_semantics=(sem,), vmem_limit_bytes=VMEM_LIMIT, flags=flags)


def _ln(x, g, b):
    mu = jnp.mean(x, -1, keepdims=True)
    xc = x - mu
    var = jnp.mean(xc * xc, -1, keepdims=True)
    return xc * lax.rsqrt(var + LN_EPS) * g + b


def _full(shape):
    return pl.BlockSpec(shape, lambda *_: (0,) * len(shape))


def _smem():
    return pl.BlockSpec(memory_space=pltpu.SMEM)


def _embed_tile(s, x_ref, meta_ref, g_ref, b_ref):
    m = _ln(meta_ref[...], g_ref[...], b_ref[...])
    meta_tile = jnp.concatenate([m] * BATCH + [jnp.zeros((TM - N_META_ROWS, D_MODEL), f32)], axis=0)
    return jnp.where(s == 0, meta_tile, _ln(x_ref[...], g_ref[...], b_ref[...]))


def _inproj_kernel(embed, *refs):
    s = pl.program_id(0)
    if embed:
        x_ref, meta_ref, eg_ref, eb_ref, w_ref, q_ref, k_ref, v_ref, glu_ref, h_ref, wb_ref = refs
        h = _embed_tile(s, x_ref, meta_ref, eg_ref, eb_ref)
        h_ref[...] = h
    else:
        hin_ref, w_ref, q_ref, k_ref, v_ref, glu_ref, wb_ref = refs
        h = hin_ref[...]

    @pl.when(s == 0)
    def _():
        wb_ref[...] = w_ref[...].astype(bf16)

    u = jnp.dot(h.astype(bf16), wb_ref[...], preferred_element_type=f32)
    q_ref[...] = (u[:, :Q_END] * (HEAD_DIM ** -0.5)).astype(bf16)
    k_ref[...] = u[:, Q_END:K_END].astype(bf16)
    v_ref[...] = u[:, K_END:V_END].astype(bf16)
    glu = u[:, V_END:V_END + CONV_CH] * jax.nn.sigmoid(u[:, V_END + CONV_CH:GLU_END])
    glu_ref[...] = glu.astype(bf16)


def _inproj(layer, w_in, h=None, x2d=None, meta=None, eg=None, eb=None):
    embed = h is None
    row = lambda w: pl.BlockSpec((TM, w), lambda s: (s, 0))
    w_spec = pl.BlockSpec((None, D_MODEL, GLU_END), lambda s: (layer, 0, 0), pipeline_mode=pl.Buffered(1))
    out_specs = [row(ATTN_WIDTH), row(KV_WIDTH), row(KV_WIDTH), row(CONV_CH)]
    out_shape = [
        jax.ShapeDtypeStruct((NT, ATTN_WIDTH), bf16),
        jax.ShapeDtypeStruct((NT, KV_WIDTH), bf16),
        jax.ShapeDtypeStruct((NT, KV_WIDTH), bf16),
        jax.ShapeDtypeStruct((NT, CONV_CH), bf16),
    ]
    if embed:
        in_specs = [
            pl.BlockSpec((TM, D_MODEL), lambda s: (jnp.maximum(s - 1, 0), 0)),
            _full((N_META, D_MODEL)), _full((1, D_MODEL)), _full((1, D_MODEL)), w_spec,
        ]
        args = (x2d, meta, eg, eb, w_in)
        out_specs.append(row(D_MODEL))
        out_shape.append(jax.ShapeDtypeStruct((NT, D_MODEL), f32))
    else:
        in_specs = [row(D_MODEL), w_spec]
        args = (h, w_in)
    return pl.pallas_call(
        functools.partial(_inproj_kernel, embed),
        grid=(N_TILES,),
        in_specs=in_specs,
        out_specs=out_specs,
        out_shape=out_shape,
        scratch_shapes=[pltpu.VMEM((D_MODEL, GLU_END), bf16)],
        compiler_params=_cparams("arbitrary"),
        name="inproj_glu",
    )(*args)


def _rel_bucket_np(dist):
    n = np.maximum(dist, 0)
    max_exact = N_BUCKETS // 2
    nf = np.maximum(n, 1).astype(np.float32)
    scaled = (np.log(nf / np.float32(max_exact)) / np.float32(math.log(MAX_DISTANCE / max_exact))
              * np.float32(N_BUCKETS - max_exact))
    large = np.minimum(max_exact + scaled.astype(np.int32), N_BUCKETS - 1)
    return np.where(n < max_exact, n, large).astype(np.int32)


def _bucket_tables():
    a = np.arange(BLOCK)[:, None]
    m = np.arange(BLOCK)[None, :]
    kk = np.arange(2 * BLOCK)[None, :]
    dist_band = BLOCK + a - kk
    band_ok = (dist_band >= 0) & (dist_band < WINDOW)
    band = np.where(band_ok, _rel_bucket_np(dist_band), -1)
    real = np.full((2, BLOCK, KSEG), -1, np.int32)
    for t, n in enumerate((0, 1)):
        dist_meta = N_META + n * BLOCK + a - m
        real[t, :, :BLOCK] = np.where(m < N_META, _rel_bucket_np(dist_meta), -1)
        real[t, :, BLOCK:] = band
    real[0, :, BLOCK:2 * BLOCK] = -1
    for n in range(2, SEQ // BLOCK):
        dist_meta = N_META + n * BLOCK + a - m
        assert np.array_equal(np.where(m < N_META, _rel_bucket_np(dist_meta), -1), real[1, :, :BLOCK])
    r = np.arange(N_META_ROWS)
    same = (r[:, None] // N_META) == (r[None, :] // N_META)
    dmm = (r[:, None] % N_META) - (r[None, :] % N_META)
    meta = np.where(same & (dmm >= 0), _rel_bucket_np(dmm), -1).astype(np.int32)
    return real, meta


_BK_REAL, _BK_META = _bucket_tables()


def _bias_from_buckets(bk, rb_ref, h):
    acc = jnp.full(bk.shape, NEG, f32)
    for bkt in range(N_BUCKETS):
        acc = jnp.where(bk == bkt, rb_ref[bkt, h], acc)
    return acc


def _expand_kv(x2):
    xf = x2.astype(f32)
    r = pltpu.roll(xf, HEAD_DIM, axis=1)
    low = lax.broadcasted_iota(i32, xf.shape, 1) < HEAD_DIM
    h0 = jnp.where(low, xf, r).astype(bf16)
    h1 = jnp.where(low, r, xf).astype(bf16)
    return (jnp.concatenate([h0, h0], axis=1), jnp.concatenate([h1, h1], axis=1))


def _block_diag(x, n_keys):
    lane_blk = lax.broadcasted_iota(i32, (n_keys, GROUP * HEAD_DIM), 1) // HEAD_DIM
    zero = jnp.zeros_like(x)
    return jnp.concatenate([jnp.where(lane_blk == h, x, zero) for h in range(GROUP)], axis=0)


def _softmax_block(sh, sink):
    mx = jnp.maximum(jnp.max(sh, -1, keepdims=True), sink)
    p = jnp.exp(sh - mx)
    den = jnp.sum(p, -1, keepdims=True) + jnp.exp(sink - mx)
    return p.astype(bf16), 1.0 / den


def _head_scale(rs, m_rows):
    ol = lax.broadcasted_iota(i32, (m_rows, GROUP * HEAD_DIM), 1) // HEAD_DIM
    return jnp.where(ol == 0, rs[0], jnp.where(ol == 1, rs[1], jnp.where(ol == 2, rs[2], rs[3])))


def _attend(qg, kx, vx, bias_of_head, sink_of_head, n_keys):
    kbd = _block_diag(kx, n_keys)
    vbd = _block_diag(vx, n_keys)
    s = lax.dot_general(qg, kbd, (((1,), (1,)), ((), ())), preferred_element_type=f32)
    ps, rs = [], []
    for h in range(GROUP):
        p, r = _softmax_block(s[:, h * n_keys:(h + 1) * n_keys] + bias_of_head(h), sink_of_head(h))
        ps.append(p)
        rs.append(r)
    o = jnp.dot(jnp.concatenate(ps, axis=1), vbd, preferred_element_type=f32)
    return o * _head_scale(rs, qg.shape[0])


def _attend_tile(qg, kbd, vbd, kbd_meta, vbd_meta, bias_of, sink_of_head):
    nt = (((1,), (1,)), ((), ()))
    nb = BLOCKS_PER_TILE
    s_meta = lax.dot_general(qg, kbd_meta, nt, preferred_element_type=f32)
    s_prev, s_cur = [None] * nb, [None] * nb
    for b in range(nb + 1):
        lo, hi = max(b - 1, 0) * BLOCK, min(b + 1, nb) * BLOCK
        sc = lax.dot_general(qg[lo:hi], kbd[b], nt, preferred_element_type=f32)
        if b > 0:
            s_cur[b - 1] = sc[:BLOCK]
        if b < nb:
            s_prev[b] = sc[-BLOCK:]
    pm, pp, pc, scales = [], [], [], []
    for blk in range(nb):
        rows = slice(blk * BLOCK, (blk + 1) * BLOCK)
        pms, pps, pcs, rs = [], [], [], []
        for h in range(GROUP):
            cols = slice(h * BLOCK, (h + 1) * BLOCK)
            sh = jnp.concatenate([s_meta[rows, cols], s_prev[blk][:, cols], s_cur[blk][:, cols]], axis=1)
            p, r = _softmax_block(sh + bias_of(blk, h), sink_of_head(h))
            pms.append(p[:, :BLOCK])
            pps.append(p[:, BLOCK:2 * BLOCK])
            pcs.append(p[:, 2 * BLOCK:])
            rs.append(r)
        pm.append(jnp.concatenate(pms, axis=1))
        pp.append(jnp.concatenate(pps, axis=1))
        pc.append(jnp.concatenate(pcs, axis=1))
        scales.append(_head_scale(rs, BLOCK))
    o = [None] * nb
    for b in range(nb + 1):
        parts = ([pc[b - 1]] if b > 0 else []) + ([pp[b]] if b < nb else [])
        ob = jnp.dot(jnp.concatenate(parts, axis=0), vbd[b], preferred_element_type=f32)
        if b > 0:
            o[b - 1] = o[b - 1] + ob[:BLOCK]
        if b < nb:
            o[b] = ob[-BLOCK:]
    om = jnp.dot(jnp.concatenate(pm, axis=0), vbd_meta, preferred_element_type=f32)
    return (jnp.concatenate(o, axis=0) + om) * jnp.concatenate(scales, axis=0)


def _bias_kernel(rb_ref, bkr_ref, bkm_ref, br_ref, bm_ref):
    for h in range(N_Q_HEADS):
        for t in range(2):
            br_ref[t, h] = _bias_from_buckets(bkr_ref[t], rb_ref, h)
        bm_ref[h] = _bias_from_buckets(bkm_ref[...], rb_ref, h)


def _bias_tables(rel_bias):
    return pl.pallas_call(
        _bias_kernel,
        in_specs=[_smem(), pl.BlockSpec(memory_space=pltpu.VMEM), pl.BlockSpec(memory_space=pltpu.VMEM)],
        out_specs=[pl.BlockSpec(memory_space=pltpu.VMEM), pl.BlockSpec(memory_space=pltpu.VMEM)],
        out_shape=[
            jax.ShapeDtypeStruct((2, N_Q_HEADS, BLOCK, KSEG), f32),
            jax.ShapeDtypeStruct((N_Q_HEADS, N_META_ROWS, N_META_ROWS), f32),
        ],
        name="rel_bias_tables",
    )(rel_bias, jnp.asarray(_BK_REAL), jnp.asarray(_BK_META))


def _attn_kernel(sink_ref, q_ref, k_ref, v_ref, kp_ref, vp_ref, km_ref, vm_ref,
                 br_ref, bm_ref, o_ref):
    s = pl.program_id(0)
    gw = GROUP * HEAD_DIM

    @pl.when(s == 0)
    def _():
        kxs = _expand_kv(k_ref[:N_META_ROWS, :])
        vxs = _expand_kv(v_ref[:N_META_ROWS, :])
        for g in range(N_KV_HEADS):
            o = _attend(q_ref[:N_META_ROWS, g * gw:(g + 1) * gw], kxs[g], vxs[g],
                        lambda h, g=g: bm_ref[g * GROUP + h],
                        lambda h, g=g: sink_ref[g * GROUP + h], N_META_ROWS)
            o_ref[:N_META_ROWS, g * gw:(g + 1) * gw] = o.astype(bf16)
        o_ref[N_META_ROWS:, :] = jnp.zeros((TM - N_META_ROWS, ATTN_WIDTH), bf16)

    @pl.when(s > 0)
    def _():
        first = ((s - 1) % TILES_PER_BATCH) == 0
        tbl0 = jnp.where(first, 0, 1)
        zpad = jnp.zeros((BLOCK - N_META, KV_WIDTH), bf16)
        kms = _expand_kv(jnp.concatenate([km_ref[...], zpad], axis=0))
        vms = _expand_kv(jnp.concatenate([vm_ref[...], zpad], axis=0))
        kxs = [_expand_kv(kp_ref[...])] + [_expand_kv(k_ref[b * BLOCK:(b + 1) * BLOCK, :])
                                          for b in range(BLOCKS_PER_TILE)]
        vxs = [_expand_kv(vp_ref[...])] + [_expand_kv(v_ref[b * BLOCK:(b + 1) * BLOCK, :])
                                          for b in range(BLOCKS_PER_TILE)]
        for g in range(N_KV_HEADS):
            o = _attend_tile(
                q_ref[:, g * gw:(g + 1) * gw],
                [_block_diag(x[g], BLOCK) for x in kxs], [_block_diag(x[g], BLOCK) for x in vxs],
                _block_diag(kms[g], BLOCK), _block_diag(vms[g], BLOCK),
                lambda blk, h, g=g: br_ref[tbl0 if blk == 0 else 1, g * GROUP + h],
                lambda h, g=g: sink_ref[g * GROUP + h])
            o_ref[:, g * gw:(g + 1) * gw] = o.astype(bf16)


def _attention(bias_real, bias_meta, sinks, q, k, v):
    row = lambda w: pl.BlockSpec((TM, w), lambda s: (s, 0))
    prev = pl.BlockSpec((BLOCK, KV_WIDTH), lambda s: (jnp.maximum(s * BLOCKS_PER_TILE - 1, 0), 0))
    meta = pl.BlockSpec((N_META, KV_WIDTH), lambda s: (jnp.maximum(s - 1, 0) // TILES_PER_BATCH, 0))
    return pl.pallas_call(
        _attn_kernel,
        grid=(N_TILES,),
        in_specs=[
            _smem(),
            row(ATTN_WIDTH), row(KV_WIDTH), row(KV_WIDTH),
            prev, prev, meta, meta,
            _full((2, N_Q_HEADS, BLOCK, KSEG)), _full((N_Q_HEADS, N_META_ROWS, N_META_ROWS)),
        ],
        out_specs=row(ATTN_WIDTH),
        out_shape=jax.ShapeDtypeStruct((NT, ATTN_WIDTH), bf16),
        compiler_params=_cparams("arbitrary"),
        name="swa_attention",
    )(sinks, q, k, v, k, v, k, v, bias_real, bias_meta)


SPREAD = TM // BATCH
SH_ROWS = CONV_HALO + TM - 8
GATE_BLOCKS = 2 * D_MODEL // MXU_COLS


def _conv_window(s, glu_ref, gprev_ref, gmeta_ref, win_ref, sh_ref):
    is_meta = s == 0
    first = ((s - 1) % TILES_PER_BATCH) == 0
    glu = glu_ref[...].astype(f32)
    zgap = jnp.zeros((SPREAD - N_META, CONV_CH), f32)
    spread = jnp.concatenate(
        [p for bb in range(BATCH) for p in (zgap, glu[bb * N_META:(bb + 1) * N_META])], axis=0)
    win_ref[CONV_HALO:, :] = jnp.where(is_meta, spread, glu)
    halo_first = jnp.concatenate(
        [jnp.zeros((CONV_HALO - N_META, CONV_CH), f32), gmeta_ref[...].astype(f32)], axis=0)
    halo = jnp.where(first, halo_first, gprev_ref[...].astype(f32))
    win_ref[:CONV_HALO, :] = jnp.where(is_meta, 0.0, halo)

    for r in range(1, 8):
        sh_ref[r - 1] = win_ref[r:r + SH_ROWS, :]


def _conv_finish(s, dw_ref, cb_ref, cg_ref, cbeta_ref, win_ref, sh_ref, cv_ref):
    off = CONV_HALO - (CONV_WIDTH - 1)
    for rc in range(TM // 128):
        for lc in range(CONV_CH // 128):
            ls = slice(lc * 128, (lc + 1) * 128)
            acc = None
            for t in range(CONV_WIDTH):
                r = (off + t) % 8
                r0 = rc * 128 + (off + t) - r
                x = win_ref[r0:r0 + 128, ls] if r == 0 else sh_ref[r - 1, r0:r0 + 128, ls]
                term = dw_ref[t:t + 1, ls] * x
                acc = term if acc is None else acc + term
            cv_ref[rc * 128:(rc + 1) * 128, ls] = acc
    y = _ln(cv_ref[...] + cb_ref[...], cg_ref[...], cbeta_ref[...])
    c = y * jax.nn.sigmoid(y)
    gathered = jnp.concatenate(
        [c[bb * SPREAD + SPREAD - N_META:(bb + 1) * SPREAD] for bb in range(BATCH)]
        + [jnp.zeros((TM - N_META_ROWS, CONV_CH), f32)], axis=0)
    return jnp.where(s == 0, gathered, c).astype(bf16)


def _mix_kernel(h_ref, a_ref, glu_ref, gprev_ref, gmeta_ref, dw_ref, cb_ref, cg_ref, cbeta_ref, *rest):
    wg_refs = rest[:GATE_BLOCKS]
    (wap_ref, wcp_ref, wo_ref, g_ref, b_ref, o_ref,
     win_ref, sh_ref, cv_ref, wgb_ref, wapb_ref, wcpb_ref, wob_ref) = rest[GATE_BLOCKS:]
    s = pl.program_id(0)

    @pl.when(s == 0)
    def _():
        for k in range(GATE_BLOCKS):
            wgb_ref[:, k * MXU_COLS:(k + 1) * MXU_COLS] = wg_refs[k][...].astype(bf16)
        wapb_ref[...] = wap_ref[...].astype(bf16)
        wcpb_ref[...] = wcp_ref[...].astype(bf16)
        wob_ref[...] = wo_ref[...].astype(bf16)

    h = h_ref[...]
    gates = jnp.dot(h.astype(bf16), wgb_ref[...], preferred_element_type=f32)
    ya = jnp.dot(a_ref[...], wapb_ref[...], preferred_element_type=f32)
    _conv_window(s, glu_ref, gprev_ref, gmeta_ref, win_ref, sh_ref)
    c = _conv_finish(s, dw_ref, cb_ref, cg_ref, cbeta_ref, win_ref, sh_ref, cv_ref)
    yc = jnp.dot(c, wcpb_ref[...], preferred_element_type=f32)
    m = jax.nn.sigmoid(gates[:, :D_MODEL]) * ya + jax.nn.sigmoid(gates[:, D_MODEL:]) * yc
    mix = jnp.dot(m.astype(bf16), wob_ref[...], preferred_element_type=f32)
    o_ref[...] = _ln(ALPHA * h + mix, g_ref[...], b_ref[...])


def _mix(layer, h, a, glu, dw, cb, cg, cbeta, w_in, wap, wcp, wo, g, b):
    row = lambda w: pl.BlockSpec((TM, w), lambda s: (s, 0))
    gate_block = lambda k: pl.BlockSpec((None, D_MODEL, MXU_COLS),
                                        lambda s: (layer, 0, GLU_END // MXU_COLS + k),
                                        pipeline_mode=pl.Buffered(1))
    gprev = pl.BlockSpec((CONV_HALO, CONV_CH),
                         lambda s: (jnp.maximum(s * (TM // CONV_HALO) - 1, 0), 0))
    gmeta = pl.BlockSpec((N_META, CONV_CH), lambda s: (jnp.maximum(s - 1, 0) // TILES_PER_BATCH, 0))
    per_layer = lambda r, c: pl.BlockSpec((None, r, c), lambda s: (layer, 0, 0), pipeline_mode=pl.Buffered(1))
    return pl.pallas_call(
        _mix_kernel,
        grid=(N_TILES,),
        in_specs=[
            row(D_MODEL), row(ATTN_WIDTH), row(CONV_CH), gprev, gmeta,
            _full((CONV_WIDTH, CONV_CH)), _full((1, CONV_CH)), _full((1, CONV_CH)), _full((1, CONV_CH)),
            *[gate_block(k) for k in range(GATE_BLOCKS)], per_layer(ATTN_WIDTH, D_MODEL),
            per_layer(CONV_CH, D_MODEL), per_layer(D_MODEL, D_MODEL),
            _full((1, D_MODEL)), _full((1, D_MODEL)),
        ],
        out_specs=row(D_MODEL),
        out_shape=jax.ShapeDtypeStruct((NT, D_MODEL), f32),
        scratch_shapes=[
            pltpu.VMEM((CONV_HALO + TM, CONV_CH), f32),
            pltpu.VMEM((7, SH_ROWS, CONV_CH), f32),
            pltpu.VMEM((TM, CONV_CH), f32),
            pltpu.VMEM((D_MODEL, 2 * D_MODEL), bf16),
            pltpu.VMEM((ATTN_WIDTH, D_MODEL), bf16),
            pltpu.VMEM((CONV_CH, D_MODEL), bf16),
            pltpu.VMEM((D_MODEL, D_MODEL), bf16),
        ],
        compiler_params=_cparams("arbitrary"),
        name="conv_mix_ln1",
    )(h, a, glu, glu, glu, dw, cb, cg, cbeta, *([w_in] * GATE_BLOCKS), wap, wcp, wo, g, b)


def _swiglu_chunks(xb, wg_ref, wu_ref, wd_ref, chunks, row_scale=None):
    acc = None
    c0 = 0
    for w in chunks:
        gt = jnp.dot(xb, wg_ref[:, c0:c0 + w].astype(bf16), preferred_element_type=f32)
        up = jnp.dot(xb, wu_ref[:, c0:c0 + w].astype(bf16), preferred_element_type=f32)
        act = gt * jax.nn.sigmoid(gt) * up
        if row_scale is not None:
            act = act * row_scale
        part = jnp.dot(act.astype(bf16), wd_ref[c0:c0 + w, :].astype(bf16), preferred_element_type=f32)
        acc = part if acc is None else acc + part
        c0 += w
    return acc


def _ffn_kernel(h_ref, wg_ref, wu_ref, wd_ref, g_ref, b_ref, o_ref):
    h = h_ref[...]
    f = _swiglu_chunks(h.astype(bf16), wg_ref, wu_ref, wd_ref, FF_CHUNKS_DENSE)
    o_ref[...] = _ln(ALPHA * h + f, g_ref[...], b_ref[...])


def _dense_ffn(j, h, wg, wu, wd, g, b):
    row = pl.BlockSpec((TM, D_MODEL), lambda s: (s, 0))
    per_layer = lambda r, c: pl.BlockSpec((None, r, c), lambda s: (j, 0, 0), pipeline_mode=pl.Buffered(1))
    return pl.pallas_call(
        _ffn_kernel,
        grid=(N_TILES,),
        in_specs=[row, per_layer(D_MODEL, D_FF), per_layer(D_MODEL, D_FF), per_layer(D_FF, D_MODEL),
                  _full((1, D_MODEL)), _full((1, D_MODEL))],
        out_specs=row,
        out_shape=jax.ShapeDtypeStruct((NT, D_MODEL), f32),
        compiler_params=_cparams("parallel"),
        name="dense_ffn_ln2",
    )(h, wg, wu, wd, g, b)


def _route_copy(src_ref, xs_ref, sem, e, row):
    return pltpu.make_async_copy(
        src_ref.at[pl.ds(e * CHUNK, CHUNK), :],
        xs_ref.at[pl.ds(pl.multiple_of(e * CAP + row, ALIGN), CHUNK), :],
        sem)


def _zero_copy(zbuf_ref, xs_ref, sem, e, row):
    return pltpu.make_async_copy(
        zbuf_ref,
        xs_ref.at[pl.ds(pl.multiple_of(e * CAP + row, ALIGN), TM), :],
        sem)


def _compact(offsets, nrows, qpos, sel, gate, hi):
    slot_id = lax.broadcasted_iota(i32, (nrows, TM), 0)
    lane = lax.broadcasted_iota(i32, (nrows, 128), 1)
    ps, gs = [], []
    for e in range(N_EXPERTS):
        pe = (slot_id + offsets[e] == qpos[e:e + 1, :]) & sel[e:e + 1, :]
        ps.append(jnp.where(pe, 1.0, 0.0).astype(bf16))
        g = jnp.sum(jnp.where(pe, gate[e:e + 1, :], 0.0), axis=1, keepdims=True)
        g_hi = g.astype(bf16).astype(f32)
        g_mid = (g - g_hi).astype(bf16).astype(f32)
        g_lo = g - g_hi - g_mid
        gs.append(jnp.where(lane == 0, g_hi, jnp.where(lane == 1, g_mid, jnp.where(lane == 2, g_lo, 0.0))))
    feats = jnp.dot(jnp.concatenate(ps, axis=0), hi, preferred_element_type=f32)
    return feats, jnp.concatenate(gs, axis=0)


def _dispatch_pass(p, qpos, sel, gate, hi, dst_ref):
    feats, gates = _compact([p * CHUNK] * N_EXPERTS, CHUNK, qpos, sel, gate, hi)
    dst_ref[:, :D_MODEL] = feats.astype(bf16)
    dst_ref[:, D_MODEL:] = gates.astype(bf16)


def _route_kernel(h_ref, r_ref, info_ref, base_ref, rows_ref, te_ref, tb_ref, tv_ref, xs_ref,
                  xbuf_ref, xov_ref, zbuf_ref, tail_ref, cnt_ref, sem_ref):
    s = pl.program_id(0)
    slot = s % 2

    @pl.when(s == 0)
    def _():
        for e in range(N_EXPERTS):
            cnt_ref[e] = 0
        zbuf_ref[...] = jnp.zeros((TM, XS_W), bf16)
        tail_ref[...] = jnp.zeros((N_EXPERTS, ALIGN, XS_W), bf16)

    h = h_ref[...]
    hi = h.astype(bf16)
    lo = (h - hi.astype(f32)).astype(bf16)
    pa = jnp.dot(hi, r_ref[...], preferred_element_type=f32)
    pb = jnp.dot(lo, r_ref[...], preferred_element_type=f32)
    logits = pa + pltpu.roll(pa, 128 - N_EXPERTS, axis=1) + pb
    lt = logits.T[:N_EXPERTS, :]

    eid = lax.broadcasted_iota(i32, (N_EXPERTS, TM), 0)
    m1 = jnp.max(lt, axis=0, keepdims=True)
    i1 = jnp.min(jnp.where(lt == m1, eid, N_EXPERTS), axis=0, keepdims=True)
    lt2 = jnp.where(eid == i1, NEG, lt)
    m2 = jnp.max(lt2, axis=0, keepdims=True)
    i2 = jnp.min(jnp.where(lt2 == m2, eid, N_EXPERTS), axis=0, keepdims=True)
    ex = jnp.exp(m2 - m1)
    g1 = 1.0 / (1.0 + ex)
    g2 = ex * g1
    tok = lax.broadcasted_iota(i32, (1, TM), 1)
    valid = (s > 0) | (tok < N_META_ROWS)
    sel1 = (eid == i1) & valid
    sel2 = (eid == i2) & valid
    sel = sel1 | sel2
    onehot = jnp.where(sel, 1.0, 0.0)
    tri = jnp.where(lax.broadcasted_iota(i32, (TM, TM), 0) < lax.broadcasted_iota(i32, (TM, TM), 1), 1.0, 0.0)
    rank = jnp.dot(onehot, tri, preferred_element_type=f32)
    gate = jnp.where(sel1, g1, jnp.where(sel2, g2, 0.0))
    rank1 = jnp.sum(jnp.where(sel1, rank, 0.0), axis=0, keepdims=True)
    rank2 = jnp.sum(jnp.where(sel2, rank, 0.0), axis=0, keepdims=True)
    info = jnp.concatenate([i1.astype(f32), i2.astype(f32), rank1, rank2,
                            jnp.zeros((128 - 4, TM), f32)], axis=0)
    info_ref[...] = info.T

    bases = [cnt_ref[e] for e in range(N_EXPERTS)]
    counts = [jnp.sum(onehot[e:e + 1, :]).astype(i32) for e in range(N_EXPERTS)]
    aligned = [b - (b & (ALIGN - 1)) for b in bases]
    rems = [b & (ALIGN - 1) for b in bases]
    ends = [bases[e] + counts[e] for e in range(N_EXPERTS)]
    rem_col = functools.reduce(lambda x, y: x + y,
                               [jnp.where(eid == e, rems[e], 0) for e in range(N_EXPERTS)])
    qpos = rank.astype(i32) + rem_col

    _dispatch_pass(0, qpos, sel, gate, hi, xbuf_ref.at[slot])
    for e in range(N_EXPERTS):
        head = pl.ds(e * CHUNK, ALIGN)
        xbuf_ref[slot, head, :] = (xbuf_ref[slot, head, :].astype(f32) + tail_ref[e].astype(f32)).astype(bf16)

    @pl.when(s > 0)
    def _():
        for e in range(N_EXPERTS):
            _route_copy(xbuf_ref.at[1 - slot], xs_ref, sem_ref.at[1 - slot, e], e, 0).wait()

    for e in range(N_EXPERTS):
        base_ref[s, e] = bases[e]
        _route_copy(xbuf_ref.at[slot], xs_ref, sem_ref.at[slot, e], e, aligned[e]).start()
        cnt_ref[e] = ends[e]
    most = functools.reduce(jnp.maximum, [rems[e] + counts[e] for e in range(N_EXPERTS)])

    for p in range(1, N_PASS):
        @pl.when(most > p * CHUNK)
        def _(p=p):
            _dispatch_pass(p, qpos, sel, gate, hi, xov_ref)
            for e in range(N_EXPERTS):
                _route_copy(xov_ref, xs_ref, sem_ref.at[2, e], e, aligned[e] + p * CHUNK).start()
            for e in range(N_EXPERTS):
                _route_copy(xov_ref, xs_ref, sem_ref.at[2, e], e, 0).wait()

    tail_off = [(ends[e] - (ends[e] & (ALIGN - 1))) - aligned[e] for e in range(N_EXPERTS)]
    feats, gates = _compact(tail_off, ALIGN, qpos, sel, gate, hi)
    for e in range(N_EXPERTS):
        rows = slice(e * ALIGN, (e + 1) * ALIGN)
        new = jnp.concatenate([feats[rows], gates[rows]], axis=1)
        old = jnp.where(tail_off[e] == 0, tail_ref[e].astype(f32), 0.0)
        tail_ref[e] = (new + old).astype(bf16)

    @pl.when(s == N_TILES - 1)
    def _():
        for e in range(N_EXPERTS):
            _route_copy(xbuf_ref.at[slot], xs_ref, sem_ref.at[slot, e], e, 0).wait()
        for e in range(N_EXPERTS):
            fill_from = ends[e] + (ALIGN - 1) - ((ends[e] + (ALIGN - 1)) & (ALIGN - 1))
            _zero_copy(zbuf_ref, xs_ref, sem_ref.at[2, e], e, fill_from).start()
        starts, stops, total = [], [], 0
        for e in range(N_EXPERTS):
            n_t = jnp.maximum(lax.shift_right_logical(ends[e] + (TM - 1), TM.bit_length() - 1), 1)
            rows_ref[e] = n_t * TM
            starts.append(total)
            total = total + n_t
            stops.append(total)

        def tile_entry(t):
            e_t = sum(jnp.where(t >= stops[e], 1, 0) for e in range(N_EXPERTS - 1))
            first = sum(jnp.where(e_t == e, starts[e], 0) for e in range(N_EXPERTS))
            return e_t, e_t * (CAP // TM) + (t - first)

        e_last, blk_last = tile_entry(total - 1)
        for t in range(FFN_GRID):
            e_t, blk_t = tile_entry(t)
            live = t < total
            te_ref[t] = jnp.where(live, e_t, e_last)
            tb_ref[t] = jnp.where(live, blk_t, blk_last)
            tv_ref[t] = jnp.where(live, 1, 0)
        for e in range(N_EXPERTS):
            _zero_copy(zbuf_ref, xs_ref, sem_ref.at[2, e], e, 0).wait()


def _route(h, rsplit):
    return pl.pallas_call(
        _route_kernel,
        grid=(N_TILES,),
        in_specs=[pl.BlockSpec((TM, D_MODEL), lambda s: (s, 0)), _full((D_MODEL, 128))],
        out_specs=[
            pl.BlockSpec((TM, 128), lambda s: (s, 0)),
            _smem(), _smem(), _smem(), _smem(), _smem(),
            pl.BlockSpec(memory_space=pl.ANY),
        ],
        out_shape=[
            jax.ShapeDtypeStruct((NT, 128), f32),
            jax.ShapeDtypeStruct((N_TILES, N_EXPERTS), i32),
            jax.ShapeDtypeStruct((N_EXPERTS,), i32),
            jax.ShapeDtypeStruct((FFN_GRID,), i32),
            jax.ShapeDtypeStruct((FFN_GRID,), i32),
            jax.ShapeDtypeStruct((FFN_GRID,), i32),
            jax.ShapeDtypeStruct((N_EXPERTS * CAP, XS_W), bf16),
        ],
        scratch_shapes=[
            pltpu.VMEM((2, N_EXPERTS * CHUNK, XS_W), bf16),
            pltpu.VMEM((N_EXPERTS * CHUNK, XS_W), bf16),
            pltpu.VMEM((TM, XS_W), bf16),
            pltpu.VMEM((N_EXPERTS, ALIGN, XS_W), bf16),
            pltpu.SMEM((N_EXPERTS,), i32),
            pltpu.SemaphoreType.DMA((3, N_EXPERTS)),
        ],
        compiler_params=_cparams("arbitrary"),
        name="moe_route_dispatch",
    )(h, rsplit)


def _expert_kernel(te_ref, tb_ref, tv_ref, x_ref, wg_ref, wu_ref, wd_ref, y_ref):
    t = pl.program_id(0)

    @pl.when(tv_ref[t] > 0)
    def _():
        x = x_ref[...]
        gate = jnp.sum(x[:, D_MODEL:].astype(f32), axis=1, keepdims=True)
        y = _swiglu_chunks(x[:, :D_MODEL], wg_ref, wu_ref, wd_ref, FF_CHUNKS_MOE, row_scale=gate)
        y_ref[...] = y.astype(bf16)


def _expert_ffn(j, tile_e, tile_blk, tile_valid, xs, wg, wu, wd):
    per_expert = lambda r, c: pl.BlockSpec((None, None, r, c), lambda t, te, tb, tv: (j, te[t], 0, 0))
    grid_spec = pltpu.PrefetchScalarGridSpec(
        num_scalar_prefetch=3,
        grid=(FFN_GRID,),
        in_specs=[
            pl.BlockSpec((TM, XS_W), lambda t, te, tb, tv: (tb[t], 0)),
            per_expert(D_MODEL, D_EXPERT), per_expert(D_MODEL, D_EXPERT), per_expert(D_EXPERT, D_MODEL),
        ],
        out_specs=pl.BlockSpec((TM, D_MODEL), lambda t, te, tb, tv: (tb[t], 0)),
    )
    return pl.pallas_call(
        _expert_kernel,
        grid_spec=grid_spec,
        out_shape=jax.ShapeDtypeStruct((N_EXPERTS * CAP, D_MODEL), bf16),
        compiler_params=_cparams("arbitrary"),
        name="moe_expert_ffn",
    )(tile_e, tile_blk, tile_valid, xs, wg, wu, wd)


def _combine_copy(ys_ref, dst_ref, sem, e, row):
    return pltpu.make_async_copy(
        ys_ref.at[pl.ds(pl.multiple_of(e * CAP + row, ALIGN), CHUNK), :],
        dst_ref.at[pl.ds(e * CHUNK, CHUNK), :],
        sem)


def _window(base_ref, rows_ref, s, e, p):
    base = base_ref[s, e]
    rem = base & (ALIGN - 1)
    want = base - rem + p * CHUNK
    start = jnp.minimum(want, rows_ref[e] - CHUNK)
    return start, want - start, rem


def _onehot_cols(info, wins, p):
    col = lax.broadcasted_iota(i32, (TM, N_EXPERTS * CHUNK), 1)
    pick = lambda ek, vals: functools.reduce(
        lambda a, b: a + b, [jnp.where(ek == e, vals[e], 0) for e in range(N_EXPERTS)])
    hit = None
    for k in range(2):
        ek = info[:, k:k + 1].astype(i32)
        slot = info[:, 2 + k:3 + k].astype(i32) + pick(ek, [w[2] for w in wins]) - p * CHUNK
        ck = jnp.where((slot >= 0) & (slot < CHUNK), ek * CHUNK + slot + pick(ek, [w[1] for w in wins]), -1)
        hk = col == ck
        hit = hk if hit is None else hit | hk
    return jnp.where(hit, 1.0, 0.0).astype(bf16)


def _combine_kernel(base_ref, rows_ref, h_ref, info_ref, ys_ref, g_ref, b_ref, o_ref,
                    ybuf_ref, yov_ref, f_ref, sem_ref):
    i = pl.program_id(0)
    n = pl.num_programs(0)
    s = i + 1
    slot = i % 2

    def start_fetch(tile, to_slot):
        for e in range(N_EXPERTS):
            start = _window(base_ref, rows_ref, tile, e, 0)[0]
            _combine_copy(ys_ref, ybuf_ref.at[to_slot], sem_ref.at[to_slot, e], e, start).start()

    @pl.when(i == 0)
    def _():
        start_fetch(s, slot)

    @pl.when(i + 1 < n)
    def _():
        start_fetch(s + 1, 1 - slot)

    info = info_ref[...]
    qm = _onehot_cols(info, [_window(base_ref, rows_ref, s, e, 0) for e in range(N_EXPERTS)], 0)
    for e in range(N_EXPERTS):
        _combine_copy(ys_ref, ybuf_ref.at[slot], sem_ref.at[slot, e], e, 0).wait()
    f_ref[...] = jnp.dot(qm, ybuf_ref[slot], preferred_element_type=f32)

    deepest = jnp.max(jnp.maximum(info[:, 2:3], info[:, 3:4])).astype(i32) + (ALIGN - 1)
    for p in range(1, N_PASS):
        @pl.when(deepest >= p * CHUNK)
        def _(p=p):
            wins = [_window(base_ref, rows_ref, s, e, p) for e in range(N_EXPERTS)]
            for e in range(N_EXPERTS):
                _combine_copy(ys_ref, yov_ref, sem_ref.at[2, e], e, wins[e][0]).start()
            qp = _onehot_cols(info, wins, p)
            for e in range(N_EXPERTS):
                _combine_copy(ys_ref, yov_ref, sem_ref.at[2, e], e, 0).wait()
            f_ref[...] += jnp.dot(qp, yov_ref[...], preferred_element_type=f32)

    o_ref[...] = _ln(ALPHA * h_ref[...] + f_ref[...], g_ref[...], b_ref[...])


def _combine(base, rows, h, info, ys, g, b):
    n_out = N_TILES - 1
    grid_spec = pltpu.PrefetchScalarGridSpec(
        num_scalar_prefetch=2,
        grid=(n_out,),
        in_specs=[
            pl.BlockSpec((TM, D_MODEL), lambda i, base, rows: (i + 1, 0)),
            pl.BlockSpec((TM, 128), lambda i, base, rows: (i + 1, 0)),
            pl.BlockSpec(memory_space=pl.ANY),
            pl.BlockSpec((1, D_MODEL), lambda i, base, rows: (0, 0)),
            pl.BlockSpec((1, D_MODEL), lambda i, base, rows: (0, 0)),
        ],
        out_specs=pl.BlockSpec((TM, D_MODEL), lambda i, base, rows: (i, 0)),
        scratch_shapes=[
            pltpu.VMEM((2, N_EXPERTS * CHUNK, D_MODEL), bf16),
            pltpu.VMEM((N_EXPERTS * CHUNK, D_MODEL), bf16),
            pltpu.VMEM((TM, D_MODEL), f32),
            pltpu.SemaphoreType.DMA((3, N_EXPERTS)),
        ],
    )
    return pl.pallas_call(
        _combine_kernel,
        grid_spec=grid_spec,
        out_shape=jax.ShapeDtypeStruct((n_out * TM, D_MODEL), f32),
        compiler_params=_cparams("arbitrary"),
        name="moe_combine_ln2",
    )(base, rows, h, info, ys, g, b)


def _split2(x):
    hi = x.astype(bf16)
    lo = (x - hi.astype(f32)).astype(bf16)
    return hi, lo


def kernel(x, meta_tokens, emb_ln_g, emb_ln_b, rel_bias, w_in, conv_dw, conv_b, conv_ln_g, conv_ln_b, sinks, w_attn_proj, w_conv_proj, w_out, ln1_g, ln1_b, ffn_w_gate, ffn_w_up, ffn_w_down, router, moe_w_gate, moe_w_up, moe_w_down, ln2_g, ln2_b):
    row = lambda v: v.reshape(1, -1)
    h = None
    out = None
    bias_real, bias_meta = _bias_tables(rel_bias)
    for i in range(DEPTH):
        if i == 0:
            q, k, v, glu, h = _inproj(i, w_in, x2d=x.reshape(BATCH * SEQ, D_MODEL), meta=meta_tokens,
                                      eg=row(emb_ln_g), eb=row(emb_ln_b))
        else:
            q, k, v, glu = _inproj(i, w_in, h=h)
        a = _attention(bias_real, bias_meta, sinks[i], q, k, v)
        h = _mix(i, h, a, glu, conv_dw[i], row(conv_b[i]), row(conv_ln_g[i]), row(conv_ln_b[i]),
                 w_in, w_attn_proj, w_conv_proj, w_out, row(ln1_g[i]), row(ln1_b[i]))
        j = i // 2
        if i % 2 == 0:
            h = _dense_ffn(j, h, ffn_w_gate, ffn_w_up, ffn_w_down, row(ln2_g[i]), row(ln2_b[i]))
        else:
            r_hi, r_lo = _split2(router[j])
            rsplit = jnp.concatenate(
                [r_hi, r_lo, jnp.zeros((D_MODEL, 128 - 2 * N_EXPERTS), bf16)], axis=1)
            info, base, rows, tile_e, tile_blk, tile_valid, xs = _route(h, rsplit)
            ys = _expert_ffn(j, tile_e, tile_blk, tile_valid, xs, moe_w_gate, moe_w_up, moe_w_down)
            out = _combine(base, rows, h, info, ys, row(ln2_g[i]), row(ln2_b[i]))
    return out.reshape(BATCH, SEQ, D_MODEL)
```

```python
import functools
import math

import numpy as np
import jax
import jax.numpy as jnp
from jax import lax
from jax.experimental import pallas as pl
from jax.experimental.pallas import tpu as pltpu

f32 = jnp.float32
bf16 = jnp.bfloat16
i32 = jnp.int32

D_MODEL = 1024
BATCH = 8
SEQ = 2048
DEPTH = 2
N_META = 16
HEAD_DIM = 64
N_Q_HEADS = 8
N_KV_HEADS = 2
GROUP = N_Q_HEADS // N_KV_HEADS
ATTN_WIDTH = N_Q_HEADS * HEAD_DIM
KV_WIDTH = N_KV_HEADS * HEAD_DIM
WINDOW = 128
BLOCK = 128
CONV_CH = D_MODEL // 2
CONV_WIDTH = 31
N_BUCKETS = 32
MAX_DISTANCE = 128
D_FF = 2816
N_EXPERTS = 8
D_EXPERT = 1408
ALPHA = (2 * DEPTH) ** 0.25
LN_EPS = 1e-5
Q_END = ATTN_WIDTH
K_END = Q_END + KV_WIDTH
V_END = K_END + KV_WIDTH
GLU_END = V_END + 2 * CONV_CH

TM = 512
N_META_ROWS = BATCH * N_META
NT = TM + BATCH * SEQ
N_TILES = NT // TM
TILES_PER_BATCH = SEQ // TM
BLOCKS_PER_TILE = TM // BLOCK
KSEG = 3 * BLOCK
CONV_HALO = 32
NEG = -1e30
LANES = 128
SUBLANES = 8

CHUNK = 192
N_PASS = -(-TM // CHUNK)
ALIGN = 2 * SUBLANES
XS_W = D_MODEL + LANES
CAP = 35 * TM
FFN_GRID = 85
MXU_COLS = 256
FF_CHUNKS_DENSE = (256,) * 11
FF_CHUNKS_MOE = (256,) * 5 + (128,)

VMEM_LIMIT = 56 * 1024 * 1024


def _cparams(sem="arbitrary"):
    return pltpu.CompilerParams(dimension_semantics=(sem,), vmem_limit_bytes=VMEM_LIMIT)


def _ln(x, g, b):
    mu = jnp.mean(x, -1, keepdims=True)
    xc = x - mu
    var = jnp.mean(xc * xc, -1, keepdims=True)
    return xc * lax.rsqrt(var + LN_EPS) * g + b


def _full(shape):
    return pl.BlockSpec(shape, lambda *_: (0,) * len(shape))


def _smem():
    return pl.BlockSpec(memory_space=pltpu.SMEM)


def _embed_tile(s, x_ref, meta_ref, g_ref, b_ref):
    m = _ln(meta_ref[...], g_ref[...], b_ref[...])
    meta_tile = jnp.concatenate([m] * BATCH + [jnp.zeros((TM - N_META_ROWS, D_MODEL), f32)], axis=0)
    return jnp.where(s == 0, meta_tile, _ln(x_ref[...], g_ref[...], b_ref[...]))


def _inproj_kernel(embed, *refs):
    s = pl.program_id(0)
    if embed:
        x_ref, meta_ref, eg_ref, eb_ref, w_ref, q_ref, k_ref, v_ref, glu_ref, h_ref, wb_ref = refs
        h = _embed_tile(s, x_ref, meta_ref, eg_ref, eb_ref)
        h_ref[...] = h
    else:
        hin_ref, w_ref, q_ref, k_ref, v_ref, glu_ref, wb_ref = refs
        h = hin_ref[...]

    @pl.when(s == 0)
    def _():
        wb_ref[...] = w_ref[...].astype(bf16)

    hb = h.astype(bf16)
    col = lambda c0, w: jnp.dot(hb, wb_ref[:, c0:c0 + w], preferred_element_type=f32)
    for c0 in range(0, Q_END, MXU_COLS):
        q_ref[:, c0:c0 + MXU_COLS] = (col(c0, MXU_COLS) * (HEAD_DIM ** -0.5)).astype(bf16)
    kv = col(Q_END, 2 * KV_WIDTH)
    k_ref[...] = kv[:, :KV_WIDTH].astype(bf16)
    v_ref[...] = kv[:, KV_WIDTH:].astype(bf16)
    for c0 in range(0, CONV_CH, MXU_COLS):
        val = col(V_END + c0, MXU_COLS)
        gate = col(V_END + CONV_CH + c0, MXU_COLS)
        glu_ref[:, c0:c0 + MXU_COLS] = (val * jax.nn.sigmoid(gate)).astype(bf16)


def _inproj(layer, w_in, h=None, x2d=None, meta=None, eg=None, eb=None):
    embed = h is None
    row = lambda w: pl.BlockSpec((TM, w), lambda s: (s, 0))
    w_spec = pl.BlockSpec((None, D_MODEL, GLU_END), lambda s: (layer, 0, 0), pipeline_mode=pl.Buffered(1))
    out_specs = [row(ATTN_WIDTH), row(KV_WIDTH), row(KV_WIDTH), row(CONV_CH)]
    out_shape = [
        jax.ShapeDtypeStruct((NT, ATTN_WIDTH), bf16),
        jax.ShapeDtypeStruct((NT, KV_WIDTH), bf16),
        jax.ShapeDtypeStruct((NT, KV_WIDTH), bf16),
        jax.ShapeDtypeStruct((NT, CONV_CH), bf16),
    ]
    if embed:
        in_specs = [
            pl.BlockSpec((TM, D_MODEL), lambda s: (jnp.maximum(s - 1, 0), 0)),
            _full((N_META, D_MODEL)), _full((1, D_MODEL)), _full((1, D_MODEL)), w_spec,
        ]
        args = (x2d, meta, eg, eb, w_in)
        out_specs.append(row(D_MODEL))
        out_shape.append(jax.ShapeDtypeStruct((NT, D_MODEL), f32))
    else:
        in_specs = [row(D_MODEL), w_spec]
        args = (h, w_in)
    return pl.pallas_call(
        functools.partial(_inproj_kernel, embed),
        grid=(N_TILES,),
        in_specs=in_specs,
        out_specs=out_specs,
        out_shape=out_shape,
        scratch_shapes=[pltpu.VMEM((D_MODEL, GLU_END), bf16)],
        compiler_params=_cparams("arbitrary"),
        name="inproj_glu",
    )(*args)


def _rel_bucket_np(dist):
    n = np.maximum(dist, 0)
    max_exact = N_BUCKETS // 2
    nf = np.maximum(n, 1).astype(np.float32)
    scaled = (np.log(nf / np.float32(max_exact)) / np.float32(math.log(MAX_DISTANCE / max_exact))
              * np.float32(N_BUCKETS - max_exact))
    large = np.minimum(max_exact + scaled.astype(np.int32), N_BUCKETS - 1)
    return np.where(n < max_exact, n, large).astype(np.int32)


def _bucket_tables():
    a = np.arange(BLOCK)[:, None]
    m = np.arange(BLOCK)[None, :]
    kk = np.arange(2 * BLOCK)[None, :]
    dist_band = BLOCK + a - kk
    band_ok = (dist_band >= 0) & (dist_band < WINDOW)
    band = np.where(band_ok, _rel_bucket_np(dist_band), -1)
    real = np.full((2, BLOCK, KSEG), -1, np.int32)
    for t, n in enumerate((0, 1)):
        dist_meta = N_META + n * BLOCK + a - m
        real[t, :, :BLOCK] = np.where(m < N_META, _rel_bucket_np(dist_meta), -1)
        real[t, :, BLOCK:] = band
    real[0, :, BLOCK:2 * BLOCK] = -1
    for n in range(2, SEQ // BLOCK):
        dist_meta = N_META + n * BLOCK + a - m
        assert np.array_equal(np.where(m < N_META, _rel_bucket_np(dist_meta), -1), real[1, :, :BLOCK])
    r = np.arange(N_META_ROWS)
    same = (r[:, None] // N_META) == (r[None, :] // N_META)
    dmm = (r[:, None] % N_META) - (r[None, :] % N_META)
    meta = np.where(same & (dmm >= 0), _rel_bucket_np(dmm), -1).astype(np.int32)
    return real, meta


_BK_REAL, _BK_META = _bucket_tables()


def _bias_from_buckets(bk, rb_ref, h):
    acc = jnp.full(bk.shape, NEG, f32)
    for bkt in range(N_BUCKETS):
        acc = jnp.where(bk == bkt, rb_ref[bkt, h], acc)
    return acc


def _expand_kv(x2):
    xf = x2.astype(f32)
    r = pltpu.roll(xf, HEAD_DIM, axis=1)
    low = lax.broadcasted_iota(i32, xf.shape, 1) < HEAD_DIM
    h0 = jnp.where(low, xf, r).astype(bf16)
    h1 = jnp.where(low, r, xf).astype(bf16)
    return (jnp.concatenate([h0, h0], axis=1), jnp.concatenate([h1, h1], axis=1))


def _block_diag(x, n_keys):
    lane_blk = lax.broadcasted_iota(i32, (n_keys, GROUP * HEAD_DIM), 1) // HEAD_DIM
    zero = jnp.zeros_like(x)
    return jnp.concatenate([jnp.where(lane_blk == h, x, zero) for h in range(GROUP)], axis=0)


def _softmax_block(sh, sink):
    mx = jnp.maximum(jnp.max(sh, -1, keepdims=True), sink)
    p = jnp.exp(sh - mx)
    den = jnp.sum(p, -1, keepdims=True) + jnp.exp(sink - mx)
    return p.astype(bf16), 1.0 / den


def _head_scale(rs, m_rows):
    ol = lax.broadcasted_iota(i32, (m_rows, GROUP * HEAD_DIM), 1) // HEAD_DIM
    return jnp.where(ol == 0, rs[0], jnp.where(ol == 1, rs[1], jnp.where(ol == 2, rs[2], rs[3])))


def _attend(qg, kx, vx, bias_of_head, sink_of_head, n_keys):
    kbd = _block_diag(kx, n_keys)
    vbd = _block_diag(vx, n_keys)
    s = lax.dot_general(qg, kbd, (((1,), (1,)), ((), ())), preferred_element_type=f32)
    ps, rs = [], []
    for h in range(GROUP):
        p, r = _softmax_block(s[:, h * n_keys:(h + 1) * n_keys] + bias_of_head(h), sink_of_head(h))
        ps.append(p)
        rs.append(r)
    o = jnp.dot(jnp.concatenate(ps, axis=1), vbd, preferred_element_type=f32)
    return o * _head_scale(rs, qg.shape[0])


def _attend_tile(qg, kbd, vbd, kbd_meta, vbd_meta, bias_of, sink_of_head):
    nt = (((1,), (1,)), ((), ()))
    nb = BLOCKS_PER_TILE
    s_meta = lax.dot_general(qg, kbd_meta, nt, preferred_element_type=f32)
    s_prev, s_cur = [None] * nb, [None] * nb
    for b in range(nb + 1):
        lo, hi = max(b - 1, 0) * BLOCK, min(b + 1, nb) * BLOCK
        sc = lax.dot_general(qg[lo:hi], kbd[b], nt, preferred_element_type=f32)
        if b > 0:
            s_cur[b - 1] = sc[:BLOCK]
        if b < nb:
            s_prev[b] = sc[-BLOCK:]
    pm, pp, pc, scales = [], [], [], []
    for blk in range(nb):
        rows = slice(blk * BLOCK, (blk + 1) * BLOCK)
        pms, pps, pcs, rs = [], [], [], []
        for h in range(GROUP):
            cols = slice(h * BLOCK, (h + 1) * BLOCK)
            sh = jnp.concatenate([s_meta[rows, cols], s_prev[blk][:, cols], s_cur[blk][:, cols]], axis=1)
            p, r = _softmax_block(sh + bias_of(blk, h), sink_of_head(h))
            pms.append(p[:, :BLOCK])
            pps.append(p[:, BLOCK:2 * BLOCK])
            pcs.append(p[:, 2 * BLOCK:])
            rs.append(r)
        pm.append(jnp.concatenate(pms, axis=1))
        pp.append(jnp.concatenate(pps, axis=1))
        pc.append(jnp.concatenate(pcs, axis=1))
        scales.append(_head_scale(rs, BLOCK))
    o = [None] * nb
    for b in range(nb + 1):
        parts = ([pc[b - 1]] if b > 0 else []) + ([pp[b]] if b < nb else [])
        ob = jnp.dot(jnp.concatenate(parts, axis=0), vbd[b], preferred_element_type=f32)
        if b > 0:
            o[b - 1] = o[b - 1] + ob[:BLOCK]
        if b < nb:
            o[b] = ob[-BLOCK:]
    om = jnp.dot(jnp.concatenate(pm, axis=0), vbd_meta, preferred_element_type=f32)
    return (jnp.concatenate(o, axis=0) + om) * jnp.concatenate(scales, axis=0)


def _bias_kernel(rb_ref, bkr_ref, bkm_ref, br_ref, bm_ref):
    for h in range(N_Q_HEADS):
        for t in range(2):
            br_ref[t, h] = _bias_from_buckets(bkr_ref[t], rb_ref, h)
        bm_ref[h] = _bias_from_buckets(bkm_ref[...], rb_ref, h)


def _bias_tables(rel_bias):
    return pl.pallas_call(
        _bias_kernel,
        in_specs=[_smem(), pl.BlockSpec(memory_space=pltpu.VMEM), pl.BlockSpec(memory_space=pltpu.VMEM)],
        out_specs=[pl.BlockSpec(memory_space=pltpu.VMEM), pl.BlockSpec(memory_space=pltpu.VMEM)],
        out_shape=[
            jax.ShapeDtypeStruct((2, N_Q_HEADS, BLOCK, KSEG), f32),
            jax.ShapeDtypeStruct((N_Q_HEADS, N_META_ROWS, N_META_ROWS), f32),
        ],
        name="rel_bias_tables",
    )(rel_bias, jnp.asarray(_BK_REAL), jnp.asarray(_BK_META))


def _attn_kernel(sink_ref, q_ref, k_ref, v_ref, kp_ref, vp_ref, km_ref, vm_ref,
                 br_ref, bm_ref, o_ref):
    s = pl.program_id(0)
    gw = GROUP * HEAD_DIM

    @pl.when(s == 0)
    def _():
        kxs = _expand_kv(k_ref[:N_META_ROWS, :])
        vxs = _expand_kv(v_ref[:N_META_ROWS, :])
        for g in range(N_KV_HEADS):
            o = _attend(q_ref[:N_META_ROWS, g * gw:(g + 1) * gw], kxs[g], vxs[g],
                        lambda h, g=g: bm_ref[g * GROUP + h],
                        lambda h, g=g: sink_ref[g * GROUP + h], N_META_ROWS)
            o_ref[:N_META_ROWS, g * gw:(g + 1) * gw] = o.astype(bf16)
        o_ref[N_META_ROWS:, :] = jnp.zeros((TM - N_META_ROWS, ATTN_WIDTH), bf16)

    @pl.when(s > 0)
    def _():
        first = ((s - 1) % TILES_PER_BATCH) == 0
        tbl0 = jnp.where(first, 0, 1)
        zpad = jnp.zeros((BLOCK - N_META, KV_WIDTH), bf16)
        kms = _expand_kv(jnp.concatenate([km_ref[...], zpad], axis=0))
        vms = _expand_kv(jnp.concatenate([vm_ref[...], zpad], axis=0))
        kxs = [_expand_kv(kp_ref[...])] + [_expand_kv(k_ref[b * BLOCK:(b + 1) * BLOCK, :])
                                          for b in range(BLOCKS_PER_TILE)]
        vxs = [_expand_kv(vp_ref[...])] + [_expand_kv(v_ref[b * BLOCK:(b + 1) * BLOCK, :])
                                          for b in range(BLOCKS_PER_TILE)]
        for g in range(N_KV_HEADS):
            o = _attend_tile(
                q_ref[:, g * gw:(g + 1) * gw],
                [_block_diag(x[g], BLOCK) for x in kxs], [_block_diag(x[g], BLOCK) for x in vxs],
                _block_diag(kms[g], BLOCK), _block_diag(vms[g], BLOCK),
                lambda blk, h, g=g: br_ref[tbl0 if blk == 0 else 1, g * GROUP + h],
                lambda h, g=g: sink_ref[g * GROUP + h])
            o_ref[:, g * gw:(g + 1) * gw] = o.astype(bf16)


def _attention(bias_real, bias_meta, sinks, q, k, v):
    row = lambda w: pl.BlockSpec((TM, w), lambda s: (s, 0))
    prev = pl.BlockSpec((BLOCK, KV_WIDTH), lambda s: (jnp.maximum(s * BLOCKS_PER_TILE - 1, 0), 0))
    meta = pl.BlockSpec((N_META, KV_WIDTH), lambda s: (jnp.maximum(s - 1, 0) // TILES_PER_BATCH, 0))
    return pl.pallas_call(
        _attn_kernel,
        grid=(N_TILES,),
        in_specs=[
            _smem(),
            row(ATTN_WIDTH), row(KV_WIDTH), row(KV_WIDTH),
            prev, prev, meta, meta,
            _full((2, N_Q_HEADS, BLOCK, KSEG)), _full((N_Q_HEADS, N_META_ROWS, N_META_ROWS)),
        ],
        out_specs=row(ATTN_WIDTH),
        out_shape=jax.ShapeDtypeStruct((NT, ATTN_WIDTH), bf16),
        compiler_params=_cparams("arbitrary"),
        name="swa_attention",
    )(sinks, q, k, v, k, v, k, v, bias_real, bias_meta)


SPREAD = TM // BATCH
SH_ROWS = CONV_HALO + TM - SUBLANES
CONV_ROWS = 128
GATE_BLOCKS = 2 * D_MODEL // MXU_COLS


def _conv_window(s, glu_ref, gprev_ref, gmeta_ref, win_ref, sh_ref):
    is_meta = s == 0
    first = ((s - 1) % TILES_PER_BATCH) == 0
    glu = glu_ref[...].astype(f32)
    zgap = jnp.zeros((SPREAD - N_META, CONV_CH), f32)
    spread = jnp.concatenate(
        [p for bb in range(BATCH) for p in (zgap, glu[bb * N_META:(bb + 1) * N_META])], axis=0)
    win_ref[CONV_HALO:, :] = jnp.where(is_meta, spread, glu)
    halo_first = jnp.concatenate(
        [jnp.zeros((CONV_HALO - N_META, CONV_CH), f32), gmeta_ref[...].astype(f32)], axis=0)
    halo = jnp.where(first, halo_first, gprev_ref[...].astype(f32))
    win_ref[:CONV_HALO, :] = jnp.where(is_meta, 0.0, halo)

    for r in range(1, SUBLANES):
        sh_ref[r - 1] = win_ref[r:r + SH_ROWS, :]


def _conv_finish(s, dw_ref, cb_ref, cg_ref, cbeta_ref, win_ref, sh_ref, cv_ref):
    off = CONV_HALO - (CONV_WIDTH - 1)
    for rc in range(TM // CONV_ROWS):
        for lc in range(CONV_CH // LANES):
            ls = slice(lc * LANES, (lc + 1) * LANES)
            acc = None
            for t in range(CONV_WIDTH):
                r = (off + t) % SUBLANES
                r0 = rc * CONV_ROWS + (off + t) - r
                x = win_ref[r0:r0 + CONV_ROWS, ls] if r == 0 else sh_ref[r - 1, r0:r0 + CONV_ROWS, ls]
                term = dw_ref[t:t + 1, ls] * x
                acc = term if acc is None else acc + term
            cv_ref[rc * CONV_ROWS:(rc + 1) * CONV_ROWS, ls] = acc
    y = _ln(cv_ref[...] + cb_ref[...], cg_ref[...], cbeta_ref[...])
    c = y * jax.nn.sigmoid(y)
    gathered = jnp.concatenate(
        [c[bb * SPREAD + SPREAD - N_META:(bb + 1) * SPREAD] for bb in range(BATCH)]
        + [jnp.zeros((TM - N_META_ROWS, CONV_CH), f32)], axis=0)
    return jnp.where(s == 0, gathered, c).astype(bf16)


def _mix_kernel(h_ref, a_ref, glu_ref, gprev_ref, gmeta_ref, dw_ref, cb_ref, cg_ref, cbeta_ref, *rest):
    wg_refs = rest[:GATE_BLOCKS]
    (wap_ref, wcp_ref, wo_ref, g_ref, b_ref, o_ref,
     win_ref, sh_ref, cv_ref, wgb_ref, wapb_ref, wcpb_ref, wob_ref) = rest[GATE_BLOCKS:]
    s = pl.program_id(0)

    @pl.when(s == 0)
    def _():
        for k in range(GATE_BLOCKS):
            wgb_ref[:, k * MXU_COLS:(k + 1) * MXU_COLS] = wg_refs[k][...].astype(bf16)
        wapb_ref[...] = wap_ref[...].astype(bf16)
        wcpb_ref[...] = wcp_ref[...].astype(bf16)
        wob_ref[...] = wo_ref[...].astype(bf16)

    h = h_ref[...]
    gates = jnp.dot(h.astype(bf16), wgb_ref[...], preferred_element_type=f32)
    ya = jnp.dot(a_ref[...], wapb_ref[...], preferred_element_type=f32)
    _conv_window(s, glu_ref, gprev_ref, gmeta_ref, win_ref, sh_ref)
    c = _conv_finish(s, dw_ref, cb_ref, cg_ref, cbeta_ref, win_ref, sh_ref, cv_ref)
    yc = jnp.dot(c, wcpb_ref[...], preferred_element_type=f32)
    m = jax.nn.sigmoid(gates[:, :D_MODEL]) * ya + jax.nn.sigmoid(gates[:, D_MODEL:]) * yc
    mix = jnp.dot(m.astype(bf16), wob_ref[...], preferred_element_type=f32)
    o_ref[...] = _ln(ALPHA * h + mix, g_ref[...], b_ref[...])


def _mix(layer, h, a, glu, dw, cb, cg, cbeta, w_in, wap, wcp, wo, g, b):
    row = lambda w: pl.BlockSpec((TM, w), lambda s: (s, 0))
    gate_block = lambda k: pl.BlockSpec((None, D_MODEL, MXU_COLS),
                                        lambda s: (layer, 0, GLU_END // MXU_COLS + k),
                                        pipeline_mode=pl.Buffered(1))
    gprev = pl.BlockSpec((CONV_HALO, CONV_CH),
                         lambda s: (jnp.maximum(s * (TM // CONV_HALO) - 1, 0), 0))
    gmeta = pl.BlockSpec((N_META, CONV_CH), lambda s: (jnp.maximum(s - 1, 0) // TILES_PER_BATCH, 0))
    per_layer = lambda r, c: pl.BlockSpec((None, r, c), lambda s: (layer, 0, 0), pipeline_mode=pl.Buffered(1))
    return pl.pallas_call(
        _mix_kernel,
        grid=(N_TILES,),
        in_specs=[
            row(D_MODEL), row(ATTN_WIDTH), row(CONV_CH), gprev, gmeta,
            _full((CONV_WIDTH, CONV_CH)), _full((1, CONV_CH)), _full((1, CONV_CH)), _full((1, CONV_CH)),
            *[gate_block(k) for k in range(GATE_BLOCKS)], per_layer(ATTN_WIDTH, D_MODEL),
            per_layer(CONV_CH, D_MODEL), per_layer(D_MODEL, D_MODEL),
            _full((1, D_MODEL)), _full((1, D_MODEL)),
        ],
        out_specs=row(D_MODEL),
        out_shape=jax.ShapeDtypeStruct((NT, D_MODEL), f32),
        scratch_shapes=[
            pltpu.VMEM((CONV_HALO + TM, CONV_CH), f32),
            pltpu.VMEM((7, SH_ROWS, CONV_CH), f32),
            pltpu.VMEM((TM, CONV_CH), f32),
            pltpu.VMEM((D_MODEL, 2 * D_MODEL), bf16),
            pltpu.VMEM((ATTN_WIDTH, D_MODEL), bf16),
            pltpu.VMEM((CONV_CH, D_MODEL), bf16),
            pltpu.VMEM((D_MODEL, D_MODEL), bf16),
        ],
        compiler_params=_cparams("arbitrary"),
        name="conv_mix_ln1",
    )(h, a, glu, glu, glu, dw, cb, cg, cbeta, *([w_in] * GATE_BLOCKS), wap, wcp, wo, g, b)


def _swiglu_chunks(xb, wg_ref, wu_ref, wd_ref, chunks, row_scale=None, narrow_ref=None):
    acc = None
    c0 = 0
    for w in chunks:
        if narrow_ref is not None and 2 * w == MXU_COLS:
            gu = jnp.dot(xb, narrow_ref[...], preferred_element_type=f32)
            gt, up = gu[:, :w], gu[:, w:]
        else:
            gt = jnp.dot(xb, wg_ref[:, c0:c0 + w].astype(bf16), preferred_element_type=f32)
            up = jnp.dot(xb, wu_ref[:, c0:c0 + w].astype(bf16), preferred_element_type=f32)
        act = gt * jax.nn.sigmoid(gt) * up
        if row_scale is not None:
            act = act * row_scale
        part = jnp.dot(act.astype(bf16), wd_ref[c0:c0 + w, :].astype(bf16), preferred_element_type=f32)
        acc = part if acc is None else acc + part
        c0 += w
    return acc


def _ffn_kernel(h_ref, wg_ref, wu_ref, wd_ref, g_ref, b_ref, o_ref):
    h = h_ref[...]
    f = _swiglu_chunks(h.astype(bf16), wg_ref, wu_ref, wd_ref, FF_CHUNKS_DENSE)
    o_ref[...] = _ln(ALPHA * h + f, g_ref[...], b_ref[...])


def _dense_ffn(j, h, wg, wu, wd, g, b):
    row = pl.BlockSpec((TM, D_MODEL), lambda s: (s, 0))
    per_layer = lambda r, c: pl.BlockSpec((None, r, c), lambda s: (j, 0, 0), pipeline_mode=pl.Buffered(1))
    return pl.pallas_call(
        _ffn_kernel,
        grid=(N_TILES,),
        in_specs=[row, per_layer(D_MODEL, D_FF), per_layer(D_MODEL, D_FF), per_layer(D_FF, D_MODEL),
                  _full((1, D_MODEL)), _full((1, D_MODEL))],
        out_specs=row,
        out_shape=jax.ShapeDtypeStruct((NT, D_MODEL), f32),
        compiler_params=_cparams("parallel"),
        name="dense_ffn_ln2",
    )(h, wg, wu, wd, g, b)


def _route_copy(src_ref, xs_ref, sem, e, row):
    return pltpu.make_async_copy(
        src_ref.at[pl.ds(e * CHUNK, CHUNK), :],
        xs_ref.at[pl.ds(pl.multiple_of(e * CAP + row, ALIGN), CHUNK), :],
        sem)


def _zero_copy(zbuf_ref, xs_ref, sem, e, row):
    return pltpu.make_async_copy(
        zbuf_ref,
        xs_ref.at[pl.ds(pl.multiple_of(e * CAP + row, ALIGN), TM), :],
        sem)


def _dispatch_pass(p, ranki, sel, gate, hi, dst_ref):
    slot_id = lax.broadcasted_iota(i32, (CHUNK, TM), 0) + p * CHUNK
    lane = lax.broadcasted_iota(i32, (CHUNK, LANES), 1)
    ps = []
    for e in range(N_EXPERTS):
        pe = (slot_id == ranki[e:e + 1, :]) & sel[e:e + 1, :]
        ps.append(jnp.where(pe, 1.0, 0.0).astype(bf16))
        gs = jnp.sum(jnp.where(pe, gate[e:e + 1, :], 0.0), axis=1, keepdims=True)
        g_hi = gs.astype(bf16).astype(f32)
        g_mid = (gs - g_hi).astype(bf16).astype(f32)
        g_lo = gs - g_hi - g_mid
        parts = jnp.where(lane == 0, g_hi, jnp.where(lane == 1, g_mid, jnp.where(lane == 2, g_lo, 0.0)))
        dst_ref[e * CHUNK:(e + 1) * CHUNK, D_MODEL:] = parts.astype(bf16)
    pall = jnp.concatenate(ps, axis=0)
    dst_ref[:, :D_MODEL] = jnp.dot(pall, hi, preferred_element_type=f32).astype(bf16)


def _route_kernel(h_ref, r_ref, info_ref, base_ref, rows_ref, te_ref, tb_ref, tv_ref, xs_ref,
                  xbuf_ref, xov_ref, zbuf_ref, cnt_ref, sem_ref):
    s = pl.program_id(0)
    slot = s % 2

    @pl.when(s == 0)
    def _():
        for e in range(N_EXPERTS):
            cnt_ref[e] = 0
        zbuf_ref[...] = jnp.zeros((TM, XS_W), bf16)

    h = h_ref[...]
    hi = h.astype(bf16)
    lo = (h - hi.astype(f32)).astype(bf16)
    pa = jnp.dot(hi, r_ref[...], preferred_element_type=f32)
    pb = jnp.dot(lo, r_ref[...], preferred_element_type=f32)
    logits = pa + pltpu.roll(pa, LANES - N_EXPERTS, axis=1) + pb
    lt = logits.T[:N_EXPERTS, :]

    eid = lax.broadcasted_iota(i32, (N_EXPERTS, TM), 0)
    m1 = jnp.max(lt, axis=0, keepdims=True)
    i1 = jnp.min(jnp.where(lt == m1, eid, N_EXPERTS), axis=0, keepdims=True)
    lt2 = jnp.where(eid == i1, NEG, lt)
    m2 = jnp.max(lt2, axis=0, keepdims=True)
    i2 = jnp.min(jnp.where(lt2 == m2, eid, N_EXPERTS), axis=0, keepdims=True)
    ex = jnp.exp(m2 - m1)
    g1 = 1.0 / (1.0 + ex)
    g2 = ex * g1
    tok = lax.broadcasted_iota(i32, (1, TM), 1)
    valid = (s > 0) | (tok < N_META_ROWS)
    sel1 = (eid == i1) & valid
    sel2 = (eid == i2) & valid
    sel = sel1 | sel2
    onehot = jnp.where(sel, 1.0, 0.0)
    tri = jnp.where(lax.broadcasted_iota(i32, (TM, TM), 0) < lax.broadcasted_iota(i32, (TM, TM), 1), 1.0, 0.0)
    rank = jnp.dot(onehot, tri, preferred_element_type=f32)
    gate = jnp.where(sel1, g1, jnp.where(sel2, g2, 0.0))
    rank1 = jnp.sum(jnp.where(sel1, rank, 0.0), axis=0, keepdims=True)
    rank2 = jnp.sum(jnp.where(sel2, rank, 0.0), axis=0, keepdims=True)
    info = jnp.concatenate([i1.astype(f32), i2.astype(f32), rank1, rank2,
                            jnp.zeros((LANES - 4, TM), f32)], axis=0)
    info_ref[...] = info.T
    ranki = rank.astype(i32)

    _dispatch_pass(0, ranki, sel, gate, hi, xbuf_ref.at[slot])

    @pl.when(s > 0)
    def _():
        for e in range(N_EXPERTS):
            _route_copy(xbuf_ref.at[1 - slot], xs_ref, sem_ref.at[1 - slot, e], e, 0).wait()

    bases, counts = [], []
    for e in range(N_EXPERTS):
        base = cnt_ref[e]
        n_e = jnp.sum(onehot[e:e + 1, :]).astype(i32)
        bases.append(base)
        counts.append(n_e)
        base_ref[s, e] = base
        _route_copy(xbuf_ref.at[slot], xs_ref, sem_ref.at[slot, e], e, base).start()
        cnt_ref[e] = base + ((n_e + (ALIGN - 1)) // ALIGN) * ALIGN
    most = functools.reduce(jnp.maximum, counts)

    for p in range(1, N_PASS):
        @pl.when(most > p * CHUNK)
        def _(p=p):
            _dispatch_pass(p, ranki, sel, gate, hi, xov_ref)
            for e in range(N_EXPERTS):
                _route_copy(xov_ref, xs_ref, sem_ref.at[2, e], e, bases[e] + p * CHUNK).start()
            for e in range(N_EXPERTS):
                _route_copy(xov_ref, xs_ref, sem_ref.at[2, e], e, 0).wait()

    @pl.when(s == N_TILES - 1)
    def _():
        for e in range(N_EXPERTS):
            _route_copy(xbuf_ref.at[slot], xs_ref, sem_ref.at[slot, e], e, 0).wait()
        used = [cnt_ref[e] for e in range(N_EXPERTS)]
        for e in range(N_EXPERTS):
            _zero_copy(zbuf_ref, xs_ref, sem_ref.at[2, e], e, used[e]).start()
        starts, ends, total = [], [], 0
        for e in range(N_EXPERTS):
            n_t = jnp.maximum(lax.shift_right_logical(used[e] + (TM - 1), TM.bit_length() - 1), 1)
            rows_ref[e] = n_t * TM
            starts.append(total)
            total = total + n_t
            ends.append(total)

        def tile_entry(t):
            e_t = sum(jnp.where(t >= ends[e], 1, 0) for e in range(N_EXPERTS - 1))
            first = sum(jnp.where(e_t == e, starts[e], 0) for e in range(N_EXPERTS))
            return e_t, e_t * (CAP // TM) + (t - first)

        e_last, blk_last = tile_entry(total - 1)
        for t in range(FFN_GRID):
            e_t, blk_t = tile_entry(t)
            live = t < total
            te_ref[t] = jnp.where(live, e_t, e_last)
            tb_ref[t] = jnp.where(live, blk_t, blk_last)
            tv_ref[t] = jnp.where(live, 1, 0)
        for e in range(N_EXPERTS):
            _zero_copy(zbuf_ref, xs_ref, sem_ref.at[2, e], e, 0).wait()


def _route(h, rsplit):
    return pl.pallas_call(
        _route_kernel,
        grid=(N_TILES,),
        in_specs=[pl.BlockSpec((TM, D_MODEL), lambda s: (s, 0)), _full((D_MODEL, LANES))],
        out_specs=[
            pl.BlockSpec((TM, LANES), lambda s: (s, 0)),
            _smem(), _smem(), _smem(), _smem(), _smem(),
            pl.BlockSpec(memory_space=pl.ANY),
        ],
        out_shape=[
            jax.ShapeDtypeStruct((NT, LANES), f32),
            jax.ShapeDtypeStruct((N_TILES, N_EXPERTS), i32),
            jax.ShapeDtypeStruct((N_EXPERTS,), i32),
            jax.ShapeDtypeStruct((FFN_GRID,), i32),
            jax.ShapeDtypeStruct((FFN_GRID,), i32),
            jax.ShapeDtypeStruct((FFN_GRID,), i32),
            jax.ShapeDtypeStruct((N_EXPERTS * CAP, XS_W), bf16),
        ],
        scratch_shapes=[
            pltpu.VMEM((2, N_EXPERTS * CHUNK, XS_W), bf16),
            pltpu.VMEM((N_EXPERTS * CHUNK, XS_W), bf16),
            pltpu.VMEM((TM, XS_W), bf16),
            pltpu.SMEM((N_EXPERTS,), i32),
            pltpu.SemaphoreType.DMA((3, N_EXPERTS)),
        ],
        compiler_params=_cparams("arbitrary"),
        name="moe_route_dispatch",
    )(h, rsplit)


def _expert_kernel(te_ref, tb_ref, tv_ref, x_ref, wg_ref, wu_ref, wd_ref, y_ref, narrow_ref):
    t = pl.program_id(0)
    new_expert = (t == 0) | (te_ref[t] != te_ref[jnp.maximum(t - 1, 0)])

    @pl.when(new_expert)
    def _():
        c0 = D_EXPERT - FF_CHUNKS_MOE[-1]
        narrow_ref[...] = jnp.concatenate([wg_ref[:, c0:], wu_ref[:, c0:]], axis=1).astype(bf16)

    @pl.when(tv_ref[t] > 0)
    def _():
        x = x_ref[...]
        gate = jnp.sum(x[:, D_MODEL:].astype(f32), axis=1, keepdims=True)
        y = _swiglu_chunks(x[:, :D_MODEL], wg_ref, wu_ref, wd_ref, FF_CHUNKS_MOE, row_scale=gate,
                           narrow_ref=narrow_ref)
        y_ref[...] = y.astype(bf16)


def _expert_ffn(j, tile_e, tile_blk, tile_valid, xs, wg, wu, wd):
    per_expert = lambda r, c: pl.BlockSpec((None, None, r, c), lambda t, te, tb, tv: (j, te[t], 0, 0))
    grid_spec = pltpu.PrefetchScalarGridSpec(
        num_scalar_prefetch=3,
        grid=(FFN_GRID,),
        in_specs=[
            pl.BlockSpec((TM, XS_W), lambda t, te, tb, tv: (tb[t], 0)),
            per_expert(D_MODEL, D_EXPERT), per_expert(D_MODEL, D_EXPERT), per_expert(D_EXPERT, D_MODEL),
        ],
        out_specs=pl.BlockSpec((TM, D_MODEL), lambda t, te, tb, tv: (tb[t], 0)),
        scratch_shapes=[pltpu.VMEM((D_MODEL, MXU_COLS), bf16)],
    )
    return pl.pallas_call(
        _expert_kernel,
        grid_spec=grid_spec,
        out_shape=jax.ShapeDtypeStruct((N_EXPERTS * CAP, D_MODEL), bf16),
        compiler_params=_cparams("arbitrary"),
        name="moe_expert_ffn",
    )(tile_e, tile_blk, tile_valid, xs, wg, wu, wd)


def _combine_copy(ys_ref, dst_ref, sem, e, row):
    return pltpu.make_async_copy(
        ys_ref.at[pl.ds(pl.multiple_of(e * CAP + row, ALIGN), CHUNK), :],
        dst_ref.at[pl.ds(e * CHUNK, CHUNK), :],
        sem)


def _window(base_ref, rows_ref, s, e, p):
    want = base_ref[s, e] + p * CHUNK
    start = jnp.minimum(want, rows_ref[e] - CHUNK)
    return start, want - start


def _onehot_cols(info, deltas, p):
    col = lax.broadcasted_iota(i32, (TM, N_EXPERTS * CHUNK), 1)
    hit = None
    for k in range(2):
        ek = info[:, k:k + 1].astype(i32)
        rk = info[:, 2 + k:3 + k].astype(i32) - p * CHUNK
        dk = functools.reduce(lambda a, b: a + b,
                              [jnp.where(ek == e, deltas[e], 0) for e in range(N_EXPERTS)])
        ck = jnp.where((rk >= 0) & (rk < CHUNK), ek * CHUNK + rk + dk, -1)
        hk = col == ck
        hit = hk if hit is None else hit | hk
    return jnp.where(hit, 1.0, 0.0).astype(bf16)


def _combine_kernel(base_ref, rows_ref, h_ref, info_ref, ys_ref, g_ref, b_ref, o_ref,
                    ybuf_ref, yov_ref, f_ref, sem_ref):
    i = pl.program_id(0)
    n = pl.num_programs(0)
    s = i + 1
    slot = i % 2

    def start_fetch(tile, to_slot):
        for e in range(N_EXPERTS):
            start, _ = _window(base_ref, rows_ref, tile, e, 0)
            _combine_copy(ys_ref, ybuf_ref.at[to_slot], sem_ref.at[to_slot, e], e, start).start()

    @pl.when(i == 0)
    def _():
        start_fetch(s, slot)

    @pl.when(i + 1 < n)
    def _():
        start_fetch(s + 1, 1 - slot)

    info = info_ref[...]
    deltas = [_window(base_ref, rows_ref, s, e, 0)[1] for e in range(N_EXPERTS)]
    qm = _onehot_cols(info, deltas, 0)
    for e in range(N_EXPERTS):
        _combine_copy(ys_ref, ybuf_ref.at[slot], sem_ref.at[slot, e], e, 0).wait()
    f_ref[...] = jnp.dot(qm, ybuf_ref[slot], preferred_element_type=f32)

    deepest = jnp.max(jnp.maximum(info[:, 2:3], info[:, 3:4])).astype(i32)
    for p in range(1, N_PASS):
        @pl.when(deepest >= p * CHUNK)
        def _(p=p):
            wins = [_window(base_ref, rows_ref, s, e, p) for e in range(N_EXPERTS)]
            for e in range(N_EXPERTS):
                _combine_copy(ys_ref, yov_ref, sem_ref.at[2, e], e, wins[e][0]).start()
            qp = _onehot_cols(info, [w[1] for w in wins], p)
            for e in range(N_EXPERTS):
                _combine_copy(ys_ref, yov_ref, sem_ref.at[2, e], e, 0).wait()
            f_ref[...] += jnp.dot(qp, yov_ref[...], preferred_element_type=f32)

    o_ref[...] = _ln(ALPHA * h_ref[...] + f_ref[...], g_ref[...], b_ref[...])


def _combine(base, rows, h, info, ys, g, b):
    n_out = N_TILES - 1
    grid_spec = pltpu.PrefetchScalarGridSpec(
        num_scalar_prefetch=2,
        grid=(n_out,),
        in_specs=[
            pl.BlockSpec((TM, D_MODEL), lambda i, base, rows: (i + 1, 0)),
            pl.BlockSpec((TM, LANES), lambda i, base, rows: (i + 1, 0)),
            pl.BlockSpec(memory_space=pl.ANY),
            pl.BlockSpec((1, D_MODEL), lambda i, base, rows: (0, 0)),
            pl.BlockSpec((1, D_MODEL), lambda i, base, rows: (0, 0)),
        ],
        out_specs=pl.BlockSpec((TM, D_MODEL), lambda i, base, rows: (i, 0)),
        scratch_shapes=[
            pltpu.VMEM((2, N_EXPERTS * CHUNK, D_MODEL), bf16),
            pltpu.VMEM((N_EXPERTS * CHUNK, D_MODEL), bf16),
            pltpu.VMEM((TM, D_MODEL), f32),
            pltpu.SemaphoreType.DMA((3, N_EXPERTS)),
        ],
    )
    return pl.pallas_call(
        _combine_kernel,
        grid_spec=grid_spec,
        out_shape=jax.ShapeDtypeStruct((n_out * TM, D_MODEL), f32),
        compiler_params=_cparams("arbitrary"),
        name="moe_combine_ln2",
    )(base, rows, h, info, ys, g, b)


def _split2(x):
    hi = x.astype(bf16)
    lo = (x - hi.astype(f32)).astype(bf16)
    return hi, lo


def kernel(x, meta_tokens, emb_ln_g, emb_ln_b, rel_bias, w_in, conv_dw, conv_b, conv_ln_g, conv_ln_b, sinks, w_attn_proj, w_conv_proj, w_out, ln1_g, ln1_b, ffn_w_gate, ffn_w_up, ffn_w_down, router, moe_w_gate, moe_w_up, moe_w_down, ln2_g, ln2_b):
    row = lambda v: v.reshape(1, -1)
    h = None
    out = None
    bias_real, bias_meta = _bias_tables(rel_bias)
    for i in range(DEPTH):
        if i == 0:
            q, k, v, glu, h = _inproj(i, w_in, x2d=x.reshape(BATCH * SEQ, D_MODEL), meta=meta_tokens,
                                      eg=row(emb_ln_g), eb=row(emb_ln_b))
        else:
            q, k, v, glu = _inproj(i, w_in, h=h)
        a = _attention(bias_real, bias_meta, sinks[i], q, k, v)
        h = _mix(i, h, a, glu, conv_dw[i], row(conv_b[i]), row(conv_ln_g[i]), row(conv_ln_b[i]),
                 w_in, w_attn_proj, w_conv_proj, w_out, row(ln1_g[i]), row(ln1_b[i]))
        j = i // 2
        if i % 2 == 0:
            h = _dense_ffn(j, h, ffn_w_gate, ffn_w_up, ffn_w_down, row(ln2_g[i]), row(ln2_b[i]))
        else:
            r_hi, r_lo = _split2(router[j])
            rsplit = jnp.concatenate(
                [r_hi, r_lo, jnp.zeros((D_MODEL, LANES - 2 * N_EXPERTS), bf16)], axis=1)
            info, base, rows, tile_e, tile_blk, tile_valid, xs = _route(h, rsplit)
            ys = _expert_ffn(j, tile_e, tile_blk, tile_valid, xs, moe_w_gate, moe_w_up, moe_w_down)
            out = _combine(base, rows, h, info, ys, row(ln2_g[i]), row(ln2_b[i]))
    return out.reshape(BATCH, SEQ, D_MODEL)
```

```python
import functools
import math

import numpy as np
import jax
import jax.numpy as jnp
from jax import lax
from jax.experimental import pallas as pl
from jax.experimental.pallas import tpu as pltpu

f32 = jnp.float32
bf16 = jnp.bfloat16
i32 = jnp.int32

D_MODEL = 1024
BATCH = 8
SEQ = 2048
DEPTH = 2
N_META = 16
HEAD_DIM = 64
N_Q_HEADS = 8
N_KV_HEADS = 2
GROUP = N_Q_HEADS // N_KV_HEADS
ATTN_WIDTH = N_Q_HEADS * HEAD_DIM
KV_WIDTH = N_KV_HEADS * HEAD_DIM
WINDOW = 128
BLOCK = 128
CONV_CH = D_MODEL // 2
CONV_WIDTH = 31
N_BUCKETS = 32
MAX_DISTANCE = 128
D_FF = 2816
N_EXPERTS = 8
D_EXPERT = 1408
ALPHA = (2 * DEPTH) ** 0.25
LN_EPS = 1e-5
Q_END = ATTN_WIDTH
K_END = Q_END + KV_WIDTH
V_END = K_END + KV_WIDTH
GLU_END = V_END + 2 * CONV_CH

TM = 512
N_META_ROWS = BATCH * N_META
NT = TM + BATCH * SEQ
N_TILES = NT // TM
TILES_PER_BATCH = SEQ // TM
BLOCKS_PER_TILE = TM // BLOCK
KSEG = 3 * BLOCK
CONV_HALO = 32
NEG = -1e30
LANES = 128
SUBLANES = 8

CHUNK = 192
N_PASS = -(-TM // CHUNK)
ALIGN = 2 * SUBLANES
XS_W = D_MODEL + LANES
CAP = 35 * TM
FFN_GRID = 85
MXU_COLS = 256
FF_CHUNKS_DENSE = (256,) * 11
FF_CHUNKS_MOE = (256,) * 5 + (128,)

VMEM_LIMIT = 56 * 1024 * 1024


def _cparams(sem="arbitrary"):
    return pltpu.CompilerParams(dimension_semantics=(sem,), vmem_limit_bytes=VMEM_LIMIT)


def _ln(x, g, b):
    mu = jnp.mean(x, -1, keepdims=True)
    xc = x - mu
    var = jnp.mean(xc * xc, -1, keepdims=True)
    return xc * lax.rsqrt(var + LN_EPS) * g + b


def _full(shape):
    return pl.BlockSpec(shape, lambda *_: (0,) * len(shape))


def _smem():
    return pl.BlockSpec(memory_space=pltpu.SMEM)


def _embed_tile(s, x_ref, meta_ref, g_ref, b_ref):
    m = _ln(meta_ref[...], g_ref[...], b_ref[...])
    meta_tile = jnp.concatenate([m] * BATCH + [jnp.zeros((TM - N_META_ROWS, D_MODEL), f32)], axis=0)
    return jnp.where(s == 0, meta_tile, _ln(x_ref[...], g_ref[...], b_ref[...]))


def _inproj_kernel(embed, *refs):
    s = pl.program_id(0)
    if embed:
        x_ref, meta_ref, eg_ref, eb_ref, w_ref, q_ref, k_ref, v_ref, glu_ref, h_ref, wb_ref = refs
        h = _embed_tile(s, x_ref, meta_ref, eg_ref, eb_ref)
        h_ref[...] = h
    else:
        hin_ref, w_ref, q_ref, k_ref, v_ref, glu_ref, wb_ref = refs
        h = hin_ref[...]

    @pl.when(s == 0)
    def _():
        wb_ref[...] = w_ref[...].astype(bf16)

    hb = h.astype(bf16)
    col = lambda c0, w: jnp.dot(hb, wb_ref[:, c0:c0 + w], preferred_element_type=f32)
    for c0 in range(0, Q_END, MXU_COLS):
        q_ref[:, c0:c0 + MXU_COLS] = (col(c0, MXU_COLS) * (HEAD_DIM ** -0.5)).astype(bf16)
    kv = col(Q_END, 2 * KV_WIDTH)
    k_ref[...] = kv[:, :KV_WIDTH].astype(bf16)
    v_ref[...] = kv[:, KV_WIDTH:].astype(bf16)
    for c0 in range(0, CONV_CH, MXU_COLS):
        val = col(V_END + c0, MXU_COLS)
        gate = col(V_END + CONV_CH + c0, MXU_COLS)
        glu_ref[:, c0:c0 + MXU_COLS] = (val * jax.nn.sigmoid(gate)).astype(bf16)


def _inproj(layer, w_in, h=None, x2d=None, meta=None, eg=None, eb=None):
    embed = h is None
    row = lambda w: pl.BlockSpec((TM, w), lambda s: (s, 0))
    w_spec = pl.BlockSpec((None, D_MODEL, GLU_END), lambda s: (layer, 0, 0), pipeline_mode=pl.Buffered(1))
    out_specs = [row(ATTN_WIDTH), row(KV_WIDTH), row(KV_WIDTH), row(CONV_CH)]
    out_shape = [
        jax.ShapeDtypeStruct((NT, ATTN_WIDTH), bf16),
        jax.ShapeDtypeStruct((NT, KV_WIDTH), bf16),
        jax.ShapeDtypeStruct((NT, KV_WIDTH), bf16),
        jax.ShapeDtypeStruct((NT, CONV_CH), bf16),
    ]
    if embed:
        in_specs = [
            pl.BlockSpec((TM, D_MODEL), lambda s: (jnp.maximum(s - 1, 0), 0)),
            _full((N_META, D_MODEL)), _full((1, D_MODEL)), _full((1, D_MODEL)), w_spec,
        ]
        args = (x2d, meta, eg, eb, w_in)
        out_specs.append(row(D_MODEL))
        out_shape.append(jax.ShapeDtypeStruct((NT, D_MODEL), f32))
    else:
        in_specs = [row(D_MODEL), w_spec]
        args = (h, w_in)
    return pl.pallas_call(
        functools.partial(_inproj_kernel, embed),
        grid=(N_TILES,),
        in_specs=in_specs,
        out_specs=out_specs,
        out_shape=out_shape,
        scratch_shapes=[pltpu.VMEM((D_MODEL, GLU_END), bf16)],
        compiler_params=_cparams("arbitrary"),
        name="inproj_glu",
    )(*args)


def _rel_bucket_np(dist):
    n = np.maximum(dist, 0)
    max_exact = N_BUCKETS // 2
    nf = np.maximum(n, 1).astype(np.float32)
    scaled = (np.log(nf / np.float32(max_exact)) / np.float32(math.log(MAX_DISTANCE / max_exact))
              * np.float32(N_BUCKETS - max_exact))
    large = np.minimum(max_exact + scaled.astype(np.int32), N_BUCKETS - 1)
    return np.where(n < max_exact, n, large).astype(np.int32)


def _bucket_tables():
    a = np.arange(BLOCK)[:, None]
    m = np.arange(BLOCK)[None, :]
    kk = np.arange(2 * BLOCK)[None, :]
    dist_band = BLOCK + a - kk
    band_ok = (dist_band >= 0) & (dist_band < WINDOW)
    band = np.where(band_ok, _rel_bucket_np(dist_band), -1)
    real = np.full((2, BLOCK, KSEG), -1, np.int32)
    for t, n in enumerate((0, 1)):
        dist_meta = N_META + n * BLOCK + a - m
        real[t, :, :BLOCK] = np.where(m < N_META, _rel_bucket_np(dist_meta), -1)
        real[t, :, BLOCK:] = band
    real[0, :, BLOCK:2 * BLOCK] = -1
    for n in range(2, SEQ // BLOCK):
        dist_meta = N_META + n * BLOCK + a - m
        assert np.array_equal(np.where(m < N_META, _rel_bucket_np(dist_meta), -1), real[1, :, :BLOCK])
    r = np.arange(N_META_ROWS)
    same = (r[:, None] // N_META) == (r[None, :] // N_META)
    dmm = (r[:, None] % N_META) - (r[None, :] % N_META)
    meta = np.where(same & (dmm >= 0), _rel_bucket_np(dmm), -1).astype(np.int32)
    return real, meta


_BK_REAL, _BK_META = _bucket_tables()


def _bias_from_buckets(bk, rb_ref, h):
    acc = jnp.full(bk.shape, NEG, f32)
    for bkt in range(N_BUCKETS):
        acc = jnp.where(bk == bkt, rb_ref[bkt, h], acc)
    return acc


def _expand_kv(x2):
    xf = x2.astype(f32)
    r = pltpu.roll(xf, HEAD_DIM, axis=1)
    low = lax.broadcasted_iota(i32, xf.shape, 1) < HEAD_DIM
    h0 = jnp.where(low, xf, r).astype(bf16)
    h1 = jnp.where(low, r, xf).astype(bf16)
    return (jnp.concatenate([h0, h0], axis=1), jnp.concatenate([h1, h1], axis=1))


def _block_diag(x, n_keys):
    lane_blk = lax.broadcasted_iota(i32, (n_keys, GROUP * HEAD_DIM), 1) // HEAD_DIM
    zero = jnp.zeros_like(x)
    return jnp.concatenate([jnp.where(lane_blk == h, x, zero) for h in range(GROUP)], axis=0)


def _softmax_block(sh, sink):
    mx = jnp.maximum(jnp.max(sh, -1, keepdims=True), sink)
    p = jnp.exp(sh - mx)
    den = jnp.sum(p, -1, keepdims=True) + jnp.exp(sink - mx)
    return p.astype(bf16), 1.0 / den


def _head_scale(rs, m_rows):
    ol = lax.broadcasted_iota(i32, (m_rows, GROUP * HEAD_DIM), 1) // HEAD_DIM
    return jnp.where(ol == 0, rs[0], jnp.where(ol == 1, rs[1], jnp.where(ol == 2, rs[2], rs[3])))


def _attend(qg, kx, vx, bias_of_head, sink_of_head, n_keys):
    kbd = _block_diag(kx, n_keys)
    vbd = _block_diag(vx, n_keys)
    s = lax.dot_general(qg, kbd, (((1,), (1,)), ((), ())), preferred_element_type=f32)
    ps, rs = [], []
    for h in range(GROUP):
        p, r = _softmax_block(s[:, h * n_keys:(h + 1) * n_keys] + bias_of_head(h), sink_of_head(h))
        ps.append(p)
        rs.append(r)
    o = jnp.dot(jnp.concatenate(ps, axis=1), vbd, preferred_element_type=f32)
    return o * _head_scale(rs, qg.shape[0])


def _attend_tile(qg, kbd, vbd, kbd_meta, vbd_meta, bias_of, sink_of_head):
    nt = (((1,), (1,)), ((), ()))
    nb = BLOCKS_PER_TILE
    s_meta = lax.dot_general(qg, kbd_meta, nt, preferred_element_type=f32)
    s_prev, s_cur = [None] * nb, [None] * nb
    for b in range(nb + 1):
        lo, hi = max(b - 1, 0) * BLOCK, min(b + 1, nb) * BLOCK
        sc = lax.dot_general(qg[lo:hi], kbd[b], nt, preferred_element_type=f32)
        if b > 0:
            s_cur[b - 1] = sc[:BLOCK]
        if b < nb:
            s_prev[b] = sc[-BLOCK:]
    pm, pp, pc, scales = [], [], [], []
    for blk in range(nb):
        rows = slice(blk * BLOCK, (blk + 1) * BLOCK)
        pms, pps, pcs, rs = [], [], [], []
        for h in range(GROUP):
            cols = slice(h * BLOCK, (h + 1) * BLOCK)
            sh = jnp.concatenate([s_meta[rows, cols], s_prev[blk][:, cols], s_cur[blk][:, cols]], axis=1)
            p, r = _softmax_block(sh + bias_of(blk, h), sink_of_head(h))
            pms.append(p[:, :BLOCK])
            pps.append(p[:, BLOCK:2 * BLOCK])
            pcs.append(p[:, 2 * BLOCK:])
            rs.append(r)
        pm.append(jnp.concatenate(pms, axis=1))
        pp.append(jnp.concatenate(pps, axis=1))
        pc.append(jnp.concatenate(pcs, axis=1))
        scales.append(_head_scale(rs, BLOCK))
    o = [None] * nb
    for b in range(nb + 1):
        parts = ([pc[b - 1]] if b > 0 else []) + ([pp[b]] if b < nb else [])
        ob = jnp.dot(jnp.concatenate(parts, axis=0), vbd[b], preferred_element_type=f32)
        if b > 0:
            o[b - 1] = o[b - 1] + ob[:BLOCK]
        if b < nb:
            o[b] = ob[-BLOCK:]
    om = jnp.dot(jnp.concatenate(pm, axis=0), vbd_meta, preferred_element_type=f32)
    return (jnp.concatenate(o, axis=0) + om) * jnp.concatenate(scales, axis=0)


def _bias_kernel(rb_ref, bkr_ref, bkm_ref, br_ref, bm_ref):
    for h in range(N_Q_HEADS):
        for t in range(2):
            br_ref[t, h] = _bias_from_buckets(bkr_ref[t], rb_ref, h)
        bm_ref[h] = _bias_from_buckets(bkm_ref[...], rb_ref, h)


def _bias_tables(rel_bias):
    return pl.pallas_call(
        _bias_kernel,
        in_specs=[_smem(), pl.BlockSpec(memory_space=pltpu.VMEM), pl.BlockSpec(memory_space=pltpu.VMEM)],
        out_specs=[pl.BlockSpec(memory_space=pltpu.VMEM), pl.BlockSpec(memory_space=pltpu.VMEM)],
        out_shape=[
            jax.ShapeDtypeStruct((2, N_Q_HEADS, BLOCK, KSEG), f32),
            jax.ShapeDtypeStruct((N_Q_HEADS, N_META_ROWS, N_META_ROWS), f32),
        ],
        name="rel_bias_tables",
    )(rel_bias, jnp.asarray(_BK_REAL), jnp.asarray(_BK_META))


def _attn_kernel(sink_ref, q_ref, k_ref, v_ref, kp_ref, vp_ref, km_ref, vm_ref,
                 br_ref, bm_ref, o_ref):
    s = pl.program_id(0)
    gw = GROUP * HEAD_DIM

    @pl.when(s == 0)
    def _():
        kxs = _expand_kv(k_ref[:N_META_ROWS, :])
        vxs = _expand_kv(v_ref[:N_META_ROWS, :])
        for g in range(N_KV_HEADS):
            o = _attend(q_ref[:N_META_ROWS, g * gw:(g + 1) * gw], kxs[g], vxs[g],
                        lambda h, g=g: bm_ref[g * GROUP + h],
                        lambda h, g=g: sink_ref[g * GROUP + h], N_META_ROWS)
            o_ref[:N_META_ROWS, g * gw:(g + 1) * gw] = o.astype(bf16)
        o_ref[N_META_ROWS:, :] = jnp.zeros((TM - N_META_ROWS, ATTN_WIDTH), bf16)

    @pl.when(s > 0)
    def _():
        first = ((s - 1) % TILES_PER_BATCH) == 0
        tbl0 = jnp.where(first, 0, 1)
        zpad = jnp.zeros((BLOCK - N_META, KV_WIDTH), bf16)
        kms = _expand_kv(jnp.concatenate([km_ref[...], zpad], axis=0))
        vms = _expand_kv(jnp.concatenate([vm_ref[...], zpad], axis=0))
        kxs = [_expand_kv(kp_ref[...])] + [_expand_kv(k_ref[b * BLOCK:(b + 1) * BLOCK, :])
                                          for b in range(BLOCKS_PER_TILE)]
        vxs = [_expand_kv(vp_ref[...])] + [_expand_kv(v_ref[b * BLOCK:(b + 1) * BLOCK, :])
                                          for b in range(BLOCKS_PER_TILE)]
        for g in range(N_KV_HEADS):
            o = _attend_tile(
                q_ref[:, g * gw:(g + 1) * gw],
                [_block_diag(x[g], BLOCK) for x in kxs], [_block_diag(x[g], BLOCK) for x in vxs],
                _block_diag(kms[g], BLOCK), _block_diag(vms[g], BLOCK),
                lambda blk, h, g=g: br_ref[tbl0 if blk == 0 else 1, g * GROUP + h],
                lambda h, g=g: sink_ref[g * GROUP + h])
            o_ref[:, g * gw:(g + 1) * gw] = o.astype(bf16)


def _attention(bias_real, bias_meta, sinks, q, k, v):
    row = lambda w: pl.BlockSpec((TM, w), lambda s: (s, 0))
    prev = pl.BlockSpec((BLOCK, KV_WIDTH), lambda s: (jnp.maximum(s * BLOCKS_PER_TILE - 1, 0), 0))
    meta = pl.BlockSpec((N_META, KV_WIDTH), lambda s: (jnp.maximum(s - 1, 0) // TILES_PER_BATCH, 0))
    return pl.pallas_call(
        _attn_kernel,
        grid=(N_TILES,),
        in_specs=[
            _smem(),
            row(ATTN_WIDTH), row(KV_WIDTH), row(KV_WIDTH),
            prev, prev, meta, meta,
            _full((2, N_Q_HEADS, BLOCK, KSEG)), _full((N_Q_HEADS, N_META_ROWS, N_META_ROWS)),
        ],
        out_specs=row(ATTN_WIDTH),
        out_shape=jax.ShapeDtypeStruct((NT, ATTN_WIDTH), bf16),
        compiler_params=_cparams("arbitrary"),
        name="swa_attention",
    )(sinks, q, k, v, k, v, k, v, bias_real, bias_meta)


SPREAD = TM // BATCH
SH_ROWS = CONV_HALO + TM - SUBLANES
CONV_ROWS = 128
GATE_BLOCKS = 2 * D_MODEL // MXU_COLS


def _conv_window(s, glu_ref, gprev_ref, gmeta_ref, win_ref, sh_ref):
    is_meta = s == 0
    first = ((s - 1) % TILES_PER_BATCH) == 0
    glu = glu_ref[...].astype(f32)
    zgap = jnp.zeros((SPREAD - N_META, CONV_CH), f32)
    spread = jnp.concatenate(
        [p for bb in range(BATCH) for p in (zgap, glu[bb * N_META:(bb + 1) * N_META])], axis=0)
    win_ref[CONV_HALO:, :] = jnp.where(is_meta, spread, glu)
    halo_first = jnp.concatenate(
        [jnp.zeros((CONV_HALO - N_META, CONV_CH), f32), gmeta_ref[...].astype(f32)], axis=0)
    halo = jnp.where(first, halo_first, gprev_ref[...].astype(f32))
    win_ref[:CONV_HALO, :] = jnp.where(is_meta, 0.0, halo)

    for r in range(1, SUBLANES):
        sh_ref[r - 1] = win_ref[r:r + SH_ROWS, :]


def _conv_finish(s, dw_ref, cb_ref, cg_ref, cbeta_ref, win_ref, sh_ref, cv_ref):
    off = CONV_HALO - (CONV_WIDTH - 1)
    for rc in range(TM // CONV_ROWS):
        for lc in range(CONV_CH // LANES):
            ls = slice(lc * LANES, (lc + 1) * LANES)
            acc = None
            for t in range(CONV_WIDTH):
                r = (off + t) % SUBLANES
                r0 = rc * CONV_ROWS + (off + t) - r
                x = win_ref[r0:r0 + CONV_ROWS, ls] if r == 0 else sh_ref[r - 1, r0:r0 + CONV_ROWS, ls]
                term = dw_ref[t:t + 1, ls] * x
                acc = term if acc is None else acc + term
            cv_ref[rc * CONV_ROWS:(rc + 1) * CONV_ROWS, ls] = acc
    y = _ln(cv_ref[...] + cb_ref[...], cg_ref[...], cbeta_ref[...])
    c = y * jax.nn.sigmoid(y)
    gathered = jnp.concatenate(
        [c[bb * SPREAD + SPREAD - N_META:(bb + 1) * SPREAD] for bb in range(BATCH)]
        + [jnp.zeros((TM - N_META_ROWS, CONV_CH), f32)], axis=0)
    return jnp.where(s == 0, gathered, c).astype(bf16)


def _mix_kernel(h_ref, a_ref, glu_ref, gprev_ref, gmeta_ref, dw_ref, cb_ref, cg_ref, cbeta_ref, *rest):
    wg_refs = rest[:GATE_BLOCKS]
    (wap_ref, wcp_ref, wo_ref, g_ref, b_ref, o_ref,
     win_ref, sh_ref, cv_ref, wgb_ref, wapb_ref, wcpb_ref, wob_ref) = rest[GATE_BLOCKS:]
    s = pl.program_id(0)

    @pl.when(s == 0)
    def _():
        for k in range(GATE_BLOCKS):
            wgb_ref[:, k * MXU_COLS:(k + 1) * MXU_COLS] = wg_refs[k][...].astype(bf16)
        wapb_ref[...] = wap_ref[...].astype(bf16)
        wcpb_ref[...] = wcp_ref[...].astype(bf16)
        wob_ref[...] = wo_ref[...].astype(bf16)

    h = h_ref[...]
    gates = jnp.dot(h.astype(bf16), wgb_ref[...], preferred_element_type=f32)
    ya = jnp.dot(a_ref[...], wapb_ref[...], preferred_element_type=f32)
    _conv_window(s, glu_ref, gprev_ref, gmeta_ref, win_ref, sh_ref)
    c = _conv_finish(s, dw_ref, cb_ref, cg_ref, cbeta_ref, win_ref, sh_ref, cv_ref)
    yc = jnp.dot(c, wcpb_ref[...], preferred_element_type=f32)
    m = jax.nn.sigmoid(gates[:, :D_MODEL]) * ya + jax.nn.sigmoid(gates[:, D_MODEL:]) * yc
    mix = jnp.dot(m.astype(bf16), wob_ref[...], preferred_element_type=f32)
    o_ref[...] = _ln(ALPHA * h + mix, g_ref[...], b_ref[...])


def _mix(layer, h, a, glu, dw, cb, cg, cbeta, w_in, wap, wcp, wo, g, b):
    row = lambda w: pl.BlockSpec((TM, w), lambda s: (s, 0))
    gate_block = lambda k: pl.BlockSpec((None, D_MODEL, MXU_COLS),
                                        lambda s: (layer, 0, GLU_END // MXU_COLS + k),
                                        pipeline_mode=pl.Buffered(1))
    gprev = pl.BlockSpec((CONV_HALO, CONV_CH),
                         lambda s: (jnp.maximum(s * (TM // CONV_HALO) - 1, 0), 0))
    gmeta = pl.BlockSpec((N_META, CONV_CH), lambda s: (jnp.maximum(s - 1, 0) // TILES_PER_BATCH, 0))
    per_layer = lambda r, c: pl.BlockSpec((None, r, c), lambda s: (layer, 0, 0), pipeline_mode=pl.Buffered(1))
    return pl.pallas_call(
        _mix_kernel,
        grid=(N_TILES,),
        in_specs=[
            row(D_MODEL), row(ATTN_WIDTH), row(CONV_CH), gprev, gmeta,
            _full((CONV_WIDTH, CONV_CH)), _full((1, CONV_CH)), _full((1, CONV_CH)), _full((1, CONV_CH)),
            *[gate_block(k) for k in range(GATE_BLOCKS)], per_layer(ATTN_WIDTH, D_MODEL),
            per_layer(CONV_CH, D_MODEL), per_layer(D_MODEL, D_MODEL),
            _full((1, D_MODEL)), _full((1, D_MODEL)),
        ],
        out_specs=row(D_MODEL),
        out_shape=jax.ShapeDtypeStruct((NT, D_MODEL), f32),
        scratch_shapes=[
            pltpu.VMEM((CONV_HALO + TM, CONV_CH), f32),
            pltpu.VMEM((7, SH_ROWS, CONV_CH), f32),
            pltpu.VMEM((TM, CONV_CH), f32),
            pltpu.VMEM((D_MODEL, 2 * D_MODEL), bf16),
            pltpu.VMEM((ATTN_WIDTH, D_MODEL), bf16),
            pltpu.VMEM((CONV_CH, D_MODEL), bf16),
            pltpu.VMEM((D_MODEL, D_MODEL), bf16),
        ],
        compiler_params=_cparams("arbitrary"),
        name="conv_mix_ln1",
    )(h, a, glu, glu, glu, dw, cb, cg, cbeta, *([w_in] * GATE_BLOCKS), wap, wcp, wo, g, b)


def _swiglu_chunks(xb, wg_ref, wu_ref, wd_ref, chunks, row_scale=None):
    acc = None
    c0 = 0
    for w in chunks:
        gt = jnp.dot(xb, wg_ref[:, c0:c0 + w].astype(bf16), preferred_element_type=f32)
        up = jnp.dot(xb, wu_ref[:, c0:c0 + w].astype(bf16), preferred_element_type=f32)
        act = gt * jax.nn.sigmoid(gt) * up
        if row_scale is not None:
            act = act * row_scale
        part = jnp.dot(act.astype(bf16), wd_ref[c0:c0 + w, :].astype(bf16), preferred_element_type=f32)
        acc = part if acc is None else acc + part
        c0 += w
    return acc


def _ffn_kernel(h_ref, wg_ref, wu_ref, wd_ref, g_ref, b_ref, o_ref):
    h = h_ref[...]
    f = _swiglu_chunks(h.astype(bf16), wg_ref, wu_ref, wd_ref, FF_CHUNKS_DENSE)
    o_ref[...] = _ln(ALPHA * h + f, g_ref[...], b_ref[...])


def _dense_ffn(j, h, wg, wu, wd, g, b):
    row = pl.BlockSpec((TM, D_MODEL), lambda s: (s, 0))
    per_layer = lambda r, c: pl.BlockSpec((None, r, c), lambda s: (j, 0, 0), pipeline_mode=pl.Buffered(1))
    return pl.pallas_call(
        _ffn_kernel,
        grid=(N_TILES,),
        in_specs=[row, per_layer(D_MODEL, D_FF), per_layer(D_MODEL, D_FF), per_layer(D_FF, D_MODEL),
                  _full((1, D_MODEL)), _full((1, D_MODEL))],
        out_specs=row,
        out_shape=jax.ShapeDtypeStruct((NT, D_MODEL), f32),
        compiler_params=_cparams("parallel"),
        name="dense_ffn_ln2",
    )(h, wg, wu, wd, g, b)


def _route_copy(src_ref, xs_ref, sem, e, row):
    return pltpu.make_async_copy(
        src_ref.at[pl.ds(e * CHUNK, CHUNK), :],
        xs_ref.at[pl.ds(pl.multiple_of(e * CAP + row, ALIGN), CHUNK), :],
        sem)


def _zero_copy(zbuf_ref, xs_ref, sem, e, row):
    return pltpu.make_async_copy(
        zbuf_ref,
        xs_ref.at[pl.ds(pl.multiple_of(e * CAP + row, ALIGN), TM), :],
        sem)


def _dispatch_pass(p, ranki, sel, gate, hi, dst_ref):
    slot_id = lax.broadcasted_iota(i32, (CHUNK, TM), 0) + p * CHUNK
    lane = lax.broadcasted_iota(i32, (CHUNK, LANES), 1)
    ps = []
    for e in range(N_EXPERTS):
        pe = (slot_id == ranki[e:e + 1, :]) & sel[e:e + 1, :]
        ps.append(jnp.where(pe, 1.0, 0.0).astype(bf16))
        gs = jnp.sum(jnp.where(pe, gate[e:e + 1, :], 0.0), axis=1, keepdims=True)
        g_hi = gs.astype(bf16).astype(f32)
        g_mid = (gs - g_hi).astype(bf16).astype(f32)
        g_lo = gs - g_hi - g_mid
        parts = jnp.where(lane == 0, g_hi, jnp.where(lane == 1, g_mid, jnp.where(lane == 2, g_lo, 0.0)))
        dst_ref[e * CHUNK:(e + 1) * CHUNK, D_MODEL:] = parts.astype(bf16)
    pall = jnp.concatenate(ps, axis=0)
    dst_ref[:, :D_MODEL] = jnp.dot(pall, hi, preferred_element_type=f32).astype(bf16)


def _route_kernel(h_ref, r_ref, info_ref, base_ref, rows_ref, te_ref, tb_ref, tv_ref, xs_ref,
                  xbuf_ref, xov_ref, zbuf_ref, cnt_ref, sem_ref):
    s = pl.program_id(0)
    slot = s % 2

    @pl.when(s == 0)
    def _():
        for e in range(N_EXPERTS):
            cnt_ref[e] = 0
        zbuf_ref[...] = jnp.zeros((TM, XS_W), bf16)

    h = h_ref[...]
    hi = h.astype(bf16)
    lo = (h - hi.astype(f32)).astype(bf16)
    pa = jnp.dot(hi, r_ref[...], preferred_element_type=f32)
    pb = jnp.dot(lo, r_ref[...], preferred_element_type=f32)
    logits = pa + pltpu.roll(pa, LANES - N_EXPERTS, axis=1) + pb
    lt = logits.T[:N_EXPERTS, :]

    eid = lax.broadcasted_iota(i32, (N_EXPERTS, TM), 0)
    m1 = jnp.max(lt, axis=0, keepdims=True)
    i1 = jnp.min(jnp.where(lt == m1, eid, N_EXPERTS), axis=0, keepdims=True)
    lt2 = jnp.where(eid == i1, NEG, lt)
    m2 = jnp.max(lt2, axis=0, keepdims=True)
    i2 = jnp.min(jnp.where(lt2 == m2, eid, N_EXPERTS), axis=0, keepdims=True)
    ex = jnp.exp(m2 - m1)
    g1 = 1.0 / (1.0 + ex)
    g2 = ex * g1
    tok = lax.broadcasted_iota(i32, (1, TM), 1)
    valid = (s > 0) | (tok < N_META_ROWS)
    sel1 = (eid == i1) & valid
    sel2 = (eid == i2) & valid
    sel = sel1 | sel2
    onehot = jnp.where(sel, 1.0, 0.0)
    tri = jnp.where(lax.broadcasted_iota(i32, (TM, TM), 0) < lax.broadcasted_iota(i32, (TM, TM), 1), 1.0, 0.0)
    rank = jnp.dot(onehot, tri, preferred_element_type=f32)
    gate = jnp.where(sel1, g1, jnp.where(sel2, g2, 0.0))
    rank1 = jnp.sum(jnp.where(sel1, rank, 0.0), axis=0, keepdims=True)
    rank2 = jnp.sum(jnp.where(sel2, rank, 0.0), axis=0, keepdims=True)
    info = jnp.concatenate([i1.astype(f32), i2.astype(f32), rank1, rank2,
                            jnp.zeros((LANES - 4, TM), f32)], axis=0)
    info_ref[...] = info.T
    ranki = rank.astype(i32)

    _dispatch_pass(0, ranki, sel, gate, hi, xbuf_ref.at[slot])

    @pl.when(s > 0)
    def _():
        for e in range(N_EXPERTS):
            _route_copy(xbuf_ref.at[1 - slot], xs_ref, sem_ref.at[1 - slot, e], e, 0).wait()

    bases, counts = [], []
    for e in range(N_EXPERTS):
        base = cnt_ref[e]
        n_e = jnp.sum(onehot[e:e + 1, :]).astype(i32)
        bases.append(base)
        counts.append(n_e)
        base_ref[s, e] = base
        _route_copy(xbuf_ref.at[slot], xs_ref, sem_ref.at[slot, e], e, base).start()
        cnt_ref[e] = base + ((n_e + (ALIGN - 1)) // ALIGN) * ALIGN
    most = functools.reduce(jnp.maximum, counts)

    for p in range(1, N_PASS):
        @pl.when(most > p * CHUNK)
        def _(p=p):
            _dispatch_pass(p, ranki, sel, gate, hi, xov_ref)
            for e in range(N_EXPERTS):
                _route_copy(xov_ref, xs_ref, sem_ref.at[2, e], e, bases[e] + p * CHUNK).start()
            for e in range(N_EXPERTS):
                _route_copy(xov_ref, xs_ref, sem_ref.at[2, e], e, 0).wait()

    @pl.when(s == N_TILES - 1)
    def _():
        for e in range(N_EXPERTS):
            _route_copy(xbuf_ref.at[slot], xs_ref, sem_ref.at[slot, e], e, 0).wait()
        used = [cnt_ref[e] for e in range(N_EXPERTS)]
        for e in range(N_EXPERTS):
            _zero_copy(zbuf_ref, xs_ref, sem_ref.at[2, e], e, used[e]).start()
        starts, ends, total = [], [], 0
        for e in range(N_EXPERTS):
            n_t = jnp.maximum(lax.shift_right_logical(used[e] + (TM - 1), TM.bit_length() - 1), 1)
            rows_ref[e] = n_t * TM
            starts.append(total)
            total = total + n_t
            ends.append(total)

        def tile_entry(t):
            e_t = sum(jnp.where(t >= ends[e], 1, 0) for e in range(N_EXPERTS - 1))
            first = sum(jnp.where(e_t == e, starts[e], 0) for e in range(N_EXPERTS))
            return e_t, e_t * (CAP // TM) + (t - first)

        e_last, blk_last = tile_entry(total - 1)
        for t in range(FFN_GRID):
            e_t, blk_t = tile_entry(t)
            live = t < total
            te_ref[t] = jnp.where(live, e_t, e_last)
            tb_ref[t] = jnp.where(live, blk_t, blk_last)
            tv_ref[t] = jnp.where(live, 1, 0)
        for e in range(N_EXPERTS):
            _zero_copy(zbuf_ref, xs_ref, sem_ref.at[2, e], e, 0).wait()


def _route(h, rsplit):
    return pl.pallas_call(
        _route_kernel,
        grid=(N_TILES,),
        in_specs=[pl.BlockSpec((TM, D_MODEL), lambda s: (s, 0)), _full((D_MODEL, LANES))],
        out_specs=[
            pl.BlockSpec((TM, LANES), lambda s: (s, 0)),
            _smem(), _smem(), _smem(), _smem(), _smem(),
            pl.BlockSpec(memory_space=pl.ANY),
        ],
        out_shape=[
            jax.ShapeDtypeStruct((NT, LANES), f32),
            jax.ShapeDtypeStruct((N_TILES, N_EXPERTS), i32),
            jax.ShapeDtypeStruct((N_EXPERTS,), i32),
            jax.ShapeDtypeStruct((FFN_GRID,), i32),
            jax.ShapeDtypeStruct((FFN_GRID,), i32),
            jax.ShapeDtypeStruct((FFN_GRID,), i32),
            jax.ShapeDtypeStruct((N_EXPERTS * CAP, XS_W), bf16),
        ],
        scratch_shapes=[
            pltpu.VMEM((2, N_EXPERTS * CHUNK, XS_W), bf16),
            pltpu.VMEM((N_EXPERTS * CHUNK, XS_W), bf16),
            pltpu.VMEM((TM, XS_W), bf16),
            pltpu.SMEM((N_EXPERTS,), i32),
            pltpu.SemaphoreType.DMA((3, N_EXPERTS)),
        ],
        compiler_params=_cparams("arbitrary"),
        name="moe_route_dispatch",
    )(h, rsplit)


def _expert_kernel(te_ref, tb_ref, tv_ref, x_ref, wg_ref, wu_ref, wd_ref, y_ref):
    t = pl.program_id(0)

    @pl.when(tv_ref[t] > 0)
    def _():
        x = x_ref[...]
        gate = jnp.sum(x[:, D_MODEL:].astype(f32), axis=1, keepdims=True)
        y = _swiglu_chunks(x[:, :D_MODEL], wg_ref, wu_ref, wd_ref, FF_CHUNKS_MOE, row_scale=gate)
        y_ref[...] = y.astype(bf16)


def _expert_ffn(j, tile_e, tile_blk, tile_valid, xs, wg, wu, wd):
    per_expert = lambda r, c: pl.BlockSpec((None, None, r, c), lambda t, te, tb, tv: (j, te[t], 0, 0))
    grid_spec = pltpu.PrefetchScalarGridSpec(
        num_scalar_prefetch=3,
        grid=(FFN_GRID,),
        in_specs=[
            pl.BlockSpec((TM, XS_W), lambda t, te, tb, tv: (tb[t], 0)),
            per_expert(D_MODEL, D_EXPERT), per_expert(D_MODEL, D_EXPERT), per_expert(D_EXPERT, D_MODEL),
        ],
        out_specs=pl.BlockSpec((TM, D_MODEL), lambda t, te, tb, tv: (tb[t], 0)),
    )
    return pl.pallas_call(
        _expert_kernel,
        grid_spec=grid_spec,
        out_shape=jax.ShapeDtypeStruct((N_EXPERTS * CAP, D_MODEL), bf16),
        compiler_params=_cparams("arbitrary"),
        name="moe_expert_ffn",
    )(tile_e, tile_blk, tile_valid, xs, wg, wu, wd)


def _combine_copy(ys_ref, dst_ref, sem, e, row):
    return pltpu.make_async_copy(
        ys_ref.at[pl.ds(pl.multiple_of(e * CAP + row, ALIGN), CHUNK), :],
        dst_ref.at[pl.ds(e * CHUNK, CHUNK), :],
        sem)


def _window(base_ref, rows_ref, s, e, p):
    want = base_ref[s, e] + p * CHUNK
    start = jnp.minimum(want, rows_ref[e] - CHUNK)
    return start, want - start


def _onehot_cols(info, deltas, p):
    col = lax.broadcasted_iota(i32, (TM, N_EXPERTS * CHUNK), 1)
    hit = None
    for k in range(2):
        ek = info[:, k:k + 1].astype(i32)
        rk = info[:, 2 + k:3 + k].astype(i32) - p * CHUNK
        dk = functools.reduce(lambda a, b: a + b,
                              [jnp.where(ek == e, deltas[e], 0) for e in range(N_EXPERTS)])
        ck = jnp.where((rk >= 0) & (rk < CHUNK), ek * CHUNK + rk + dk, -1)
        hk = col == ck
        hit = hk if hit is None else hit | hk
    return jnp.where(hit, 1.0, 0.0).astype(bf16)


def _combine_kernel(base_ref, rows_ref, h_ref, info_ref, ys_ref, g_ref, b_ref, o_ref,
                    ybuf_ref, yov_ref, f_ref, sem_ref):
    i = pl.program_id(0)
    n = pl.num_programs(0)
    s = i + 1
    slot = i % 2

    def start_fetch(tile, to_slot):
        for e in range(N_EXPERTS):
            start, _ = _window(base_ref, rows_ref, tile, e, 0)
            _combine_copy(ys_ref, ybuf_ref.at[to_slot], sem_ref.at[to_slot, e], e, start).start()

    @pl.when(i == 0)
    def _():
        start_fetch(s, slot)

    @pl.when(i + 1 < n)
    def _():
        start_fetch(s + 1, 1 - slot)

    info = info_ref[...]
    deltas = [_window(base_ref, rows_ref, s, e, 0)[1] for e in range(N_EXPERTS)]
    qm = _onehot_cols(info, deltas, 0)
    for e in range(N_EXPERTS):
        _combine_copy(ys_ref, ybuf_ref.at[slot], sem_ref.at[slot, e], e, 0).wait()
    f_ref[...] = jnp.dot(qm, ybuf_ref[slot], preferred_element_type=f32)

    deepest = jnp.max(jnp.maximum(info[:, 2:3], info[:, 3:4])).astype(i32)
    for p in range(1, N_PASS):
        @pl.when(deepest >= p * CHUNK)
        def _(p=p):
            wins = [_window(base_ref, rows_ref, s, e, p) for e in range(N_EXPERTS)]
            for e in range(N_EXPERTS):
                _combine_copy(ys_ref, yov_ref, sem_ref.at[2, e], e, wins[e][0]).start()
            qp = _onehot_cols(info, [w[1] for w in wins], p)
            for e in range(N_EXPERTS):
                _combine_copy(ys_ref, yov_ref, sem_ref.at[2, e], e, 0).wait()
            f_ref[...] += jnp.dot(qp, yov_ref[...], preferred_element_type=f32)

    o_ref[...] = _ln(ALPHA * h_ref[...] + f_ref[...], g_ref[...], b_ref[...])


def _combine(base, rows, h, info, ys, g, b):
    n_out = N_TILES - 1
    grid_spec = pltpu.PrefetchScalarGridSpec(
        num_scalar_prefetch=2,
        grid=(n_out,),
        in_specs=[
            pl.BlockSpec((TM, D_MODEL), lambda i, base, rows: (i + 1, 0)),
            pl.BlockSpec((TM, LANES), lambda i, base, rows: (i + 1, 0)),
            pl.BlockSpec(memory_space=pl.ANY),
            pl.BlockSpec((1, D_MODEL), lambda i, base, rows: (0, 0)),
            pl.BlockSpec((1, D_MODEL), lambda i, base, rows: (0, 0)),
        ],
        out_specs=pl.BlockSpec((TM, D_MODEL), lambda i, base, rows: (i, 0)),
        scratch_shapes=[
            pltpu.VMEM((2, N_EXPERTS * CHUNK, D_MODEL), bf16),
            pltpu.VMEM((N_EXPERTS * CHUNK, D_MODEL), bf16),
            pltpu.VMEM((TM, D_MODEL), f32),
            pltpu.SemaphoreType.DMA((3, N_EXPERTS)),
        ],
    )
    return pl.pallas_call(
        _combine_kernel,
        grid_spec=grid_spec,
        out_shape=jax.ShapeDtypeStruct((n_out * TM, D_MODEL), f32),
        compiler_params=_cparams("arbitrary"),
        name="moe_combine_ln2",
    )(base, rows, h, info, ys, g, b)


def _split2(x):
    hi = x.astype(bf16)
    lo = (x - hi.astype(f32)).astype(bf16)
    return hi, lo


def kernel(x, meta_tokens, emb_ln_g, emb_ln_b, rel_bias, w_in, conv_dw, conv_b, conv_ln_g, conv_ln_b, sinks, w_attn_proj, w_conv_proj, w_out, ln1_g, ln1_b, ffn_w_gate, ffn_w_up, ffn_w_down, router, moe_w_gate, moe_w_up, moe_w_down, ln2_g, ln2_b):
    row = lambda v: v.reshape(1, -1)
    h = None
    out = None
    bias_real, bias_meta = _bias_tables(rel_bias)
    for i in range(DEPTH):
        if i == 0:
            q, k, v, glu, h = _inproj(i, w_in, x2d=x.reshape(BATCH * SEQ, D_MODEL), meta=meta_tokens,
                                      eg=row(emb_ln_g), eb=row(emb_ln_b))
        else:
            q, k, v, glu = _inproj(i, w_in, h=h)
        a = _attention(bias_real, bias_meta, sinks[i], q, k, v)
        h = _mix(i, h, a, glu, conv_dw[i], row(conv_b[i]), row(conv_ln_g[i]), row(conv_ln_b[i]),
                 w_in, w_attn_proj, w_conv_proj, w_out, row(ln1_g[i]), row(ln1_b[i]))
        j = i // 2
        if i % 2 == 0:
            h = _dense_ffn(j, h, ffn_w_gate, ffn_w_up, ffn_w_down, row(ln2_g[i]), row(ln2_b[i]))
        else:
            r_hi, r_lo = _split2(router[j])
            rsplit = jnp.concatenate(
                [r_hi, r_lo, jnp.zeros((D_MODEL, LANES - 2 * N_EXPERTS), bf16)], axis=1)
            info, base, rows, tile_e, tile_blk, tile_valid, xs = _route(h, rsplit)
            ys = _expert_ffn(j, tile_e, tile_blk, tile_valid, xs, moe_w_gate, moe_w_up, moe_w_down)
            out = _combine(base, rows, h, info, ys, row(ln2_g[i]), row(ln2_b[i]))
    return out.reshape(BATCH, SEQ, D_MODEL)
```

```python
import functools
import math

import numpy as np
import jax
import jax.numpy as jnp
from jax import lax
from jax.experimental import pallas as pl
from jax.experimental.pallas import tpu as pltpu

f32 = jnp.float32
bf16 = jnp.bfloat16
i32 = jnp.int32

D_MODEL = 1024
BATCH = 8
SEQ = 2048
DEPTH = 2
N_META = 16
HEAD_DIM = 64
N_Q_HEADS = 8
N_KV_HEADS = 2
GROUP = N_Q_HEADS // N_KV_HEADS
ATTN_WIDTH = N_Q_HEADS * HEAD_DIM
KV_WIDTH = N_KV_HEADS * HEAD_DIM
WINDOW = 128
BLOCK = 128
CONV_CH = D_MODEL // 2
CONV_WIDTH = 31
N_BUCKETS = 32
MAX_DISTANCE = 128
D_FF = 2816
N_EXPERTS = 8
D_EXPERT = 1408
ALPHA = (2 * DEPTH) ** 0.25
LN_EPS = 1e-5
Q_END = ATTN_WIDTH
K_END = Q_END + KV_WIDTH
V_END = K_END + KV_WIDTH
GLU_END = V_END + 2 * CONV_CH

TM = 512
N_META_ROWS = BATCH * N_META
NT = TM + BATCH * SEQ
N_TILES = NT // TM
TILES_PER_BATCH = SEQ // TM
BLOCKS_PER_TILE = TM // BLOCK
KSEG = 3 * BLOCK
CONV_HALO = 32
NEG = -1e30
LANES = 128
SUBLANES = 8

CHUNK = 192
N_PASS = -(-TM // CHUNK)
ALIGN = 2 * SUBLANES
XS_W = D_MODEL + LANES
CAP = 35 * TM
FFN_GRID = 85
MXU_COLS = 256
FF_CHUNKS_DENSE = (256,) * 11
FF_CHUNKS_MOE = (256,) * 5 + (128,)

VMEM_LIMIT = 56 * 1024 * 1024


def _cparams(sem="arbitrary"):
    return pltpu.CompilerParams(dimension_semantics=(sem,), vmem_limit_bytes=VMEM_LIMIT)


def _ln(x, g, b):
    mu = jnp.mean(x, -1, keepdims=True)
    xc = x - mu
    var = jnp.mean(xc * xc, -1, keepdims=True)
    return xc * lax.rsqrt(var + LN_EPS) * g + b


def _full(shape):
    return pl.BlockSpec(shape, lambda *_: (0,) * len(shape))


def _smem():
    return pl.BlockSpec(memory_space=pltpu.SMEM)


def _embed_tile(s, x_ref, meta_ref, g_ref, b_ref):
    m = _ln(meta_ref[...], g_ref[...], b_ref[...])
    meta_tile = jnp.concatenate([m] * BATCH + [jnp.zeros((TM - N_META_ROWS, D_MODEL), f32)], axis=0)
    return jnp.where(s == 0, meta_tile, _ln(x_ref[...], g_ref[...], b_ref[...]))


def _inproj_kernel(embed, *refs):
    s = pl.program_id(0)
    if embed:
        x_ref, meta_ref, eg_ref, eb_ref, w_ref, q_ref, k_ref, v_ref, glu_ref, h_ref, wb_ref = refs
        h = _embed_tile(s, x_ref, meta_ref, eg_ref, eb_ref)
        h_ref[...] = h
    else:
        hin_ref, w_ref, q_ref, k_ref, v_ref, glu_ref, wb_ref = refs
        h = hin_ref[...]

    @pl.when(s == 0)
    def _():
        wb_ref[...] = w_ref[...].astype(bf16)

    hb = h.astype(bf16)
    col = lambda c0, w: jnp.dot(hb, wb_ref[:, c0:c0 + w], preferred_element_type=f32)
    for c0 in range(0, Q_END, MXU_COLS):
        q_ref[:, c0:c0 + MXU_COLS] = (col(c0, MXU_COLS) * (HEAD_DIM ** -0.5)).astype(bf16)
    kv = col(Q_END, 2 * KV_WIDTH)
    k_ref[...] = kv[:, :KV_WIDTH].astype(bf16)
    v_ref[...] = kv[:, KV_WIDTH:].astype(bf16)
    for c0 in range(0, CONV_CH, MXU_COLS):
        val = col(V_END + c0, MXU_COLS)
        gate = col(V_END + CONV_CH + c0, MXU_COLS)
        glu_ref[:, c0:c0 + MXU_COLS] = (val * jax.nn.sigmoid(gate)).astype(bf16)


def _inproj(layer, w_in, h=None, x2d=None, meta=None, eg=None, eb=None):
    embed = h is None
    row = lambda w: pl.BlockSpec((TM, w), lambda s: (s, 0))
    w_spec = pl.BlockSpec((None, D_MODEL, GLU_END), lambda s: (layer, 0, 0), pipeline_mode=pl.Buffered(1))
    out_specs = [row(ATTN_WIDTH), row(KV_WIDTH), row(KV_WIDTH), row(CONV_CH)]
    out_shape = [
        jax.ShapeDtypeStruct((NT, ATTN_WIDTH), bf16),
        jax.ShapeDtypeStruct((NT, KV_WIDTH), bf16),
        jax.ShapeDtypeStruct((NT, KV_WIDTH), bf16),
        jax.ShapeDtypeStruct((NT, CONV_CH), bf16),
    ]
    if embed:
        in_specs = [
            pl.BlockSpec((TM, D_MODEL), lambda s: (jnp.maximum(s - 1, 0), 0)),
            _full((N_META, D_MODEL)), _full((1, D_MODEL)), _full((1, D_MODEL)), w_spec,
        ]
        args = (x2d, meta, eg, eb, w_in)
        out_specs.append(row(D_MODEL))
        out_shape.append(jax.ShapeDtypeStruct((NT, D_MODEL), f32))
    else:
        in_specs = [row(D_MODEL), w_spec]
        args = (h, w_in)
    return pl.pallas_call(
        functools.partial(_inproj_kernel, embed),
        grid=(N_TILES,),
        in_specs=in_specs,
        out_specs=out_specs,
        out_shape=out_shape,
        scratch_shapes=[pltpu.VMEM((D_MODEL, GLU_END), bf16)],
        compiler_params=_cparams("arbitrary"),
        name="inproj_glu",
    )(*args)


def _rel_bucket_np(dist):
    n = np.maximum(dist, 0)
    max_exact = N_BUCKETS // 2
    nf = np.maximum(n, 1).astype(np.float32)
    scaled = (np.log(nf / np.float32(max_exact)) / np.float32(math.log(MAX_DISTANCE / max_exact))
              * np.float32(N_BUCKETS - max_exact))
    large = np.minimum(max_exact + scaled.astype(np.int32), N_BUCKETS - 1)
    return np.where(n < max_exact, n, large).astype(np.int32)


def _bucket_tables():
    a = np.arange(BLOCK)[:, None]
    m = np.arange(BLOCK)[None, :]
    kk = np.arange(2 * BLOCK)[None, :]
    dist_band = BLOCK + a - kk
    band_ok = (dist_band >= 0) & (dist_band < WINDOW)
    band = np.where(band_ok, _rel_bucket_np(dist_band), -1)
    real = np.full((2, BLOCK, KSEG), -1, np.int32)
    for t, n in enumerate((0, 1)):
        dist_meta = N_META + n * BLOCK + a - m
        real[t, :, :BLOCK] = np.where(m < N_META, _rel_bucket_np(dist_meta), -1)
        real[t, :, BLOCK:] = band
    real[0, :, BLOCK:2 * BLOCK] = -1
    for n in range(2, SEQ // BLOCK):
        dist_meta = N_META + n * BLOCK + a - m
        assert np.array_equal(np.where(m < N_META, _rel_bucket_np(dist_meta), -1), real[1, :, :BLOCK])
    r = np.arange(N_META_ROWS)
    same = (r[:, None] // N_META) == (r[None, :] // N_META)
    dmm = (r[:, None] % N_META) - (r[None, :] % N_META)
    meta = np.where(same & (dmm >= 0), _rel_bucket_np(dmm), -1).astype(np.int32)
    return real, meta


_BK_REAL, _BK_META = _bucket_tables()


def _bias_from_buckets(bk, rb_ref, h):
    acc = jnp.full(bk.shape, NEG, f32)
    for bkt in range(N_BUCKETS):
        acc = jnp.where(bk == bkt, rb_ref[bkt, h], acc)
    return acc


def _expand_kv(x2):
    xf = x2.astype(f32)
    r = pltpu.roll(xf, HEAD_DIM, axis=1)
    low = lax.broadcasted_iota(i32, xf.shape, 1) < HEAD_DIM
    h0 = jnp.where(low, xf, r).astype(bf16)
    h1 = jnp.where(low, r, xf).astype(bf16)
    return (jnp.concatenate([h0, h0], axis=1), jnp.concatenate([h1, h1], axis=1))


def _block_diag(x, n_keys):
    lane_blk = lax.broadcasted_iota(i32, (n_keys, GROUP * HEAD_DIM), 1) // HEAD_DIM
    zero = jnp.zeros_like(x)
    return jnp.concatenate([jnp.where(lane_blk == h, x, zero) for h in range(GROUP)], axis=0)


def _softmax_block(sh, sink):
    mx = jnp.maximum(jnp.max(sh, -1, keepdims=True), sink)
    p = jnp.exp(sh - mx)
    den = jnp.sum(p, -1, keepdims=True) + jnp.exp(sink - mx)
    return p.astype(bf16), 1.0 / den


def _head_scale(rs, m_rows):
    ol = lax.broadcasted_iota(i32, (m_rows, GROUP * HEAD_DIM), 1) // HEAD_DIM
    return jnp.where(ol == 0, rs[0], jnp.where(ol == 1, rs[1], jnp.where(ol == 2, rs[2], rs[3])))


def _attend(qg, kx, vx, bias_of_head, sink_of_head, n_keys):
    kbd = _block_diag(kx, n_keys)
    vbd = _block_diag(vx, n_keys)
    s = lax.dot_general(qg, kbd, (((1,), (1,)), ((), ())), preferred_element_type=f32)
    ps, rs = [], []
    for h in range(GROUP):
        p, r = _softmax_block(s[:, h * n_keys:(h + 1) * n_keys] + bias_of_head(h), sink_of_head(h))
        ps.append(p)
        rs.append(r)
    o = jnp.dot(jnp.concatenate(ps, axis=1), vbd, preferred_element_type=f32)
    return o * _head_scale(rs, qg.shape[0])


def _attend_tile(qg, kbd, vbd, kbd_meta, vbd_meta, bias_of, sink_of_head):
    nt = (((1,), (1,)), ((), ()))
    nb = BLOCKS_PER_TILE
    s_meta = lax.dot_general(qg, kbd_meta, nt, preferred_element_type=f32)
    s_prev, s_cur = [None] * nb, [None] * nb
    for b in range(nb + 1):
        lo, hi = max(b - 1, 0) * BLOCK, min(b + 1, nb) * BLOCK
        sc = lax.dot_general(qg[lo:hi], kbd[b], nt, preferred_element_type=f32)
        if b > 0:
            s_cur[b - 1] = sc[:BLOCK]
        if b < nb:
            s_prev[b] = sc[-BLOCK:]
    pm, pp, pc, scales = [], [], [], []
    for blk in range(nb):
        rows = slice(blk * BLOCK, (blk + 1) * BLOCK)
        pms, pps, pcs, rs = [], [], [], []
        for h in range(GROUP):
            cols = slice(h * BLOCK, (h + 1) * BLOCK)
            sh = jnp.concatenate([s_meta[rows, cols], s_prev[blk][:, cols], s_cur[blk][:, cols]], axis=1)
            p, r = _softmax_block(sh + bias_of(blk, h), sink_of_head(h))
            pms.append(p[:, :BLOCK])
            pps.append(p[:, BLOCK:2 * BLOCK])
            pcs.append(p[:, 2 * BLOCK:])
            rs.append(r)
        pm.append(jnp.concatenate(pms, axis=1))
        pp.append(jnp.concatenate(pps, axis=1))
        pc.append(jnp.concatenate(pcs, axis=1))
        scales.append(_head_scale(rs, BLOCK))
    o = [None] * nb
    for b in range(nb + 1):
        parts = ([pc[b - 1]] if b > 0 else []) + ([pp[b]] if b < nb else [])
        ob = jnp.dot(jnp.concatenate(parts, axis=0), vbd[b], preferred_element_type=f32)
        if b > 0:
            o[b - 1] = o[b - 1] + ob[:BLOCK]
        if b < nb:
            o[b] = ob[-BLOCK:]
    om = jnp.dot(jnp.concatenate(pm, axis=0), vbd_meta, preferred_element_type=f32)
    return (jnp.concatenate(o, axis=0) + om) * jnp.concatenate(scales, axis=0)


def _bias_kernel(rb_ref, bkr_ref, bkm_ref, br_ref, bm_ref):
    for h in range(N_Q_HEADS):
        for t in range(2):
            br_ref[t, h] = _bias_from_buckets(bkr_ref[t], rb_ref, h)
        bm_ref[h] = _bias_from_buckets(bkm_ref[...], rb_ref, h)


def _bias_tables(rel_bias):
    return pl.pallas_call(
        _bias_kernel,
        in_specs=[_smem(), pl.BlockSpec(memory_space=pltpu.VMEM), pl.BlockSpec(memory_space=pltpu.VMEM)],
        out_specs=[pl.BlockSpec(memory_space=pltpu.VMEM), pl.BlockSpec(memory_space=pltpu.VMEM)],
        out_shape=[
            jax.ShapeDtypeStruct((2, N_Q_HEADS, BLOCK, KSEG), f32),
            jax.ShapeDtypeStruct((N_Q_HEADS, N_META_ROWS, N_META_ROWS), f32),
        ],
        name="rel_bias_tables",
    )(rel_bias, jnp.asarray(_BK_REAL), jnp.asarray(_BK_META))


def _attn_kernel(sink_ref, q_ref, k_ref, v_ref, kp_ref, vp_ref, km_ref, vm_ref,
                 br_ref, bm_ref, o_ref):
    s = pl.program_id(0)
    gw = GROUP * HEAD_DIM

    @pl.when(s == 0)
    def _():
        kxs = _expand_kv(k_ref[:N_META_ROWS, :])
        vxs = _expand_kv(v_ref[:N_META_ROWS, :])
        for g in range(N_KV_HEADS):
            o = _attend(q_ref[:N_META_ROWS, g * gw:(g + 1) * gw], kxs[g], vxs[g],
                        lambda h, g=g: bm_ref[g * GROUP + h],
                        lambda h, g=g: sink_ref[g * GROUP + h], N_META_ROWS)
            o_ref[:N_META_ROWS, g * gw:(g + 1) * gw] = o.astype(bf16)
        o_ref[N_META_ROWS:, :] = jnp.zeros((TM - N_META_ROWS, ATTN_WIDTH), bf16)

    @pl.when(s > 0)
    def _():
        first = ((s - 1) % TILES_PER_BATCH) == 0
        tbl0 = jnp.where(first, 0, 1)
        zpad = jnp.zeros((BLOCK - N_META, KV_WIDTH), bf16)
        kms = _expand_kv(jnp.concatenate([km_ref[...], zpad], axis=0))
        vms = _expand_kv(jnp.concatenate([vm_ref[...], zpad], axis=0))
        kxs = [_expand_kv(kp_ref[...])] + [_expand_kv(k_ref[b * BLOCK:(b + 1) * BLOCK, :])
                                          for b in range(BLOCKS_PER_TILE)]
        vxs = [_expand_kv(vp_ref[...])] + [_expand_kv(v_ref[b * BLOCK:(b + 1) * BLOCK, :])
                                          for b in range(BLOCKS_PER_TILE)]
        for g in range(N_KV_HEADS):
            o = _attend_tile(
                q_ref[:, g * gw:(g + 1) * gw],
                [_block_diag(x[g], BLOCK) for x in kxs], [_block_diag(x[g], BLOCK) for x in vxs],
                _block_diag(kms[g], BLOCK), _block_diag(vms[g], BLOCK),
                lambda blk, h, g=g: br_ref[tbl0 if blk == 0 else 1, g * GROUP + h],
                lambda h, g=g: sink_ref[g * GROUP + h])
            o_ref[:, g * gw:(g + 1) * gw] = o.astype(bf16)


def _attention(bias_real, bias_meta, sinks, q, k, v):
    row = lambda w: pl.BlockSpec((TM, w), lambda s: (s, 0))
    prev = pl.BlockSpec((BLOCK, KV_WIDTH), lambda s: (jnp.maximum(s * BLOCKS_PER_TILE - 1, 0), 0))
    meta = pl.BlockSpec((N_META, KV_WIDTH), lambda s: (jnp.maximum(s - 1, 0) // TILES_PER_BATCH, 0))
    return pl.pallas_call(
        _attn_kernel,
        grid=(N_TILES,),
        in_specs=[
            _smem(),
            row(ATTN_WIDTH), row(KV_WIDTH), row(KV_WIDTH),
            prev, prev, meta, meta,
            _full((2, N_Q_HEADS, BLOCK, KSEG)), _full((N_Q_HEADS, N_META_ROWS, N_META_ROWS)),
        ],
        out_specs=row(ATTN_WIDTH),
        out_shape=jax.ShapeDtypeStruct((NT, ATTN_WIDTH), bf16),
        compiler_params=_cparams("arbitrary"),
        name="swa_attention",
    )(sinks, q, k, v, k, v, k, v, bias_real, bias_meta)


SPREAD = TM // BATCH
SH_ROWS = CONV_HALO + TM - SUBLANES
CONV_ROWS = 128
GATE_BLOCKS = 2 * D_MODEL // MXU_COLS


def _conv_window(s, glu_ref, gprev_ref, gmeta_ref, win_ref, sh_ref):
    is_meta = s == 0
    first = ((s - 1) % TILES_PER_BATCH) == 0
    glu = glu_ref[...].astype(f32)
    zgap = jnp.zeros((SPREAD - N_META, CONV_CH), f32)
    spread = jnp.concatenate(
        [p for bb in range(BATCH) for p in (zgap, glu[bb * N_META:(bb + 1) * N_META])], axis=0)
    win_ref[CONV_HALO:, :] = jnp.where(is_meta, spread, glu)
    halo_first = jnp.concatenate(
        [jnp.zeros((CONV_HALO - N_META, CONV_CH), f32), gmeta_ref[...].astype(f32)], axis=0)
    halo = jnp.where(first, halo_first, gprev_ref[...].astype(f32))
    win_ref[:CONV_HALO, :] = jnp.where(is_meta, 0.0, halo)

    for r in range(1, SUBLANES):
        sh_ref[r - 1] = win_ref[r:r + SH_ROWS, :]


def _conv_finish(s, dw_ref, cb_ref, cg_ref, cbeta_ref, win_ref, sh_ref, cv_ref):
    off = CONV_HALO - (CONV_WIDTH - 1)
    for rc in range(TM // CONV_ROWS):
        for lc in range(CONV_CH // LANES):
            ls = slice(lc * LANES, (lc + 1) * LANES)
            acc = None
            for t in range(CONV_WIDTH):
                r = (off + t) % SUBLANES
                r0 = rc * CONV_ROWS + (off + t) - r
                x = win_ref[r0:r0 + CONV_ROWS, ls] if r == 0 else sh_ref[r - 1, r0:r0 + CONV_ROWS, ls]
                term = dw_ref[t:t + 1, ls] * x
                acc = term if acc is None else acc + term
            cv_ref[rc * CONV_ROWS:(rc + 1) * CONV_ROWS, ls] = acc
    y = _ln(cv_ref[...] + cb_ref[...], cg_ref[...], cbeta_ref[...])
    c = y * jax.nn.sigmoid(y)
    gathered = jnp.concatenate(
        [c[bb * SPREAD + SPREAD - N_META:(bb + 1) * SPREAD] for bb in range(BATCH)]
        + [jnp.zeros((TM - N_META_ROWS, CONV_CH), f32)], axis=0)
    return jnp.where(s == 0, gathered, c).astype(bf16)


def _mix_kernel(h_ref, a_ref, glu_ref, gprev_ref, gmeta_ref, dw_ref, cb_ref, cg_ref, cbeta_ref, *rest):
    wg_refs = rest[:GATE_BLOCKS]
    (wap_ref, wcp_ref, wo_ref, g_ref, b_ref, o_ref,
     win_ref, sh_ref, cv_ref, wgb_ref, wapb_ref, wcpb_ref, wob_ref) = rest[GATE_BLOCKS:]
    s = pl.program_id(0)

    @pl.when(s == 0)
    def _():
        for k in range(GATE_BLOCKS):
            wgb_ref[:, k * MXU_COLS:(k + 1) * MXU_COLS] = wg_refs[k][...].astype(bf16)
        wapb_ref[...] = wap_ref[...].astype(bf16)
        wcpb_ref[...] = wcp_ref[...].astype(bf16)
        wob_ref[...] = wo_ref[...].astype(bf16)

    h = h_ref[...]
    gates = jnp.dot(h.astype(bf16), wgb_ref[...], preferred_element_type=f32)
    ya = jnp.dot(a_ref[...], wapb_ref[...], preferred_element_type=f32)
    _conv_window(s, glu_ref, gprev_ref, gmeta_ref, win_ref, sh_ref)
    c = _conv_finish(s, dw_ref, cb_ref, cg_ref, cbeta_ref, win_ref, sh_ref, cv_ref)
    yc = jnp.dot(c, wcpb_ref[...], preferred_element_type=f32)
    m = jax.nn.sigmoid(gates[:, :D_MODEL]) * ya + jax.nn.sigmoid(gates[:, D_MODEL:]) * yc
    mix = jnp.dot(m.astype(bf16), wob_ref[...], preferred_element_type=f32)
    o_ref[...] = _ln(ALPHA * h + mix, g_ref[...], b_ref[...])


def _mix(layer, h, a, glu, dw, cb, cg, cbeta, w_in, wap, wcp, wo, g, b):
    row = lambda w: pl.BlockSpec((TM, w), lambda s: (s, 0))
    gate_block = lambda k: pl.BlockSpec((None, D_MODEL, MXU_COLS),
                                        lambda s: (layer, 0, GLU_END // MXU_COLS + k),
                                        pipeline_mode=pl.Buffered(1))
    gprev = pl.BlockSpec((CONV_HALO, CONV_CH),
                         lambda s: (jnp.maximum(s * (TM // CONV_HALO) - 1, 0), 0))
    gmeta = pl.BlockSpec((N_META, CONV_CH), lambda s: (jnp.maximum(s - 1, 0) // TILES_PER_BATCH, 0))
    per_layer = lambda r, c: pl.BlockSpec((None, r, c), lambda s: (layer, 0, 0), pipeline_mode=pl.Buffered(1))
    return pl.pallas_call(
        _mix_kernel,
        grid=(N_TILES,),
        in_specs=[
            row(D_MODEL), row(ATTN_WIDTH), row(CONV_CH), gprev, gmeta,
            _full((CONV_WIDTH, CONV_CH)), _full((1, CONV_CH)), _full((1, CONV_CH)), _full((1, CONV_CH)),
            *[gate_block(k) for k in range(GATE_BLOCKS)], per_layer(ATTN_WIDTH, D_MODEL),
            per_layer(CONV_CH, D_MODEL), per_layer(D_MODEL, D_MODEL),
            _full((1, D_MODEL)), _full((1, D_MODEL)),
        ],
        out_specs=row(D_MODEL),
        out_shape=jax.ShapeDtypeStruct((NT, D_MODEL), f32),
        scratch_shapes=[
            pltpu.VMEM((CONV_HALO + TM, CONV_CH), f32),
            pltpu.VMEM((7, SH_ROWS, CONV_CH), f32),
            pltpu.VMEM((TM, CONV_CH), f32),
            pltpu.VMEM((D_MODEL, 2 * D_MODEL), bf16),
            pltpu.VMEM((ATTN_WIDTH, D_MODEL), bf16),
            pltpu.VMEM((CONV_CH, D_MODEL), bf16),
            pltpu.VMEM((D_MODEL, D_MODEL), bf16),
        ],
        compiler_params=_cparams("arbitrary"),
        name="conv_mix_ln1",
    )(h, a, glu, glu, glu, dw, cb, cg, cbeta, *([w_in] * GATE_BLOCKS), wap, wcp, wo, g, b)


def _swiglu_chunks(xb, wg_ref, wu_ref, wd_ref, chunks, row_scale=None):
    acc = None
    c0 = 0
    for w in chunks:
        gt = jnp.dot(xb, wg_ref[:, c0:c0 + w].astype(bf16), preferred_element_type=f32)
        up = jnp.dot(xb, wu_ref[:, c0:c0 + w].astype(bf16), preferred_element_type=f32)
        act = gt * jax.nn.sigmoid(gt) * up
        if row_scale is not None:
            act = act * row_scale
        part = jnp.dot(act.astype(bf16), wd_ref[c0:c0 + w, :].astype(bf16), preferred_element_type=f32)
        acc = part if acc is None else acc + part
        c0 += w
    return acc


def _ffn_kernel(h_ref, wg_ref, wu_ref, wd_ref, g_ref, b_ref, o_ref, ob_ref):
    h = h_ref[...]
    f = _swiglu_chunks(h.astype(bf16), wg_ref, wu_ref, wd_ref, FF_CHUNKS_DENSE)
    out = _ln(ALPHA * h + f, g_ref[...], b_ref[...])
    o_ref[...] = out
    ob_ref[...] = out.astype(bf16)


def _dense_ffn(j, h, wg, wu, wd, g, b):
    row = pl.BlockSpec((TM, D_MODEL), lambda s: (s, 0))
    per_layer = lambda r, c: pl.BlockSpec((None, r, c), lambda s: (j, 0, 0), pipeline_mode=pl.Buffered(1))
    return pl.pallas_call(
        _ffn_kernel,
        grid=(N_TILES,),
        in_specs=[row, per_layer(D_MODEL, D_FF), per_layer(D_MODEL, D_FF), per_layer(D_FF, D_MODEL),
                  _full((1, D_MODEL)), _full((1, D_MODEL))],
        out_specs=[row, row],
        out_shape=[jax.ShapeDtypeStruct((NT, D_MODEL), f32), jax.ShapeDtypeStruct((NT, D_MODEL), bf16)],
        compiler_params=_cparams("parallel"),
        name="dense_ffn_ln2",
    )(h, wg, wu, wd, g, b)


def _route_copy(src_ref, xs_ref, sem, e, row):
    return pltpu.make_async_copy(
        src_ref.at[pl.ds(e * CHUNK, CHUNK), :],
        xs_ref.at[pl.ds(pl.multiple_of(e * CAP + row, ALIGN), CHUNK), :],
        sem)


def _zero_copy(zbuf_ref, xs_ref, sem, e, row):
    return pltpu.make_async_copy(
        zbuf_ref,
        xs_ref.at[pl.ds(pl.multiple_of(e * CAP + row, ALIGN), TM), :],
        sem)


def _dispatch_pass(p, ranki, sel, gate, hi, dst_ref):
    slot_id = lax.broadcasted_iota(i32, (CHUNK, TM), 0) + p * CHUNK
    lane = lax.broadcasted_iota(i32, (CHUNK, LANES), 1)
    ps = []
    for e in range(N_EXPERTS):
        pe = (slot_id == ranki[e:e + 1, :]) & sel[e:e + 1, :]
        ps.append(jnp.where(pe, 1.0, 0.0).astype(bf16))
        gs = jnp.sum(jnp.where(pe, gate[e:e + 1, :], 0.0), axis=1, keepdims=True)
        g_hi = gs.astype(bf16).astype(f32)
        g_mid = (gs - g_hi).astype(bf16).astype(f32)
        g_lo = gs - g_hi - g_mid
        parts = jnp.where(lane == 0, g_hi, jnp.where(lane == 1, g_mid, jnp.where(lane == 2, g_lo, 0.0)))
        dst_ref[e * CHUNK:(e + 1) * CHUNK, D_MODEL:] = parts.astype(bf16)
    pall = jnp.concatenate(ps, axis=0)
    dst_ref[:, :D_MODEL] = jnp.dot(pall, hi, preferred_element_type=f32).astype(bf16)


def _route_kernel(h_ref, r_ref, info_ref, base_ref, rows_ref, te_ref, tb_ref, tv_ref, xs_ref,
                  xbuf_ref, xov_ref, zbuf_ref, cnt_ref, sem_ref):
    s = pl.program_id(0)
    slot = s % 2

    @pl.when(s == 0)
    def _():
        for e in range(N_EXPERTS):
            cnt_ref[e] = 0
        zbuf_ref[...] = jnp.zeros((TM, XS_W), bf16)

    h = h_ref[...]
    hi = h.astype(bf16)
    lo = (h - hi.astype(f32)).astype(bf16)
    pa = jnp.dot(hi, r_ref[...], preferred_element_type=f32)
    pb = jnp.dot(lo, r_ref[...], preferred_element_type=f32)
    logits = pa + pltpu.roll(pa, LANES - N_EXPERTS, axis=1) + pb
    lt = logits.T[:N_EXPERTS, :]

    eid = lax.broadcasted_iota(i32, (N_EXPERTS, TM), 0)
    m1 = jnp.max(lt, axis=0, keepdims=True)
    i1 = jnp.min(jnp.where(lt == m1, eid, N_EXPERTS), axis=0, keepdims=True)
    lt2 = jnp.where(eid == i1, NEG, lt)
    m2 = jnp.max(lt2, axis=0, keepdims=True)
    i2 = jnp.min(jnp.where(lt2 == m2, eid, N_EXPERTS), axis=0, keepdims=True)
    ex = jnp.exp(m2 - m1)
    g1 = 1.0 / (1.0 + ex)
    g2 = ex * g1
    tok = lax.broadcasted_iota(i32, (1, TM), 1)
    valid = (s > 0) | (tok < N_META_ROWS)
    sel1 = (eid == i1) & valid
    sel2 = (eid == i2) & valid
    sel = sel1 | sel2
    onehot = jnp.where(sel, 1.0, 0.0)
    tri = jnp.where(lax.broadcasted_iota(i32, (TM, TM), 0) < lax.broadcasted_iota(i32, (TM, TM), 1), 1.0, 0.0)
    rank = jnp.dot(onehot, tri, preferred_element_type=f32)
    gate = jnp.where(sel1, g1, jnp.where(sel2, g2, 0.0))
    rank1 = jnp.sum(jnp.where(sel1, rank, 0.0), axis=0, keepdims=True)
    rank2 = jnp.sum(jnp.where(sel2, rank, 0.0), axis=0, keepdims=True)
    info = jnp.concatenate([i1.astype(f32), i2.astype(f32), rank1, rank2,
                            jnp.zeros((LANES - 4, TM), f32)], axis=0)
    info_ref[...] = info.T
    ranki = rank.astype(i32)

    _dispatch_pass(0, ranki, sel, gate, hi, xbuf_ref.at[slot])

    @pl.when(s > 0)
    def _():
        for e in range(N_EXPERTS):
            _route_copy(xbuf_ref.at[1 - slot], xs_ref, sem_ref.at[1 - slot, e], e, 0).wait()

    bases, counts = [], []
    for e in range(N_EXPERTS):
        base = cnt_ref[e]
        n_e = jnp.sum(onehot[e:e + 1, :]).astype(i32)
        bases.append(base)
        counts.append(n_e)
        base_ref[s, e] = base
        _route_copy(xbuf_ref.at[slot], xs_ref, sem_ref.at[slot, e], e, base).start()
        cnt_ref[e] = base + ((n_e + (ALIGN - 1)) // ALIGN) * ALIGN
    most = functools.reduce(jnp.maximum, counts)

    for p in range(1, N_PASS):
        @pl.when(most > p * CHUNK)
        def _(p=p):
            _dispatch_pass(p, ranki, sel, gate, hi, xov_ref)
            for e in range(N_EXPERTS):
                _route_copy(xov_ref, xs_ref, sem_ref.at[2, e], e, bases[e] + p * CHUNK).start()
            for e in range(N_EXPERTS):
                _route_copy(xov_ref, xs_ref, sem_ref.at[2, e], e, 0).wait()

    @pl.when(s == N_TILES - 1)
    def _():
        for e in range(N_EXPERTS):
            _route_copy(xbuf_ref.at[slot], xs_ref, sem_ref.at[slot, e], e, 0).wait()
        used = [cnt_ref[e] for e in range(N_EXPERTS)]
        for e in range(N_EXPERTS):
            _zero_copy(zbuf_ref, xs_ref, sem_ref.at[2, e], e, used[e]).start()
        starts, ends, total = [], [], 0
        for e in range(N_EXPERTS):
            n_t = jnp.maximum(lax.shift_right_logical(used[e] + (TM - 1), TM.bit_length() - 1), 1)
            rows_ref[e] = n_t * TM
            starts.append(total)
            total = total + n_t
            ends.append(total)

        def tile_entry(t):
            e_t = sum(jnp.where(t >= ends[e], 1, 0) for e in range(N_EXPERTS - 1))
            first = sum(jnp.where(e_t == e, starts[e], 0) for e in range(N_EXPERTS))
            return e_t, e_t * (CAP // TM) + (t - first)

        e_last, blk_last = tile_entry(total - 1)
        for t in range(FFN_GRID):
            e_t, blk_t = tile_entry(t)
            live = t < total
            te_ref[t] = jnp.where(live, e_t, e_last)
            tb_ref[t] = jnp.where(live, blk_t, blk_last)
            tv_ref[t] = jnp.where(live, 1, 0)
        for e in range(N_EXPERTS):
            _zero_copy(zbuf_ref, xs_ref, sem_ref.at[2, e], e, 0).wait()


def _route(h, rsplit):
    return pl.pallas_call(
        _route_kernel,
        grid=(N_TILES,),
        in_specs=[pl.BlockSpec((TM, D_MODEL), lambda s: (s, 0)), _full((D_MODEL, LANES))],
        out_specs=[
            pl.BlockSpec((TM, LANES), lambda s: (s, 0)),
            _smem(), _smem(), _smem(), _smem(), _smem(),
            pl.BlockSpec(memory_space=pl.ANY),
        ],
        out_shape=[
            jax.ShapeDtypeStruct((NT, LANES), f32),
            jax.ShapeDtypeStruct((N_TILES, N_EXPERTS), i32),
            jax.ShapeDtypeStruct((N_EXPERTS,), i32),
            jax.ShapeDtypeStruct((FFN_GRID,), i32),
            jax.ShapeDtypeStruct((FFN_GRID,), i32),
            jax.ShapeDtypeStruct((FFN_GRID,), i32),
            jax.ShapeDtypeStruct((N_EXPERTS * CAP, XS_W), bf16),
        ],
        scratch_shapes=[
            pltpu.VMEM((2, N_EXPERTS * CHUNK, XS_W), bf16),
            pltpu.VMEM((N_EXPERTS * CHUNK, XS_W), bf16),
            pltpu.VMEM((TM, XS_W), bf16),
            pltpu.SMEM((N_EXPERTS,), i32),
            pltpu.SemaphoreType.DMA((3, N_EXPERTS)),
        ],
        compiler_params=_cparams("arbitrary"),
        name="moe_route_dispatch",
    )(h, rsplit)


def _expert_kernel(te_ref, tb_ref, tv_ref, x_ref, wg_ref, wu_ref, wd_ref, y_ref):
    t = pl.program_id(0)

    @pl.when(tv_ref[t] > 0)
    def _():
        x = x_ref[...]
        gate = jnp.sum(x[:, D_MODEL:].astype(f32), axis=1, keepdims=True)
        y = _swiglu_chunks(x[:, :D_MODEL], wg_ref, wu_ref, wd_ref, FF_CHUNKS_MOE, row_scale=gate)
        y_ref[...] = y.astype(bf16)


def _expert_ffn(j, tile_e, tile_blk, tile_valid, xs, wg, wu, wd):
    per_expert = lambda r, c: pl.BlockSpec((None, None, r, c), lambda t, te, tb, tv: (j, te[t], 0, 0))
    grid_spec = pltpu.PrefetchScalarGridSpec(
        num_scalar_prefetch=3,
        grid=(FFN_GRID,),
        in_specs=[
            pl.BlockSpec((TM, XS_W), lambda t, te, tb, tv: (tb[t], 0)),
            per_expert(D_MODEL, D_EXPERT), per_expert(D_MODEL, D_EXPERT), per_expert(D_EXPERT, D_MODEL),
        ],
        out_specs=pl.BlockSpec((TM, D_MODEL), lambda t, te, tb, tv: (tb[t], 0)),
    )
    return pl.pallas_call(
        _expert_kernel,
        grid_spec=grid_spec,
        out_shape=jax.ShapeDtypeStruct((N_EXPERTS * CAP, D_MODEL), bf16),
        compiler_params=_cparams("arbitrary"),
        name="moe_expert_ffn",
    )(tile_e, tile_blk, tile_valid, xs, wg, wu, wd)


def _combine_copy(ys_ref, dst_ref, sem, e, row):
    return pltpu.make_async_copy(
        ys_ref.at[pl.ds(pl.multiple_of(e * CAP + row, ALIGN), CHUNK), :],
        dst_ref.at[pl.ds(e * CHUNK, CHUNK), :],
        sem)


def _window(base_ref, rows_ref, s, e, p):
    want = base_ref[s, e] + p * CHUNK
    start = jnp.minimum(want, rows_ref[e] - CHUNK)
    return start, want - start


def _onehot_cols(info, deltas, p):
    col = lax.broadcasted_iota(i32, (TM, N_EXPERTS * CHUNK), 1)
    hit = None
    for k in range(2):
        ek = info[:, k:k + 1].astype(i32)
        rk = info[:, 2 + k:3 + k].astype(i32) - p * CHUNK
        dk = functools.reduce(lambda a, b: a + b,
                              [jnp.where(ek == e, deltas[e], 0) for e in range(N_EXPERTS)])
        ck = jnp.where((rk >= 0) & (rk < CHUNK), ek * CHUNK + rk + dk, -1)
        hk = col == ck
        hit = hk if hit is None else hit | hk
    return jnp.where(hit, 1.0, 0.0).astype(bf16)


def _combine_kernel(base_ref, rows_ref, h_ref, info_ref, ys_ref, g_ref, b_ref, o_ref,
                    ybuf_ref, yov_ref, f_ref, sem_ref):
    i = pl.program_id(0)
    n = pl.num_programs(0)
    s = i + 1
    slot = i % 2

    def start_fetch(tile, to_slot):
        for e in range(N_EXPERTS):
            start, _ = _window(base_ref, rows_ref, tile, e, 0)
            _combine_copy(ys_ref, ybuf_ref.at[to_slot], sem_ref.at[to_slot, e], e, start).start()

    @pl.when(i == 0)
    def _():
        start_fetch(s, slot)

    @pl.when(i + 1 < n)
    def _():
        start_fetch(s + 1, 1 - slot)

    info = info_ref[...]
    deltas = [_window(base_ref, rows_ref, s, e, 0)[1] for e in range(N_EXPERTS)]
    qm = _onehot_cols(info, deltas, 0)
    for e in range(N_EXPERTS):
        _combine_copy(ys_ref, ybuf_ref.at[slot], sem_ref.at[slot, e], e, 0).wait()
    f_ref[...] = jnp.dot(qm, ybuf_ref[slot], preferred_element_type=f32)

    deepest = jnp.max(jnp.maximum(info[:, 2:3], info[:, 3:4])).astype(i32)
    for p in range(1, N_PASS):
        @pl.when(deepest >= p * CHUNK)
        def _(p=p):
            wins = [_window(base_ref, rows_ref, s, e, p) for e in range(N_EXPERTS)]
            for e in range(N_EXPERTS):
                _combine_copy(ys_ref, yov_ref, sem_ref.at[2, e], e, wins[e][0]).start()
            qp = _onehot_cols(info, [w[1] for w in wins], p)
            for e in range(N_EXPERTS):
                _combine_copy(ys_ref, yov_ref, sem_ref.at[2, e], e, 0).wait()
            f_ref[...] += jnp.dot(qp, yov_ref[...], preferred_element_type=f32)

    o_ref[...] = _ln(ALPHA * h_ref[...] + f_ref[...], g_ref[...], b_ref[...])


def _combine(base, rows, h, info, ys, g, b):
    n_out = N_TILES - 1
    grid_spec = pltpu.PrefetchScalarGridSpec(
        num_scalar_prefetch=2,
        grid=(n_out,),
        in_specs=[
            pl.BlockSpec((TM, D_MODEL), lambda i, base, rows: (i + 1, 0)),
            pl.BlockSpec((TM, LANES), lambda i, base, rows: (i + 1, 0)),
            pl.BlockSpec(memory_space=pl.ANY),
            pl.BlockSpec((1, D_MODEL), lambda i, base, rows: (0, 0)),
            pl.BlockSpec((1, D_MODEL), lambda i, base, rows: (0, 0)),
        ],
        out_specs=pl.BlockSpec((TM, D_MODEL), lambda i, base, rows: (i, 0)),
        scratch_shapes=[
            pltpu.VMEM((2, N_EXPERTS * CHUNK, D_MODEL), bf16),
            pltpu.VMEM((N_EXPERTS * CHUNK, D_MODEL), bf16),
            pltpu.VMEM((TM, D_MODEL), f32),
            pltpu.SemaphoreType.DMA((3, N_EXPERTS)),
        ],
    )
    return pl.pallas_call(
        _combine_kernel,
        grid_spec=grid_spec,
        out_shape=jax.ShapeDtypeStruct((n_out * TM, D_MODEL), f32),
        compiler_params=_cparams("arbitrary"),
        name="moe_combine_ln2",
    )(base, rows, h, info, ys, g, b)


def _split2(x):
    hi = x.astype(bf16)
    lo = (x - hi.astype(f32)).astype(bf16)
    return hi, lo


def kernel(x, meta_tokens, emb_ln_g, emb_ln_b, rel_bias, w_in, conv_dw, conv_b, conv_ln_g, conv_ln_b, sinks, w_attn_proj, w_conv_proj, w_out, ln1_g, ln1_b, ffn_w_gate, ffn_w_up, ffn_w_down, router, moe_w_gate, moe_w_up, moe_w_down, ln2_g, ln2_b):
    row = lambda v: v.reshape(1, -1)
    h = None
    out = None
    bias_real, bias_meta = _bias_tables(rel_bias)
    for i in range(DEPTH):
        if i == 0:
            q, k, v, glu, h = _inproj(i, w_in, x2d=x.reshape(BATCH * SEQ, D_MODEL), meta=meta_tokens,
                                      eg=row(emb_ln_g), eb=row(emb_ln_b))
        else:
            q, k, v, glu = _inproj(i, w_in, h=h_bf16)
        a = _attention(bias_real, bias_meta, sinks[i], q, k, v)
        h = _mix(i, h, a, glu, conv_dw[i], row(conv_b[i]), row(conv_ln_g[i]), row(conv_ln_b[i]),
                 w_in, w_attn_proj, w_conv_proj, w_out, row(ln1_g[i]), row(ln1_b[i]))
        j = i // 2
        if i % 2 == 0:
            h, h_bf16 = _dense_ffn(j, h, ffn_w_gate, ffn_w_up, ffn_w_down, row(ln2_g[i]), row(ln2_b[i]))
        else:
            r_hi, r_lo = _split2(router[j])
            rsplit = jnp.concatenate(
                [r_hi, r_lo, jnp.zeros((D_MODEL, LANES - 2 * N_EXPERTS), bf16)], axis=1)
            info, base, rows, tile_e, tile_blk, tile_valid, xs = _route(h, rsplit)
            ys = _expert_ffn(j, tile_e, tile_blk, tile_valid, xs, moe_w_gate, moe_w_up, moe_w_down)
            out = _combine(base, rows, h, info, ys, row(ln2_g[i]), row(ln2_b[i]))
    return out.reshape(BATCH, SEQ, D_MODEL)
```

```python
import functools
import math

import numpy as np
import jax
import jax.numpy as jnp
from jax import lax
from jax.experimental import pallas as pl
from jax.experimental.pallas import tpu as pltpu

f32 = jnp.float32
bf16 = jnp.bfloat16
i32 = jnp.int32

D_MODEL = 1024
BATCH = 8
SEQ = 2048
DEPTH = 2
N_META = 16
HEAD_DIM = 64
N_Q_HEADS = 8
N_KV_HEADS = 2
GROUP = N_Q_HEADS // N_KV_HEADS
ATTN_WIDTH = N_Q_HEADS * HEAD_DIM
KV_WIDTH = N_KV_HEADS * HEAD_DIM
WINDOW = 128
BLOCK = 128
CONV_CH = D_MODEL // 2
CONV_WIDTH = 31
N_BUCKETS = 32
MAX_DISTANCE = 128
D_FF = 2816
N_EXPERTS = 8
D_EXPERT = 1408
ALPHA = (2 * DEPTH) ** 0.25
LN_EPS = 1e-5
Q_END = ATTN_WIDTH
K_END = Q_END + KV_WIDTH
V_END = K_END + KV_WIDTH
GLU_END = V_END + 2 * CONV_CH

TM = 512
N_META_ROWS = BATCH * N_META
NT = TM + BATCH * SEQ
N_TILES = NT // TM
TILES_PER_BATCH = SEQ // TM
BLOCKS_PER_TILE = TM // BLOCK
KSEG = 3 * BLOCK
CONV_HALO = 32
NEG = -1e30
LANES = 128
SUBLANES = 8

CHUNK = 192
N_PASS = -(-TM // CHUNK)
ALIGN = 2 * SUBLANES
XS_W = D_MODEL + LANES
CAP = 35 * TM
FFN_GRID = 85
MXU_COLS = 256
FF_CHUNKS_DENSE = (256,) * 11
FF_CHUNKS_MOE = (256,) * 5 + (128,)

VMEM_LIMIT = 56 * 1024 * 1024


def _cparams(sem="arbitrary"):
    return pltpu.CompilerParams(dimension_semantics=(sem,), vmem_limit_bytes=VMEM_LIMIT)


def _ln(x, g, b):
    mu = jnp.mean(x, -1, keepdims=True)
    xc = x - mu
    var = jnp.mean(xc * xc, -1, keepdims=True)
    return xc * lax.rsqrt(var + LN_EPS) * g + b


def _full(shape):
    return pl.BlockSpec(shape, lambda *_: (0,) * len(shape))


def _smem():
    return pl.BlockSpec(memory_space=pltpu.SMEM)


def _layer_row(ref, layer):
    return ref[layer:layer + 1, :]


def _embed_tile(s, x_ref, meta_ref, g_ref, b_ref):
    m = _ln(meta_ref[...], g_ref[...], b_ref[...])
    meta_tile = jnp.concatenate([m] * BATCH + [jnp.zeros((TM - N_META_ROWS, D_MODEL), f32)], axis=0)
    return jnp.where(s == 0, meta_tile, _ln(x_ref[...], g_ref[...], b_ref[...]))


def _inproj_kernel(embed, *refs):
    s = pl.program_id(0)
    if embed:
        x_ref, meta_ref, eg_ref, eb_ref, w_ref, q_ref, k_ref, v_ref, glu_ref, h_ref, wb_ref = refs
        h = _embed_tile(s, x_ref, meta_ref, eg_ref, eb_ref)
        h_ref[...] = h
    else:
        hin_ref, w_ref, q_ref, k_ref, v_ref, glu_ref, wb_ref = refs
        h = hin_ref[...]

    @pl.when(s == 0)
    def _():
        wb_ref[...] = w_ref[...].astype(bf16)

    hb = h.astype(bf16)
    col = lambda c0, w: jnp.dot(hb, wb_ref[:, c0:c0 + w], preferred_element_type=f32)
    for c0 in range(0, Q_END, MXU_COLS):
        q_ref[:, c0:c0 + MXU_COLS] = (col(c0, MXU_COLS) * (HEAD_DIM ** -0.5)).astype(bf16)
    kv = col(Q_END, 2 * KV_WIDTH)
    k_ref[...] = kv[:, :KV_WIDTH].astype(bf16)
    v_ref[...] = kv[:, KV_WIDTH:].astype(bf16)
    for c0 in range(0, CONV_CH, MXU_COLS):
        val = col(V_END + c0, MXU_COLS)
        gate = col(V_END + CONV_CH + c0, MXU_COLS)
        glu_ref[:, c0:c0 + MXU_COLS] = (val * jax.nn.sigmoid(gate)).astype(bf16)


def _inproj(layer, w_in, h=None, x2d=None, meta=None, eg=None, eb=None):
    embed = h is None
    row = lambda w: pl.BlockSpec((TM, w), lambda s: (s, 0))
    w_spec = pl.BlockSpec((None, D_MODEL, GLU_END), lambda s: (layer, 0, 0), pipeline_mode=pl.Buffered(1))
    out_specs = [row(ATTN_WIDTH), row(KV_WIDTH), row(KV_WIDTH), row(CONV_CH)]
    out_shape = [
        jax.ShapeDtypeStruct((NT, ATTN_WIDTH), bf16),
        jax.ShapeDtypeStruct((NT, KV_WIDTH), bf16),
        jax.ShapeDtypeStruct((NT, KV_WIDTH), bf16),
        jax.ShapeDtypeStruct((NT, CONV_CH), bf16),
    ]
    if embed:
        in_specs = [
            pl.BlockSpec((TM, D_MODEL), lambda s: (jnp.maximum(s - 1, 0), 0)),
            _full((N_META, D_MODEL)), _full((1, D_MODEL)), _full((1, D_MODEL)), w_spec,
        ]
        args = (x2d, meta, eg, eb, w_in)
        out_specs.append(row(D_MODEL))
        out_shape.append(jax.ShapeDtypeStruct((NT, D_MODEL), f32))
    else:
        in_specs = [row(D_MODEL), w_spec]
        args = (h, w_in)
    return pl.pallas_call(
        functools.partial(_inproj_kernel, embed),
        grid=(N_TILES,),
        in_specs=in_specs,
        out_specs=out_specs,
        out_shape=out_shape,
        scratch_shapes=[pltpu.VMEM((D_MODEL, GLU_END), bf16)],
        compiler_params=_cparams("arbitrary"),
        name="inproj_glu",
    )(*args)


def _rel_bucket_np(dist):
    n = np.maximum(dist, 0)
    max_exact = N_BUCKETS // 2
    nf = np.maximum(n, 1).astype(np.float32)
    scaled = (np.log(nf / np.float32(max_exact)) / np.float32(math.log(MAX_DISTANCE / max_exact))
              * np.float32(N_BUCKETS - max_exact))
    large = np.minimum(max_exact + scaled.astype(np.int32), N_BUCKETS - 1)
    return np.where(n < max_exact, n, large).astype(np.int32)


def _bucket_tables():
    a = np.arange(BLOCK)[:, None]
    m = np.arange(BLOCK)[None, :]
    kk = np.arange(2 * BLOCK)[None, :]
    dist_band = BLOCK + a - kk
    band_ok = (dist_band >= 0) & (dist_band < WINDOW)
    band = np.where(band_ok, _rel_bucket_np(dist_band), -1)
    real = np.full((2, BLOCK, KSEG), -1, np.int32)
    for t, n in enumerate((0, 1)):
        dist_meta = N_META + n * BLOCK + a - m
        real[t, :, :BLOCK] = np.where(m < N_META, _rel_bucket_np(dist_meta), -1)
        real[t, :, BLOCK:] = band
    real[0, :, BLOCK:2 * BLOCK] = -1
    for n in range(2, SEQ // BLOCK):
        dist_meta = N_META + n * BLOCK + a - m
        assert np.array_equal(np.where(m < N_META, _rel_bucket_np(dist_meta), -1), real[1, :, :BLOCK])
    r = np.arange(N_META_ROWS)
    same = (r[:, None] // N_META) == (r[None, :] // N_META)
    dmm = (r[:, None] % N_META) - (r[None, :] % N_META)
    meta = np.where(same & (dmm >= 0), _rel_bucket_np(dmm), -1).astype(np.int32)
    return real, meta


_BK_REAL, _BK_META = _bucket_tables()


def _bias_from_buckets(bk, rb_ref, h):
    acc = jnp.full(bk.shape, NEG, f32)
    for bkt in range(N_BUCKETS):
        acc = jnp.where(bk == bkt, rb_ref[bkt, h], acc)
    return acc


def _expand_kv(x2):
    xf = x2.astype(f32)
    r = pltpu.roll(xf, HEAD_DIM, axis=1)
    low = lax.broadcasted_iota(i32, xf.shape, 1) < HEAD_DIM
    h0 = jnp.where(low, xf, r).astype(bf16)
    h1 = jnp.where(low, r, xf).astype(bf16)
    return (jnp.concatenate([h0, h0], axis=1), jnp.concatenate([h1, h1], axis=1))


def _block_diag(x, n_keys):
    lane_blk = lax.broadcasted_iota(i32, (n_keys, GROUP * HEAD_DIM), 1) // HEAD_DIM
    zero = jnp.zeros_like(x)
    return jnp.concatenate([jnp.where(lane_blk == h, x, zero) for h in range(GROUP)], axis=0)


def _softmax_block(sh, sink):
    mx = jnp.maximum(jnp.max(sh, -1, keepdims=True), sink)
    p = jnp.exp(sh - mx)
    den = jnp.sum(p, -1, keepdims=True) + jnp.exp(sink - mx)
    return p.astype(bf16), 1.0 / den


def _head_scale(rs, m_rows):
    ol = lax.broadcasted_iota(i32, (m_rows, GROUP * HEAD_DIM), 1) // HEAD_DIM
    return jnp.where(ol == 0, rs[0], jnp.where(ol == 1, rs[1], jnp.where(ol == 2, rs[2], rs[3])))


def _attend(qg, kx, vx, bias_of_head, sink_of_head, n_keys):
    kbd = _block_diag(kx, n_keys)
    vbd = _block_diag(vx, n_keys)
    s = lax.dot_general(qg, kbd, (((1,), (1,)), ((), ())), preferred_element_type=f32)
    ps, rs = [], []
    for h in range(GROUP):
        p, r = _softmax_block(s[:, h * n_keys:(h + 1) * n_keys] + bias_of_head(h), sink_of_head(h))
        ps.append(p)
        rs.append(r)
    o = jnp.dot(jnp.concatenate(ps, axis=1), vbd, preferred_element_type=f32)
    return o * _head_scale(rs, qg.shape[0])


def _attend_tile(qg, kbd, vbd, kbd_meta, vbd_meta, bias_of, sink_of_head):
    nt = (((1,), (1,)), ((), ()))
    nb = BLOCKS_PER_TILE
    s_meta = lax.dot_general(qg, kbd_meta, nt, preferred_element_type=f32)
    s_prev, s_cur = [None] * nb, [None] * nb
    for b in range(nb + 1):
        lo, hi = max(b - 1, 0) * BLOCK, min(b + 1, nb) * BLOCK
        sc = lax.dot_general(qg[lo:hi], kbd[b], nt, preferred_element_type=f32)
        if b > 0:
            s_cur[b - 1] = sc[:BLOCK]
        if b < nb:
            s_prev[b] = sc[-BLOCK:]
    pm, pp, pc, scales = [], [], [], []
    for blk in range(nb):
        rows = slice(blk * BLOCK, (blk + 1) * BLOCK)
        pms, pps, pcs, rs = [], [], [], []
        for h in range(GROUP):
            cols = slice(h * BLOCK, (h + 1) * BLOCK)
            sh = jnp.concatenate([s_meta[rows, cols], s_prev[blk][:, cols], s_cur[blk][:, cols]], axis=1)
            p, r = _softmax_block(sh + bias_of(blk, h), sink_of_head(h))
            pms.append(p[:, :BLOCK])
            pps.append(p[:, BLOCK:2 * BLOCK])
            pcs.append(p[:, 2 * BLOCK:])
            rs.append(r)
        pm.append(jnp.concatenate(pms, axis=1))
        pp.append(jnp.concatenate(pps, axis=1))
        pc.append(jnp.concatenate(pcs, axis=1))
        scales.append(_head_scale(rs, BLOCK))
    o = [None] * nb
    for b in range(nb + 1):
        parts = ([pc[b - 1]] if b > 0 else []) + ([pp[b]] if b < nb else [])
        ob = jnp.dot(jnp.concatenate(parts, axis=0), vbd[b], preferred_element_type=f32)
        if b > 0:
            o[b - 1] = o[b - 1] + ob[:BLOCK]
        if b < nb:
            o[b] = ob[-BLOCK:]
    om = jnp.dot(jnp.concatenate(pm, axis=0), vbd_meta, preferred_element_type=f32)
    return (jnp.concatenate(o, axis=0) + om) * jnp.concatenate(scales, axis=0)


def _bias_kernel(rb_ref, bkr_ref, bkm_ref, br_ref, bm_ref):
    for h in range(N_Q_HEADS):
        for t in range(2):
            br_ref[t, h] = _bias_from_buckets(bkr_ref[t], rb_ref, h)
        bm_ref[h] = _bias_from_buckets(bkm_ref[...], rb_ref, h)


def _bias_tables(rel_bias):
    return pl.pallas_call(
        _bias_kernel,
        in_specs=[_smem(), pl.BlockSpec(memory_space=pltpu.VMEM), pl.BlockSpec(memory_space=pltpu.VMEM)],
        out_specs=[pl.BlockSpec(memory_space=pltpu.VMEM), pl.BlockSpec(memory_space=pltpu.VMEM)],
        out_shape=[
            jax.ShapeDtypeStruct((2, N_Q_HEADS, BLOCK, KSEG), f32),
            jax.ShapeDtypeStruct((N_Q_HEADS, N_META_ROWS, N_META_ROWS), f32),
        ],
        name="rel_bias_tables",
    )(rel_bias, jnp.asarray(_BK_REAL), jnp.asarray(_BK_META))


def _attn_kernel(layer, sinks_ref, q_ref, k_ref, v_ref, kp_ref, vp_ref, km_ref, vm_ref,
                 br_ref, bm_ref, o_ref):
    s = pl.program_id(0)
    gw = GROUP * HEAD_DIM
    sink = lambda head: sinks_ref[layer, head]

    @pl.when(s == 0)
    def _():
        kxs = _expand_kv(k_ref[:N_META_ROWS, :])
        vxs = _expand_kv(v_ref[:N_META_ROWS, :])
        for g in range(N_KV_HEADS):
            o = _attend(q_ref[:N_META_ROWS, g * gw:(g + 1) * gw], kxs[g], vxs[g],
                        lambda h, g=g: bm_ref[g * GROUP + h],
                        lambda h, g=g: sink(g * GROUP + h), N_META_ROWS)
            o_ref[:N_META_ROWS, g * gw:(g + 1) * gw] = o.astype(bf16)
        o_ref[N_META_ROWS:, :] = jnp.zeros((TM - N_META_ROWS, ATTN_WIDTH), bf16)

    @pl.when(s > 0)
    def _():
        first = ((s - 1) % TILES_PER_BATCH) == 0
        tbl0 = jnp.where(first, 0, 1)
        zpad = jnp.zeros((BLOCK - N_META, KV_WIDTH), bf16)
        kms = _expand_kv(jnp.concatenate([km_ref[...], zpad], axis=0))
        vms = _expand_kv(jnp.concatenate([vm_ref[...], zpad], axis=0))
        kxs = [_expand_kv(kp_ref[...])] + [_expand_kv(k_ref[b * BLOCK:(b + 1) * BLOCK, :])
                                          for b in range(BLOCKS_PER_TILE)]
        vxs = [_expand_kv(vp_ref[...])] + [_expand_kv(v_ref[b * BLOCK:(b + 1) * BLOCK, :])
                                          for b in range(BLOCKS_PER_TILE)]
        for g in range(N_KV_HEADS):
            o = _attend_tile(
                q_ref[:, g * gw:(g + 1) * gw],
                [_block_diag(x[g], BLOCK) for x in kxs], [_block_diag(x[g], BLOCK) for x in vxs],
                _block_diag(kms[g], BLOCK), _block_diag(vms[g], BLOCK),
                lambda blk, h, g=g: br_ref[tbl0 if blk == 0 else 1, g * GROUP + h],
                lambda h, g=g: sink(g * GROUP + h))
            o_ref[:, g * gw:(g + 1) * gw] = o.astype(bf16)


def _attention(layer, bias_real, bias_meta, sinks, q, k, v):
    row = lambda w: pl.BlockSpec((TM, w), lambda s: (s, 0))
    prev = pl.BlockSpec((BLOCK, KV_WIDTH), lambda s: (jnp.maximum(s * BLOCKS_PER_TILE - 1, 0), 0))
    meta = pl.BlockSpec((N_META, KV_WIDTH), lambda s: (jnp.maximum(s - 1, 0) // TILES_PER_BATCH, 0))
    return pl.pallas_call(
        functools.partial(_attn_kernel, layer),
        grid=(N_TILES,),
        in_specs=[
            _smem(),
            row(ATTN_WIDTH), row(KV_WIDTH), row(KV_WIDTH),
            prev, prev, meta, meta,
            _full((2, N_Q_HEADS, BLOCK, KSEG)), _full((N_Q_HEADS, N_META_ROWS, N_META_ROWS)),
        ],
        out_specs=row(ATTN_WIDTH),
        out_shape=jax.ShapeDtypeStruct((NT, ATTN_WIDTH), bf16),
        compiler_params=_cparams("arbitrary"),
        name="swa_attention",
    )(sinks, q, k, v, k, v, k, v, bias_real, bias_meta)


SPREAD = TM // BATCH
SH_ROWS = CONV_HALO + TM - SUBLANES
CONV_ROWS = 128
GATE_BLOCKS = 2 * D_MODEL // MXU_COLS


def _conv_window(s, glu_ref, gprev_ref, gmeta_ref, win_ref, sh_ref):
    is_meta = s == 0
    first = ((s - 1) % TILES_PER_BATCH) == 0
    glu = glu_ref[...].astype(f32)
    zgap = jnp.zeros((SPREAD - N_META, CONV_CH), f32)
    spread = jnp.concatenate(
        [p for bb in range(BATCH) for p in (zgap, glu[bb * N_META:(bb + 1) * N_META])], axis=0)
    win_ref[CONV_HALO:, :] = jnp.where(is_meta, spread, glu)
    halo_first = jnp.concatenate(
        [jnp.zeros((CONV_HALO - N_META, CONV_CH), f32), gmeta_ref[...].astype(f32)], axis=0)
    halo = jnp.where(first, halo_first, gprev_ref[...].astype(f32))
    win_ref[:CONV_HALO, :] = jnp.where(is_meta, 0.0, halo)

    for r in range(1, SUBLANES):
        sh_ref[r - 1] = win_ref[r:r + SH_ROWS, :]


def _conv_finish(s, layer, dw_ref, cb_ref, cg_ref, cbeta_ref, win_ref, sh_ref, cv_ref):
    off = CONV_HALO - (CONV_WIDTH - 1)
    for rc in range(TM // CONV_ROWS):
        for lc in range(CONV_CH // LANES):
            ls = slice(lc * LANES, (lc + 1) * LANES)
            acc = None
            for t in range(CONV_WIDTH):
                r = (off + t) % SUBLANES
                r0 = rc * CONV_ROWS + (off + t) - r
                x = win_ref[r0:r0 + CONV_ROWS, ls] if r == 0 else sh_ref[r - 1, r0:r0 + CONV_ROWS, ls]
                term = dw_ref[t:t + 1, ls] * x
                acc = term if acc is None else acc + term
            cv_ref[rc * CONV_ROWS:(rc + 1) * CONV_ROWS, ls] = acc
    y = _ln(cv_ref[...] + _layer_row(cb_ref, layer), _layer_row(cg_ref, layer), _layer_row(cbeta_ref, layer))
    c = y * jax.nn.sigmoid(y)
    gathered = jnp.concatenate(
        [c[bb * SPREAD + SPREAD - N_META:(bb + 1) * SPREAD] for bb in range(BATCH)]
        + [jnp.zeros((TM - N_META_ROWS, CONV_CH), f32)], axis=0)
    return jnp.where(s == 0, gathered, c).astype(bf16)


def _mix_kernel(layer, h_ref, a_ref, glu_ref, gprev_ref, gmeta_ref, dw_ref, cb_ref, cg_ref, cbeta_ref, *rest):
    wg_refs = rest[:GATE_BLOCKS]
    (wap_ref, wcp_ref, wo_ref, g_ref, b_ref, o_ref,
     win_ref, sh_ref, cv_ref, wgb_ref, wapb_ref, wcpb_ref, wob_ref) = rest[GATE_BLOCKS:]
    s = pl.program_id(0)

    @pl.when(s == 0)
    def _():
        for k in range(GATE_BLOCKS):
            wgb_ref[:, k * MXU_COLS:(k + 1) * MXU_COLS] = wg_refs[k][...].astype(bf16)
        wapb_ref[...] = wap_ref[...].astype(bf16)
        wcpb_ref[...] = wcp_ref[...].astype(bf16)
        wob_ref[...] = wo_ref[...].astype(bf16)

    h = h_ref[...]
    gates = jnp.dot(h.astype(bf16), wgb_ref[...], preferred_element_type=f32)
    ya = jnp.dot(a_ref[...], wapb_ref[...], preferred_element_type=f32)
    _conv_window(s, glu_ref, gprev_ref, gmeta_ref, win_ref, sh_ref)
    c = _conv_finish(s, layer, dw_ref, cb_ref, cg_ref, cbeta_ref, win_ref, sh_ref, cv_ref)
    yc = jnp.dot(c, wcpb_ref[...], preferred_element_type=f32)
    m = jax.nn.sigmoid(gates[:, :D_MODEL]) * ya + jax.nn.sigmoid(gates[:, D_MODEL:]) * yc
    mix = jnp.dot(m.astype(bf16), wob_ref[...], preferred_element_type=f32)
    o_ref[...] = _ln(ALPHA * h + mix, _layer_row(g_ref, layer), _layer_row(b_ref, layer))


def _mix(layer, h, a, glu, dw, cb, cg, cbeta, w_in, wap, wcp, wo, g, b):
    row = lambda w: pl.BlockSpec((TM, w), lambda s: (s, 0))
    gate_block = lambda k: pl.BlockSpec((None, D_MODEL, MXU_COLS),
                                        lambda s: (layer, 0, GLU_END // MXU_COLS + k),
                                        pipeline_mode=pl.Buffered(1))
    gprev = pl.BlockSpec((CONV_HALO, CONV_CH),
                         lambda s: (jnp.maximum(s * (TM // CONV_HALO) - 1, 0), 0))
    gmeta = pl.BlockSpec((N_META, CONV_CH), lambda s: (jnp.maximum(s - 1, 0) // TILES_PER_BATCH, 0))
    per_layer = lambda r, c: pl.BlockSpec((None, r, c), lambda s: (layer, 0, 0), pipeline_mode=pl.Buffered(1))
    conv_vec = _full((DEPTH, CONV_CH))
    model_vec = _full((DEPTH, D_MODEL))
    return pl.pallas_call(
        functools.partial(_mix_kernel, layer),
        grid=(N_TILES,),
        in_specs=[
            row(D_MODEL), row(ATTN_WIDTH), row(CONV_CH), gprev, gmeta,
            pl.BlockSpec((None, CONV_WIDTH, CONV_CH), lambda s: (layer, 0, 0)), conv_vec, conv_vec, conv_vec,
            *[gate_block(k) for k in range(GATE_BLOCKS)], per_layer(ATTN_WIDTH, D_MODEL),
            per_layer(CONV_CH, D_MODEL), per_layer(D_MODEL, D_MODEL),
            model_vec, model_vec,
        ],
        out_specs=row(D_MODEL),
        out_shape=jax.ShapeDtypeStruct((NT, D_MODEL), f32),
        scratch_shapes=[
            pltpu.VMEM((CONV_HALO + TM, CONV_CH), f32),
            pltpu.VMEM((7, SH_ROWS, CONV_CH), f32),
            pltpu.VMEM((TM, CONV_CH), f32),
            pltpu.VMEM((D_MODEL, 2 * D_MODEL), bf16),
            pltpu.VMEM((ATTN_WIDTH, D_MODEL), bf16),
            pltpu.VMEM((CONV_CH, D_MODEL), bf16),
            pltpu.VMEM((D_MODEL, D_MODEL), bf16),
        ],
        compiler_params=_cparams("arbitrary"),
        name="conv_mix_ln1",
    )(h, a, glu, glu, glu, dw, cb, cg, cbeta, *([w_in] * GATE_BLOCKS), wap, wcp, wo, g, b)


def _swiglu_chunks(xb, wg_ref, wu_ref, wd_ref, chunks, row_scale=None):
    acc = None
    c0 = 0
    for w in chunks:
        gt = jnp.dot(xb, wg_ref[:, c0:c0 + w].astype(bf16), preferred_element_type=f32)
        up = jnp.dot(xb, wu_ref[:, c0:c0 + w].astype(bf16), preferred_element_type=f32)
        act = gt * jax.nn.sigmoid(gt) * up
        if row_scale is not None:
            act = act * row_scale
        part = jnp.dot(act.astype(bf16), wd_ref[c0:c0 + w, :].astype(bf16), preferred_element_type=f32)
        acc = part if acc is None else acc + part
        c0 += w
    return acc


def _ffn_kernel(layer, h_ref, wg_ref, wu_ref, wd_ref, g_ref, b_ref, o_ref):
    h = h_ref[...]
    f = _swiglu_chunks(h.astype(bf16), wg_ref, wu_ref, wd_ref, FF_CHUNKS_DENSE)
    o_ref[...] = _ln(ALPHA * h + f, _layer_row(g_ref, layer), _layer_row(b_ref, layer))


def _dense_ffn(layer, j, h, wg, wu, wd, g, b):
    row = pl.BlockSpec((TM, D_MODEL), lambda s: (s, 0))
    per_layer = lambda r, c: pl.BlockSpec((None, r, c), lambda s: (j, 0, 0), pipeline_mode=pl.Buffered(1))
    return pl.pallas_call(
        functools.partial(_ffn_kernel, layer),
        grid=(N_TILES,),
        in_specs=[row, per_layer(D_MODEL, D_FF), per_layer(D_MODEL, D_FF), per_layer(D_FF, D_MODEL),
                  _full((DEPTH, D_MODEL)), _full((DEPTH, D_MODEL))],
        out_specs=row,
        out_shape=jax.ShapeDtypeStruct((NT, D_MODEL), f32),
        compiler_params=_cparams("parallel"),
        name="dense_ffn_ln2",
    )(h, wg, wu, wd, g, b)


def _route_copy(src_ref, xs_ref, sem, e, row):
    return pltpu.make_async_copy(
        src_ref.at[pl.ds(e * CHUNK, CHUNK), :],
        xs_ref.at[pl.ds(pl.multiple_of(e * CAP + row, ALIGN), CHUNK), :],
        sem)


def _zero_copy(zbuf_ref, xs_ref, sem, e, row):
    return pltpu.make_async_copy(
        zbuf_ref,
        xs_ref.at[pl.ds(pl.multiple_of(e * CAP + row, ALIGN), TM), :],
        sem)


def _dispatch_pass(p, ranki, sel, gate, hi, dst_ref):
    slot_id = lax.broadcasted_iota(i32, (CHUNK, TM), 0) + p * CHUNK
    lane = lax.broadcasted_iota(i32, (CHUNK, LANES), 1)
    ps = []
    for e in range(N_EXPERTS):
        pe = (slot_id == ranki[e:e + 1, :]) & sel[e:e + 1, :]
        ps.append(jnp.where(pe, 1.0, 0.0).astype(bf16))
        gs = jnp.sum(jnp.where(pe, gate[e:e + 1, :], 0.0), axis=1, keepdims=True)
        g_hi = gs.astype(bf16).astype(f32)
        g_mid = (gs - g_hi).astype(bf16).astype(f32)
        g_lo = gs - g_hi - g_mid
        parts = jnp.where(lane == 0, g_hi, jnp.where(lane == 1, g_mid, jnp.where(lane == 2, g_lo, 0.0)))
        dst_ref[e * CHUNK:(e + 1) * CHUNK, D_MODEL:] = parts.astype(bf16)
    pall = jnp.concatenate(ps, axis=0)
    dst_ref[:, :D_MODEL] = jnp.dot(pall, hi, preferred_element_type=f32).astype(bf16)


def _route_kernel(h_ref, r_ref, info_ref, base_ref, rows_ref, te_ref, tb_ref, tv_ref, xs_ref,
                  xbuf_ref, xov_ref, zbuf_ref, cnt_ref, sem_ref):
    s = pl.program_id(0)
    slot = s % 2

    @pl.when(s == 0)
    def _():
        for e in range(N_EXPERTS):
            cnt_ref[e] = 0
        zbuf_ref[...] = jnp.zeros((TM, XS_W), bf16)

    h = h_ref[...]
    hi = h.astype(bf16)
    lo = (h - hi.astype(f32)).astype(bf16)
    pa = jnp.dot(hi, r_ref[...], preferred_element_type=f32)
    pb = jnp.dot(lo, r_ref[...], preferred_element_type=f32)
    logits = pa + pltpu.roll(pa, LANES - N_EXPERTS, axis=1) + pb
    lt = logits.T[:N_EXPERTS, :]

    eid = lax.broadcasted_iota(i32, (N_EXPERTS, TM), 0)
    m1 = jnp.max(lt, axis=0, keepdims=True)
    i1 = jnp.min(jnp.where(lt == m1, eid, N_EXPERTS), axis=0, keepdims=True)
    lt2 = jnp.where(eid == i1, NEG, lt)
    m2 = jnp.max(lt2, axis=0, keepdims=True)
    i2 = jnp.min(jnp.where(lt2 == m2, eid, N_EXPERTS), axis=0, keepdims=True)
    ex = jnp.exp(m2 - m1)
    g1 = 1.0 / (1.0 + ex)
    g2 = ex * g1
    tok = lax.broadcasted_iota(i32, (1, TM), 1)
    valid = (s > 0) | (tok < N_META_ROWS)
    sel1 = (eid == i1) & valid
    sel2 = (eid == i2) & valid
    sel = sel1 | sel2
    onehot = jnp.where(sel, 1.0, 0.0)
    tri = jnp.where(lax.broadcasted_iota(i32, (TM, TM), 0) < lax.broadcasted_iota(i32, (TM, TM), 1), 1.0, 0.0)
    rank = jnp.dot(onehot, tri, preferred_element_type=f32)
    gate = jnp.where(sel1, g1, jnp.where(sel2, g2, 0.0))
    rank1 = jnp.sum(jnp.where(sel1, rank, 0.0), axis=0, keepdims=True)
    rank2 = jnp.sum(jnp.where(sel2, rank, 0.0), axis=0, keepdims=True)
    info = jnp.concatenate([i1.astype(f32), i2.astype(f32), rank1, rank2,
                            jnp.zeros((LANES - 4, TM), f32)], axis=0)
    info_ref[...] = info.T
    ranki = rank.astype(i32)

    _dispatch_pass(0, ranki, sel, gate, hi, xbuf_ref.at[slot])

    @pl.when(s > 0)
    def _():
        for e in range(N_EXPERTS):
            _route_copy(xbuf_ref.at[1 - slot], xs_ref, sem_ref.at[1 - slot, e], e, 0).wait()

    bases, counts = [], []
    for e in range(N_EXPERTS):
        base = cnt_ref[e]
        n_e = jnp.sum(onehot[e:e + 1, :]).astype(i32)
        bases.append(base)
        counts.append(n_e)
        base_ref[s, e] = base
        _route_copy(xbuf_ref.at[slot], xs_ref, sem_ref.at[slot, e], e, base).start()
        cnt_ref[e] = base + ((n_e + (ALIGN - 1)) // ALIGN) * ALIGN
    most = functools.reduce(jnp.maximum, counts)

    for p in range(1, N_PASS):
        @pl.when(most > p * CHUNK)
        def _(p=p):
            _dispatch_pass(p, ranki, sel, gate, hi, xov_ref)
            for e in range(N_EXPERTS):
                _route_copy(xov_ref, xs_ref, sem_ref.at[2, e], e, bases[e] + p * CHUNK).start()
            for e in range(N_EXPERTS):
                _route_copy(xov_ref, xs_ref, sem_ref.at[2, e], e, 0).wait()

    @pl.when(s == N_TILES - 1)
    def _():
        for e in range(N_EXPERTS):
            _route_copy(xbuf_ref.at[slot], xs_ref, sem_ref.at[slot, e], e, 0).wait()
        used = [cnt_ref[e] for e in range(N_EXPERTS)]
        for e in range(N_EXPERTS):
            _zero_copy(zbuf_ref, xs_ref, sem_ref.at[2, e], e, used[e]).start()
        starts, ends, total = [], [], 0
        for e in range(N_EXPERTS):
            n_t = jnp.maximum(lax.shift_right_logical(used[e] + (TM - 1), TM.bit_length() - 1), 1)
            rows_ref[e] = n_t * TM
            starts.append(total)
            total = total + n_t
            ends.append(total)

        def tile_entry(t):
            e_t = sum(jnp.where(t >= ends[e], 1, 0) for e in range(N_EXPERTS - 1))
            first = sum(jnp.where(e_t == e, starts[e], 0) for e in range(N_EXPERTS))
            return e_t, e_t * (CAP // TM) + (t - first)

        e_last, blk_last = tile_entry(total - 1)
        for t in range(FFN_GRID):
            e_t, blk_t = tile_entry(t)
            live = t < total
            te_ref[t] = jnp.where(live, e_t, e_last)
            tb_ref[t] = jnp.where(live, blk_t, blk_last)
            tv_ref[t] = jnp.where(live, 1, 0)
        for e in range(N_EXPERTS):
            _zero_copy(zbuf_ref, xs_ref, sem_ref.at[2, e], e, 0).wait()


def _route(h, rsplit):
    return pl.pallas_call(
        _route_kernel,
        grid=(N_TILES,),
        in_specs=[pl.BlockSpec((TM, D_MODEL), lambda s: (s, 0)), _full((D_MODEL, LANES))],
        out_specs=[
            pl.BlockSpec((TM, LANES), lambda s: (s, 0)),
            _smem(), _smem(), _smem(), _smem(), _smem(),
            pl.BlockSpec(memory_space=pl.ANY),
        ],
        out_shape=[
            jax.ShapeDtypeStruct((NT, LANES), f32),
            jax.ShapeDtypeStruct((N_TILES, N_EXPERTS), i32),
            jax.ShapeDtypeStruct((N_EXPERTS,), i32),
            jax.ShapeDtypeStruct((FFN_GRID,), i32),
            jax.ShapeDtypeStruct((FFN_GRID,), i32),
            jax.ShapeDtypeStruct((FFN_GRID,), i32),
            jax.ShapeDtypeStruct((N_EXPERTS * CAP, XS_W), bf16),
        ],
        scratch_shapes=[
            pltpu.VMEM((2, N_EXPERTS * CHUNK, XS_W), bf16),
            pltpu.VMEM((N_EXPERTS * CHUNK, XS_W), bf16),
            pltpu.VMEM((TM, XS_W), bf16),
            pltpu.SMEM((N_EXPERTS,), i32),
            pltpu.SemaphoreType.DMA((3, N_EXPERTS)),
        ],
        compiler_params=_cparams("arbitrary"),
        name="moe_route_dispatch",
    )(h, rsplit)


def _expert_kernel(te_ref, tb_ref, tv_ref, x_ref, wg_ref, wu_ref, wd_ref, y_ref):
    t = pl.program_id(0)

    @pl.when(tv_ref[t] > 0)
    def _():
        x = x_ref[...]
        gate = jnp.sum(x[:, D_MODEL:].astype(f32), axis=1, keepdims=True)
        y = _swiglu_chunks(x[:, :D_MODEL], wg_ref, wu_ref, wd_ref, FF_CHUNKS_MOE, row_scale=gate)
        y_ref[...] = y.astype(bf16)


def _expert_ffn(j, tile_e, tile_blk, tile_valid, xs, wg, wu, wd):
    per_expert = lambda r, c: pl.BlockSpec((None, None, r, c), lambda t, te, tb, tv: (j, te[t], 0, 0))
    grid_spec = pltpu.PrefetchScalarGridSpec(
        num_scalar_prefetch=3,
        grid=(FFN_GRID,),
        in_specs=[
            pl.BlockSpec((TM, XS_W), lambda t, te, tb, tv: (tb[t], 0)),
            per_expert(D_MODEL, D_EXPERT), per_expert(D_MODEL, D_EXPERT), per_expert(D_EXPERT, D_MODEL),
        ],
        out_specs=pl.BlockSpec((TM, D_MODEL), lambda t, te, tb, tv: (tb[t], 0)),
    )
    return pl.pallas_call(
        _expert_kernel,
        grid_spec=grid_spec,
        out_shape=jax.ShapeDtypeStruct((N_EXPERTS * CAP, D_MODEL), bf16),
        compiler_params=_cparams("arbitrary"),
        name="moe_expert_ffn",
    )(tile_e, tile_blk, tile_valid, xs, wg, wu, wd)


def _combine_copy(ys_ref, dst_ref, sem, e, row):
    return pltpu.make_async_copy(
        ys_ref.at[pl.ds(pl.multiple_of(e * CAP + row, ALIGN), CHUNK), :],
        dst_ref.at[pl.ds(e * CHUNK, CHUNK), :],
        sem)


def _window(base_ref, rows_ref, s, e, p):
    want = base_ref[s, e] + p * CHUNK
    start = jnp.minimum(want, rows_ref[e] - CHUNK)
    return start, want - start


def _onehot_cols(info, deltas, p):
    col = lax.broadcasted_iota(i32, (TM, N_EXPERTS * CHUNK), 1)
    hit = None
    for k in range(2):
        ek = info[:, k:k + 1].astype(i32)
        rk = info[:, 2 + k:3 + k].astype(i32) - p * CHUNK
        dk = functools.reduce(lambda a, b: a + b,
                              [jnp.where(ek == e, deltas[e], 0) for e in range(N_EXPERTS)])
        ck = jnp.where((rk >= 0) & (rk < CHUNK), ek * CHUNK + rk + dk, -1)
        hk = col == ck
        hit = hk if hit is None else hit | hk
    return jnp.where(hit, 1.0, 0.0).astype(bf16)


def _combine_kernel(layer, base_ref, rows_ref, h_ref, info_ref, ys_ref, g_ref, b_ref, o_ref,
                    ybuf_ref, yov_ref, f_ref, sem_ref):
    i = pl.program_id(0)
    n = pl.num_programs(0)
    s = i + 1
    slot = i % 2

    def start_fetch(tile, to_slot):
        for e in range(N_EXPERTS):
            start, _ = _window(base_ref, rows_ref, tile, e, 0)
            _combine_copy(ys_ref, ybuf_ref.at[to_slot], sem_ref.at[to_slot, e], e, start).start()

    @pl.when(i == 0)
    def _():
        start_fetch(s, slot)

    @pl.when(i + 1 < n)
    def _():
        start_fetch(s + 1, 1 - slot)

    info = info_ref[...]
    deltas = [_window(base_ref, rows_ref, s, e, 0)[1] for e in range(N_EXPERTS)]
    qm = _onehot_cols(info, deltas, 0)
    for e in range(N_EXPERTS):
        _combine_copy(ys_ref, ybuf_ref.at[slot], sem_ref.at[slot, e], e, 0).wait()
    f_ref[...] = jnp.dot(qm, ybuf_ref[slot], preferred_element_type=f32)

    deepest = jnp.max(jnp.maximum(info[:, 2:3], info[:, 3:4])).astype(i32)
    for p in range(1, N_PASS):
        @pl.when(deepest >= p * CHUNK)
        def _(p=p):
            wins = [_window(base_ref, rows_ref, s, e, p) for e in range(N_EXPERTS)]
            for e in range(N_EXPERTS):
                _combine_copy(ys_ref, yov_ref, sem_ref.at[2, e], e, wins[e][0]).start()
            qp = _onehot_cols(info, [w[1] for w in wins], p)
            for e in range(N_EXPERTS):
                _combine_copy(ys_ref, yov_ref, sem_ref.at[2, e], e, 0).wait()
            f_ref[...] += jnp.dot(qp, yov_ref[...], preferred_element_type=f32)

    o_ref[...] = _ln(ALPHA * h_ref[...] + f_ref[...], _layer_row(g_ref, layer), _layer_row(b_ref, layer))


def _combine(layer, base, rows, h, info, ys, g, b):
    n_out = N_TILES - 1
    grid_spec = pltpu.PrefetchScalarGridSpec(
        num_scalar_prefetch=2,
        grid=(n_out,),
        in_specs=[
            pl.BlockSpec((TM, D_MODEL), lambda i, base, rows: (i + 1, 0)),
            pl.BlockSpec((TM, LANES), lambda i, base, rows: (i + 1, 0)),
            pl.BlockSpec(memory_space=pl.ANY),
            pl.BlockSpec((DEPTH, D_MODEL), lambda i, base, rows: (0, 0)),
            pl.BlockSpec((DEPTH, D_MODEL), lambda i, base, rows: (0, 0)),
        ],
        out_specs=pl.BlockSpec((TM, D_MODEL), lambda i, base, rows: (i, 0)),
        scratch_shapes=[
            pltpu.VMEM((2, N_EXPERTS * CHUNK, D_MODEL), bf16),
            pltpu.VMEM((N_EXPERTS * CHUNK, D_MODEL), bf16),
            pltpu.VMEM((TM, D_MODEL), f32),
            pltpu.SemaphoreType.DMA((3, N_EXPERTS)),
        ],
    )
    return pl.pallas_call(
        functools.partial(_combine_kernel, layer),
        grid_spec=grid_spec,
        out_shape=jax.ShapeDtypeStruct((n_out * TM, D_MODEL), f32),
        compiler_params=_cparams("arbitrary"),
        name="moe_combine_ln2",
    )(base, rows, h, info, ys, g, b)


def _split2(x):
    hi = x.astype(bf16)
    lo = (x - hi.astype(f32)).astype(bf16)
    return hi, lo


def kernel(x, meta_tokens, emb_ln_g, emb_ln_b, rel_bias, w_in, conv_dw, conv_b, conv_ln_g, conv_ln_b, sinks, w_attn_proj, w_conv_proj, w_out, ln1_g, ln1_b, ffn_w_gate, ffn_w_up, ffn_w_down, router, moe_w_gate, moe_w_up, moe_w_down, ln2_g, ln2_b):
    row = lambda v: v.reshape(1, -1)
    h = None
    out = None
    bias_real, bias_meta = _bias_tables(rel_bias)
    for i in range(DEPTH):
        if i == 0:
            q, k, v, glu, h = _inproj(i, w_in, x2d=x.reshape(BATCH * SEQ, D_MODEL), meta=meta_tokens,
                                      eg=row(emb_ln_g), eb=row(emb_ln_b))
        else:
            q, k, v, glu = _inproj(i, w_in, h=h)
        a = _attention(i, bias_real, bias_meta, sinks, q, k, v)
        h = _mix(i, h, a, glu, conv_dw, conv_b, conv_ln_g, conv_ln_b,
                 w_in, w_attn_proj, w_conv_proj, w_out, ln1_g, ln1_b)
        j = i // 2
        if i % 2 == 0:
            h = _dense_ffn(i, j, h, ffn_w_gate, ffn_w_up, ffn_w_down, ln2_g, ln2_b)
        else:
            r_hi, r_lo = _split2(router[j])
            rsplit = jnp.concatenate(
                [r_hi, r_lo, jnp.zeros((D_MODEL, LANES - 2 * N_EXPERTS), bf16)], axis=1)
            info, base, rows, tile_e, tile_blk, tile_valid, xs = _route(h, rsplit)
            ys = _expert_ffn(j, tile_e, tile_blk, tile_valid, xs, moe_w_gate, moe_w_up, moe_w_down)
            out = _combine(i, base, rows, h, info, ys, ln2_g, ln2_b)
    return out.reshape(BATCH, SEQ, D_MODEL)
```

```python
import functools
import math

import numpy as np
import jax
import jax.numpy as jnp
from jax import lax
from jax.experimental import pallas as pl
from jax.experimental.pallas import tpu as pltpu

f32 = jnp.float32
bf16 = jnp.bfloat16
i32 = jnp.int32

D_MODEL = 1024
BATCH = 8
SEQ = 2048
DEPTH = 2
N_META = 16
HEAD_DIM = 64
N_Q_HEADS = 8
N_KV_HEADS = 2
GROUP = N_Q_HEADS // N_KV_HEADS
ATTN_WIDTH = N_Q_HEADS * HEAD_DIM
KV_WIDTH = N_KV_HEADS * HEAD_DIM
WINDOW = 128
BLOCK = 128
CONV_CH = D_MODEL // 2
CONV_WIDTH = 31
N_BUCKETS = 32
MAX_DISTANCE = 128
D_FF = 2816
N_EXPERTS = 8
D_EXPERT = 1408
ALPHA = (2 * DEPTH) ** 0.25
LN_EPS = 1e-5
Q_END = ATTN_WIDTH
K_END = Q_END + KV_WIDTH
V_END = K_END + KV_WIDTH
GLU_END = V_END + 2 * CONV_CH

TM = 512
N_META_ROWS = BATCH * N_META
NT = TM + BATCH * SEQ
N_TILES = NT // TM
TILES_PER_BATCH = SEQ // TM
BLOCKS_PER_TILE = TM // BLOCK
KSEG = 3 * BLOCK
CONV_HALO = 32
NEG = -1e30
LANES = 128
SUBLANES = 8

CHUNK = 192
N_PASS = -(-TM // CHUNK)
ALIGN = 2 * SUBLANES
XS_W = D_MODEL + LANES
CAP = 35 * TM
FFN_GRID = 85
TILE_SKIP, TILE_FULL, TILE_HALF = 0, 1, 2
MXU_COLS = 256
FF_CHUNKS_DENSE = (256,) * 11
FF_CHUNKS_MOE = (256,) * 5 + (128,)

VMEM_LIMIT = 56 * 1024 * 1024


def _cparams(sem="arbitrary"):
    return pltpu.CompilerParams(dimension_semantics=(sem,), vmem_limit_bytes=VMEM_LIMIT)


def _ln(x, g, b):
    mu = jnp.mean(x, -1, keepdims=True)
    xc = x - mu
    var = jnp.mean(xc * xc, -1, keepdims=True)
    return xc * lax.rsqrt(var + LN_EPS) * g + b


def _full(shape):
    return pl.BlockSpec(shape, lambda *_: (0,) * len(shape))


def _smem():
    return pl.BlockSpec(memory_space=pltpu.SMEM)


def _layer_row(ref, layer):
    return ref[layer:layer + 1, :]


def _embed_tile(s, x_ref, meta_ref, g_ref, b_ref):
    m = _ln(meta_ref[...], g_ref[...], b_ref[...])
    meta_tile = jnp.concatenate([m] * BATCH + [jnp.zeros((TM - N_META_ROWS, D_MODEL), f32)], axis=0)
    return jnp.where(s == 0, meta_tile, _ln(x_ref[...], g_ref[...], b_ref[...]))


def _inproj_kernel(embed, *refs):
    s = pl.program_id(0)
    if embed:
        x_ref, meta_ref, eg_ref, eb_ref, w_ref, q_ref, k_ref, v_ref, glu_ref, h_ref, wb_ref = refs
        h = _embed_tile(s, x_ref, meta_ref, eg_ref, eb_ref)
        h_ref[...] = h
    else:
        hin_ref, w_ref, q_ref, k_ref, v_ref, glu_ref, wb_ref = refs
        h = hin_ref[...]

    @pl.when(s == 0)
    def _():
        wb_ref[...] = w_ref[...].astype(bf16)

    hb = h.astype(bf16)
    col = lambda c0, w: jnp.dot(hb, wb_ref[:, c0:c0 + w], preferred_element_type=f32)
    for c0 in range(0, Q_END, MXU_COLS):
        q_ref[:, c0:c0 + MXU_COLS] = (col(c0, MXU_COLS) * (HEAD_DIM ** -0.5)).astype(bf16)
    kv = col(Q_END, 2 * KV_WIDTH)
    k_ref[...] = kv[:, :KV_WIDTH].astype(bf16)
    v_ref[...] = kv[:, KV_WIDTH:].astype(bf16)
    for c0 in range(0, CONV_CH, MXU_COLS):
        val = col(V_END + c0, MXU_COLS)
        gate = col(V_END + CONV_CH + c0, MXU_COLS)
        glu_ref[:, c0:c0 + MXU_COLS] = (val * jax.nn.sigmoid(gate)).astype(bf16)


def _inproj(layer, w_in, h=None, x2d=None, meta=None, eg=None, eb=None):
    embed = h is None
    row = lambda w: pl.BlockSpec((TM, w), lambda s: (s, 0))
    w_spec = pl.BlockSpec((None, D_MODEL, GLU_END), lambda s: (layer, 0, 0), pipeline_mode=pl.Buffered(1))
    out_specs = [row(ATTN_WIDTH), row(KV_WIDTH), row(KV_WIDTH), row(CONV_CH)]
    out_shape = [
        jax.ShapeDtypeStruct((NT, ATTN_WIDTH), bf16),
        jax.ShapeDtypeStruct((NT, KV_WIDTH), bf16),
        jax.ShapeDtypeStruct((NT, KV_WIDTH), bf16),
        jax.ShapeDtypeStruct((NT, CONV_CH), bf16),
    ]
    if embed:
        in_specs = [
            pl.BlockSpec((TM, D_MODEL), lambda s: (jnp.maximum(s - 1, 0), 0)),
            _full((N_META, D_MODEL)), _full((1, D_MODEL)), _full((1, D_MODEL)), w_spec,
        ]
        args = (x2d, meta, eg, eb, w_in)
        out_specs.append(row(D_MODEL))
        out_shape.append(jax.ShapeDtypeStruct((NT, D_MODEL), f32))
    else:
        in_specs = [row(D_MODEL), w_spec]
        args = (h, w_in)
    return pl.pallas_call(
        functools.partial(_inproj_kernel, embed),
        grid=(N_TILES,),
        in_specs=in_specs,
        out_specs=out_specs,
        out_shape=out_shape,
        scratch_shapes=[pltpu.VMEM((D_MODEL, GLU_END), bf16)],
        compiler_params=_cparams("arbitrary"),
        name="inproj_glu",
    )(*args)


def _rel_bucket_np(dist):
    n = np.maximum(dist, 0)
    max_exact = N_BUCKETS // 2
    nf = np.maximum(n, 1).astype(np.float32)
    scaled = (np.log(nf / np.float32(max_exact)) / np.float32(math.log(MAX_DISTANCE / max_exact))
              * np.float32(N_BUCKETS - max_exact))
    large = np.minimum(max_exact + scaled.astype(np.int32), N_BUCKETS - 1)
    return np.where(n < max_exact, n, large).astype(np.int32)


def _bucket_tables():
    a = np.arange(BLOCK)[:, None]
    m = np.arange(BLOCK)[None, :]
    kk = np.arange(2 * BLOCK)[None, :]
    dist_band = BLOCK + a - kk
    band_ok = (dist_band >= 0) & (dist_band < WINDOW)
    band = np.where(band_ok, _rel_bucket_np(dist_band), -1)
    real = np.full((2, BLOCK, KSEG), -1, np.int32)
    for t, n in enumerate((0, 1)):
        dist_meta = N_META + n * BLOCK + a - m
        real[t, :, :BLOCK] = np.where(m < N_META, _rel_bucket_np(dist_meta), -1)
        real[t, :, BLOCK:] = band
    real[0, :, BLOCK:2 * BLOCK] = -1
    for n in range(2, SEQ // BLOCK):
        dist_meta = N_META + n * BLOCK + a - m
        assert np.array_equal(np.where(m < N_META, _rel_bucket_np(dist_meta), -1), real[1, :, :BLOCK])
    r = np.arange(N_META_ROWS)
    same = (r[:, None] // N_META) == (r[None, :] // N_META)
    dmm = (r[:, None] % N_META) - (r[None, :] % N_META)
    meta = np.where(same & (dmm >= 0), _rel_bucket_np(dmm), -1).astype(np.int32)
    return real, meta


_BK_REAL, _BK_META = _bucket_tables()


def _bias_from_buckets(bk, rb_ref, h):
    acc = jnp.full(bk.shape, NEG, f32)
    for bkt in range(N_BUCKETS):
        acc = jnp.where(bk == bkt, rb_ref[bkt, h], acc)
    return acc


def _expand_kv(x2):
    xf = x2.astype(f32)
    r = pltpu.roll(xf, HEAD_DIM, axis=1)
    low = lax.broadcasted_iota(i32, xf.shape, 1) < HEAD_DIM
    h0 = jnp.where(low, xf, r).astype(bf16)
    h1 = jnp.where(low, r, xf).astype(bf16)
    return (jnp.concatenate([h0, h0], axis=1), jnp.concatenate([h1, h1], axis=1))


def _block_diag(x, n_keys):
    lane_blk = lax.broadcasted_iota(i32, (n_keys, GROUP * HEAD_DIM), 1) // HEAD_DIM
    zero = jnp.zeros_like(x)
    return jnp.concatenate([jnp.where(lane_blk == h, x, zero) for h in range(GROUP)], axis=0)


def _softmax_block(sh, sink):
    mx = jnp.maximum(jnp.max(sh, -1, keepdims=True), sink)
    p = jnp.exp(sh - mx)
    den = jnp.sum(p, -1, keepdims=True) + jnp.exp(sink - mx)
    return p.astype(bf16), 1.0 / den


def _head_scale(rs, m_rows):
    ol = lax.broadcasted_iota(i32, (m_rows, GROUP * HEAD_DIM), 1) // HEAD_DIM
    return jnp.where(ol == 0, rs[0], jnp.where(ol == 1, rs[1], jnp.where(ol == 2, rs[2], rs[3])))


def _attend(qg, kx, vx, bias_of_head, sink_of_head, n_keys):
    kbd = _block_diag(kx, n_keys)
    vbd = _block_diag(vx, n_keys)
    s = lax.dot_general(qg, kbd, (((1,), (1,)), ((), ())), preferred_element_type=f32)
    ps, rs = [], []
    for h in range(GROUP):
        p, r = _softmax_block(s[:, h * n_keys:(h + 1) * n_keys] + bias_of_head(h), sink_of_head(h))
        ps.append(p)
        rs.append(r)
    o = jnp.dot(jnp.concatenate(ps, axis=1), vbd, preferred_element_type=f32)
    return o * _head_scale(rs, qg.shape[0])


def _attend_tile(qg, kbd, vbd, kbd_meta, vbd_meta, bias_of, sink_of_head):
    nt = (((1,), (1,)), ((), ()))
    nb = BLOCKS_PER_TILE
    s_meta = lax.dot_general(qg, kbd_meta, nt, preferred_element_type=f32)
    s_prev, s_cur = [None] * nb, [None] * nb
    for b in range(nb + 1):
        lo, hi = max(b - 1, 0) * BLOCK, min(b + 1, nb) * BLOCK
        sc = lax.dot_general(qg[lo:hi], kbd[b], nt, preferred_element_type=f32)
        if b > 0:
            s_cur[b - 1] = sc[:BLOCK]
        if b < nb:
            s_prev[b] = sc[-BLOCK:]
    pm, pp, pc, scales = [], [], [], []
    for blk in range(nb):
        rows = slice(blk * BLOCK, (blk + 1) * BLOCK)
        pms, pps, pcs, rs = [], [], [], []
        for h in range(GROUP):
            cols = slice(h * BLOCK, (h + 1) * BLOCK)
            sh = jnp.concatenate([s_meta[rows, cols], s_prev[blk][:, cols], s_cur[blk][:, cols]], axis=1)
            p, r = _softmax_block(sh + bias_of(blk, h), sink_of_head(h))
            pms.append(p[:, :BLOCK])
            pps.append(p[:, BLOCK:2 * BLOCK])
            pcs.append(p[:, 2 * BLOCK:])
            rs.append(r)
        pm.append(jnp.concatenate(pms, axis=1))
        pp.append(jnp.concatenate(pps, axis=1))
        pc.append(jnp.concatenate(pcs, axis=1))
        scales.append(_head_scale(rs, BLOCK))
    o = [None] * nb
    for b in range(nb + 1):
        parts = ([pc[b - 1]] if b > 0 else []) + ([pp[b]] if b < nb else [])
        ob = jnp.dot(jnp.concatenate(parts, axis=0), vbd[b], preferred_element_type=f32)
        if b > 0:
            o[b - 1] = o[b - 1] + ob[:BLOCK]
        if b < nb:
            o[b] = ob[-BLOCK:]
    om = jnp.dot(jnp.concatenate(pm, axis=0), vbd_meta, preferred_element_type=f32)
    return (jnp.concatenate(o, axis=0) + om) * jnp.concatenate(scales, axis=0)


def _bias_kernel(rb_ref, bkr_ref, bkm_ref, br_ref, bm_ref):
    for h in range(N_Q_HEADS):
        for t in range(2):
            br_ref[t, h] = _bias_from_buckets(bkr_ref[t], rb_ref, h)
        bm_ref[h] = _bias_from_buckets(bkm_ref[...], rb_ref, h)


def _bias_tables(rel_bias):
    return pl.pallas_call(
        _bias_kernel,
        in_specs=[_smem(), pl.BlockSpec(memory_space=pltpu.VMEM), pl.BlockSpec(memory_space=pltpu.VMEM)],
        out_specs=[pl.BlockSpec(memory_space=pltpu.VMEM), pl.BlockSpec(memory_space=pltpu.VMEM)],
        out_shape=[
            jax.ShapeDtypeStruct((2, N_Q_HEADS, BLOCK, KSEG), f32),
            jax.ShapeDtypeStruct((N_Q_HEADS, N_META_ROWS, N_META_ROWS), f32),
        ],
        name="rel_bias_tables",
    )(rel_bias, jnp.asarray(_BK_REAL), jnp.asarray(_BK_META))


def _attn_kernel(layer, sinks_ref, q_ref, k_ref, v_ref, kp_ref, vp_ref, km_ref, vm_ref,
                 br_ref, bm_ref, o_ref):
    s = pl.program_id(0)
    gw = GROUP * HEAD_DIM
    sink = lambda head: sinks_ref[layer, head]

    @pl.when(s == 0)
    def _():
        kxs = _expand_kv(k_ref[:N_META_ROWS, :])
        vxs = _expand_kv(v_ref[:N_META_ROWS, :])
        for g in range(N_KV_HEADS):
            o = _attend(q_ref[:N_META_ROWS, g * gw:(g + 1) * gw], kxs[g], vxs[g],
                        lambda h, g=g: bm_ref[g * GROUP + h],
                        lambda h, g=g: sink(g * GROUP + h), N_META_ROWS)
            o_ref[:N_META_ROWS, g * gw:(g + 1) * gw] = o.astype(bf16)
        o_ref[N_META_ROWS:, :] = jnp.zeros((TM - N_META_ROWS, ATTN_WIDTH), bf16)

    @pl.when(s > 0)
    def _():
        first = ((s - 1) % TILES_PER_BATCH) == 0
        tbl0 = jnp.where(first, 0, 1)
        zpad = jnp.zeros((BLOCK - N_META, KV_WIDTH), bf16)
        kms = _expand_kv(jnp.concatenate([km_ref[...], zpad], axis=0))
        vms = _expand_kv(jnp.concatenate([vm_ref[...], zpad], axis=0))
        kxs = [_expand_kv(kp_ref[...])] + [_expand_kv(k_ref[b * BLOCK:(b + 1) * BLOCK, :])
                                          for b in range(BLOCKS_PER_TILE)]
        vxs = [_expand_kv(vp_ref[...])] + [_expand_kv(v_ref[b * BLOCK:(b + 1) * BLOCK, :])
                                          for b in range(BLOCKS_PER_TILE)]
        for g in range(N_KV_HEADS):
            o = _attend_tile(
                q_ref[:, g * gw:(g + 1) * gw],
                [_block_diag(x[g], BLOCK) for x in kxs], [_block_diag(x[g], BLOCK) for x in vxs],
                _block_diag(kms[g], BLOCK), _block_diag(vms[g], BLOCK),
                lambda blk, h, g=g: br_ref[tbl0 if blk == 0 else 1, g * GROUP + h],
                lambda h, g=g: sink(g * GROUP + h))
            o_ref[:, g * gw:(g + 1) * gw] = o.astype(bf16)


def _attention(layer, bias_real, bias_meta, sinks, q, k, v):
    row = lambda w: pl.BlockSpec((TM, w), lambda s: (s, 0))
    prev = pl.BlockSpec((BLOCK, KV_WIDTH), lambda s: (jnp.maximum(s * BLOCKS_PER_TILE - 1, 0), 0))
    meta = pl.BlockSpec((N_META, KV_WIDTH), lambda s: (jnp.maximum(s - 1, 0) // TILES_PER_BATCH, 0))
    return pl.pallas_call(
        functools.partial(_attn_kernel, layer),
        grid=(N_TILES,),
        in_specs=[
            _smem(),
            row(ATTN_WIDTH), row(KV_WIDTH), row(KV_WIDTH),
            prev, prev, meta, meta,
            _full((2, N_Q_HEADS, BLOCK, KSEG)), _full((N_Q_HEADS, N_META_ROWS, N_META_ROWS)),
        ],
        out_specs=row(ATTN_WIDTH),
        out_shape=jax.ShapeDtypeStruct((NT, ATTN_WIDTH), bf16),
        compiler_params=_cparams("arbitrary"),
        name="swa_attention",
    )(sinks, q, k, v, k, v, k, v, bias_real, bias_meta)


SPREAD = TM // BATCH
SH_ROWS = CONV_HALO + TM - SUBLANES
CONV_ROWS = 128
GATE_BLOCKS = 2 * D_MODEL // MXU_COLS


def _conv_window(s, glu_ref, gprev_ref, gmeta_ref, win_ref, sh_ref):
    is_meta = s == 0
    first = ((s - 1) % TILES_PER_BATCH) == 0
    glu = glu_ref[...].astype(f32)
    zgap = jnp.zeros((SPREAD - N_META, CONV_CH), f32)
    spread = jnp.concatenate(
        [p for bb in range(BATCH) for p in (zgap, glu[bb * N_META:(bb + 1) * N_META])], axis=0)
    win_ref[CONV_HALO:, :] = jnp.where(is_meta, spread, glu)
    halo_first = jnp.concatenate(
        [jnp.zeros((CONV_HALO - N_META, CONV_CH), f32), gmeta_ref[...].astype(f32)], axis=0)
    halo = jnp.where(first, halo_first, gprev_ref[...].astype(f32))
    win_ref[:CONV_HALO, :] = jnp.where(is_meta, 0.0, halo)

    for r in range(1, SUBLANES):
        sh_ref[r - 1] = win_ref[r:r + SH_ROWS, :]


def _conv_finish(s, layer, dw_ref, cb_ref, cg_ref, cbeta_ref, win_ref, sh_ref, cv_ref):
    off = CONV_HALO - (CONV_WIDTH - 1)
    for rc in range(TM // CONV_ROWS):
        for lc in range(CONV_CH // LANES):
            ls = slice(lc * LANES, (lc + 1) * LANES)
            acc = None
            for t in range(CONV_WIDTH):
                r = (off + t) % SUBLANES
                r0 = rc * CONV_ROWS + (off + t) - r
                x = win_ref[r0:r0 + CONV_ROWS, ls] if r == 0 else sh_ref[r - 1, r0:r0 + CONV_ROWS, ls]
                term = dw_ref[t:t + 1, ls] * x
                acc = term if acc is None else acc + term
            cv_ref[rc * CONV_ROWS:(rc + 1) * CONV_ROWS, ls] = acc
    y = _ln(cv_ref[...] + _layer_row(cb_ref, layer), _layer_row(cg_ref, layer), _layer_row(cbeta_ref, layer))
    c = y * jax.nn.sigmoid(y)
    gathered = jnp.concatenate(
        [c[bb * SPREAD + SPREAD - N_META:(bb + 1) * SPREAD] for bb in range(BATCH)]
        + [jnp.zeros((TM - N_META_ROWS, CONV_CH), f32)], axis=0)
    return jnp.where(s == 0, gathered, c).astype(bf16)


def _mix_kernel(layer, h_ref, a_ref, glu_ref, gprev_ref, gmeta_ref, dw_ref, cb_ref, cg_ref, cbeta_ref, *rest):
    wg_refs = rest[:GATE_BLOCKS]
    (wap_ref, wcp_ref, wo_ref, g_ref, b_ref, o_ref,
     win_ref, sh_ref, cv_ref, wgb_ref, wapb_ref, wcpb_ref, wob_ref) = rest[GATE_BLOCKS:]
    s = pl.program_id(0)

    @pl.when(s == 0)
    def _():
        for k in range(GATE_BLOCKS):
            wgb_ref[:, k * MXU_COLS:(k + 1) * MXU_COLS] = wg_refs[k][...].astype(bf16)
        wapb_ref[...] = wap_ref[...].astype(bf16)
        wcpb_ref[...] = wcp_ref[...].astype(bf16)
        wob_ref[...] = wo_ref[...].astype(bf16)

    h = h_ref[...]
    gates = jnp.dot(h.astype(bf16), wgb_ref[...], preferred_element_type=f32)
    ya = jnp.dot(a_ref[...], wapb_ref[...], preferred_element_type=f32)
    _conv_window(s, glu_ref, gprev_ref, gmeta_ref, win_ref, sh_ref)
    c = _conv_finish(s, layer, dw_ref, cb_ref, cg_ref, cbeta_ref, win_ref, sh_ref, cv_ref)
    yc = jnp.dot(c, wcpb_ref[...], preferred_element_type=f32)
    m = jax.nn.sigmoid(gates[:, :D_MODEL]) * ya + jax.nn.sigmoid(gates[:, D_MODEL:]) * yc
    mix = jnp.dot(m.astype(bf16), wob_ref[...], preferred_element_type=f32)
    o_ref[...] = _ln(ALPHA * h + mix, _layer_row(g_ref, layer), _layer_row(b_ref, layer))


def _mix(layer, h, a, glu, dw, cb, cg, cbeta, w_in, wap, wcp, wo, g, b):
    row = lambda w: pl.BlockSpec((TM, w), lambda s: (s, 0))
    gate_block = lambda k: pl.BlockSpec((None, D_MODEL, MXU_COLS),
                                        lambda s: (layer, 0, GLU_END // MXU_COLS + k),
                                        pipeline_mode=pl.Buffered(1))
    gprev = pl.BlockSpec((CONV_HALO, CONV_CH),
                         lambda s: (jnp.maximum(s * (TM // CONV_HALO) - 1, 0), 0))
    gmeta = pl.BlockSpec((N_META, CONV_CH), lambda s: (jnp.maximum(s - 1, 0) // TILES_PER_BATCH, 0))
    per_layer = lambda r, c: pl.BlockSpec((None, r, c), lambda s: (layer, 0, 0), pipeline_mode=pl.Buffered(1))
    conv_vec = _full((DEPTH, CONV_CH))
    model_vec = _full((DEPTH, D_MODEL))
    return pl.pallas_call(
        functools.partial(_mix_kernel, layer),
        grid=(N_TILES,),
        in_specs=[
            row(D_MODEL), row(ATTN_WIDTH), row(CONV_CH), gprev, gmeta,
            pl.BlockSpec((None, CONV_WIDTH, CONV_CH), lambda s: (layer, 0, 0)), conv_vec, conv_vec, conv_vec,
            *[gate_block(k) for k in range(GATE_BLOCKS)], per_layer(ATTN_WIDTH, D_MODEL),
            per_layer(CONV_CH, D_MODEL), per_layer(D_MODEL, D_MODEL),
            model_vec, model_vec,
        ],
        out_specs=row(D_MODEL),
        out_shape=jax.ShapeDtypeStruct((NT, D_MODEL), f32),
        scratch_shapes=[
            pltpu.VMEM((CONV_HALO + TM, CONV_CH), f32),
            pltpu.VMEM((7, SH_ROWS, CONV_CH), f32),
            pltpu.VMEM((TM, CONV_CH), f32),
            pltpu.VMEM((D_MODEL, 2 * D_MODEL), bf16),
            pltpu.VMEM((ATTN_WIDTH, D_MODEL), bf16),
            pltpu.VMEM((CONV_CH, D_MODEL), bf16),
            pltpu.VMEM((D_MODEL, D_MODEL), bf16),
        ],
        compiler_params=_cparams("arbitrary"),
        name="conv_mix_ln1",
    )(h, a, glu, glu, glu, dw, cb, cg, cbeta, *([w_in] * GATE_BLOCKS), wap, wcp, wo, g, b)


def _swiglu_chunks(xb, wg_ref, wu_ref, wd_ref, chunks, row_scale=None):
    acc = None
    c0 = 0
    for w in chunks:
        gt = jnp.dot(xb, wg_ref[:, c0:c0 + w].astype(bf16), preferred_element_type=f32)
        up = jnp.dot(xb, wu_ref[:, c0:c0 + w].astype(bf16), preferred_element_type=f32)
        act = gt * jax.nn.sigmoid(gt) * up
        if row_scale is not None:
            act = act * row_scale
        part = jnp.dot(act.astype(bf16), wd_ref[c0:c0 + w, :].astype(bf16), preferred_element_type=f32)
        acc = part if acc is None else acc + part
        c0 += w
    return acc


def _ffn_kernel(layer, h_ref, wg_ref, wu_ref, wd_ref, g_ref, b_ref, o_ref):
    h = h_ref[...]
    f = _swiglu_chunks(h.astype(bf16), wg_ref, wu_ref, wd_ref, FF_CHUNKS_DENSE)
    o_ref[...] = _ln(ALPHA * h + f, _layer_row(g_ref, layer), _layer_row(b_ref, layer))


def _dense_ffn(layer, j, h, wg, wu, wd, g, b):
    row = pl.BlockSpec((TM, D_MODEL), lambda s: (s, 0))
    per_layer = lambda r, c: pl.BlockSpec((None, r, c), lambda s: (j, 0, 0), pipeline_mode=pl.Buffered(1))
    return pl.pallas_call(
        functools.partial(_ffn_kernel, layer),
        grid=(N_TILES,),
        in_specs=[row, per_layer(D_MODEL, D_FF), per_layer(D_MODEL, D_FF), per_layer(D_FF, D_MODEL),
                  _full((DEPTH, D_MODEL)), _full((DEPTH, D_MODEL))],
        out_specs=row,
        out_shape=jax.ShapeDtypeStruct((NT, D_MODEL), f32),
        compiler_params=_cparams("parallel"),
        name="dense_ffn_ln2",
    )(h, wg, wu, wd, g, b)


def _route_copy(src_ref, xs_ref, sem, e, row):
    return pltpu.make_async_copy(
        src_ref.at[pl.ds(e * CHUNK, CHUNK), :],
        xs_ref.at[pl.ds(pl.multiple_of(e * CAP + row, ALIGN), CHUNK), :],
        sem)


def _zero_copy(zbuf_ref, xs_ref, sem, e, row):
    return pltpu.make_async_copy(
        zbuf_ref,
        xs_ref.at[pl.ds(pl.multiple_of(e * CAP + row, ALIGN), TM), :],
        sem)


def _dispatch_pass(p, ranki, sel, gate, hi, dst_ref):
    slot_id = lax.broadcasted_iota(i32, (CHUNK, TM), 0) + p * CHUNK
    lane = lax.broadcasted_iota(i32, (CHUNK, LANES), 1)
    ps = []
    for e in range(N_EXPERTS):
        pe = (slot_id == ranki[e:e + 1, :]) & sel[e:e + 1, :]
        ps.append(jnp.where(pe, 1.0, 0.0).astype(bf16))
        gs = jnp.sum(jnp.where(pe, gate[e:e + 1, :], 0.0), axis=1, keepdims=True)
        g_hi = gs.astype(bf16).astype(f32)
        g_mid = (gs - g_hi).astype(bf16).astype(f32)
        g_lo = gs - g_hi - g_mid
        parts = jnp.where(lane == 0, g_hi, jnp.where(lane == 1, g_mid, jnp.where(lane == 2, g_lo, 0.0)))
        dst_ref[e * CHUNK:(e + 1) * CHUNK, D_MODEL:] = parts.astype(bf16)
    pall = jnp.concatenate(ps, axis=0)
    dst_ref[:, :D_MODEL] = jnp.dot(pall, hi, preferred_element_type=f32).astype(bf16)


def _route_kernel(h_ref, r_ref, info_ref, base_ref, rows_ref, te_ref, tb_ref, tv_ref, xs_ref,
                  xbuf_ref, xov_ref, zbuf_ref, cnt_ref, sem_ref):
    s = pl.program_id(0)
    slot = s % 2

    @pl.when(s == 0)
    def _():
        for e in range(N_EXPERTS):
            cnt_ref[e] = 0
        zbuf_ref[...] = jnp.zeros((TM, XS_W), bf16)

    h = h_ref[...]
    hi = h.astype(bf16)
    lo = (h - hi.astype(f32)).astype(bf16)
    pa = jnp.dot(hi, r_ref[...], preferred_element_type=f32)
    pb = jnp.dot(lo, r_ref[...], preferred_element_type=f32)
    logits = pa + pltpu.roll(pa, LANES - N_EXPERTS, axis=1) + pb
    lt = logits.T[:N_EXPERTS, :]

    eid = lax.broadcasted_iota(i32, (N_EXPERTS, TM), 0)
    m1 = jnp.max(lt, axis=0, keepdims=True)
    i1 = jnp.min(jnp.where(lt == m1, eid, N_EXPERTS), axis=0, keepdims=True)
    lt2 = jnp.where(eid == i1, NEG, lt)
    m2 = jnp.max(lt2, axis=0, keepdims=True)
    i2 = jnp.min(jnp.where(lt2 == m2, eid, N_EXPERTS), axis=0, keepdims=True)
    ex = jnp.exp(m2 - m1)
    g1 = 1.0 / (1.0 + ex)
    g2 = ex * g1
    tok = lax.broadcasted_iota(i32, (1, TM), 1)
    valid = (s > 0) | (tok < N_META_ROWS)
    sel1 = (eid == i1) & valid
    sel2 = (eid == i2) & valid
    sel = sel1 | sel2
    onehot = jnp.where(sel, 1.0, 0.0)
    tri = jnp.where(lax.broadcasted_iota(i32, (TM, TM), 0) < lax.broadcasted_iota(i32, (TM, TM), 1), 1.0, 0.0)
    rank = jnp.dot(onehot, tri, preferred_element_type=f32)
    gate = jnp.where(sel1, g1, jnp.where(sel2, g2, 0.0))
    rank1 = jnp.sum(jnp.where(sel1, rank, 0.0), axis=0, keepdims=True)
    rank2 = jnp.sum(jnp.where(sel2, rank, 0.0), axis=0, keepdims=True)
    info = jnp.concatenate([i1.astype(f32), i2.astype(f32), rank1, rank2,
                            jnp.zeros((LANES - 4, TM), f32)], axis=0)
    info_ref[...] = info.T
    ranki = rank.astype(i32)

    _dispatch_pass(0, ranki, sel, gate, hi, xbuf_ref.at[slot])

    @pl.when(s > 0)
    def _():
        for e in range(N_EXPERTS):
            _route_copy(xbuf_ref.at[1 - slot], xs_ref, sem_ref.at[1 - slot, e], e, 0).wait()

    bases, counts = [], []
    for e in range(N_EXPERTS):
        base = cnt_ref[e]
        n_e = jnp.sum(onehot[e:e + 1, :]).astype(i32)
        bases.append(base)
        counts.append(n_e)
        base_ref[s, e] = base
        _route_copy(xbuf_ref.at[slot], xs_ref, sem_ref.at[slot, e], e, base).start()
        cnt_ref[e] = base + ((n_e + (ALIGN - 1)) // ALIGN) * ALIGN
    most = functools.reduce(jnp.maximum, counts)

    for p in range(1, N_PASS):
        @pl.when(most > p * CHUNK)
        def _(p=p):
            _dispatch_pass(p, ranki, sel, gate, hi, xov_ref)
            for e in range(N_EXPERTS):
                _route_copy(xov_ref, xs_ref, sem_ref.at[2, e], e, bases[e] + p * CHUNK).start()
            for e in range(N_EXPERTS):
                _route_copy(xov_ref, xs_ref, sem_ref.at[2, e], e, 0).wait()

    @pl.when(s == N_TILES - 1)
    def _():
        for e in range(N_EXPERTS):
            _route_copy(xbuf_ref.at[slot], xs_ref, sem_ref.at[slot, e], e, 0).wait()
        used = [cnt_ref[e] for e in range(N_EXPERTS)]
        for e in range(N_EXPERTS):
            _zero_copy(zbuf_ref, xs_ref, sem_ref.at[2, e], e, used[e]).start()
        starts, ends, total = [], [], 0
        for e in range(N_EXPERTS):
            n_t = jnp.maximum(lax.shift_right_logical(used[e] + (TM - 1), TM.bit_length() - 1), 1)
            rows_ref[e] = n_t * TM
            starts.append(total)
            total = total + n_t
            ends.append(total)

        def tile_entry(t):
            e_t = sum(jnp.where(t >= ends[e], 1, 0) for e in range(N_EXPERTS - 1))
            first = sum(jnp.where(e_t == e, starts[e], 0) for e in range(N_EXPERTS))
            left = sum(jnp.where(e_t == e, used[e], 0) for e in range(N_EXPERTS)) - (t - first) * TM
            return e_t, e_t * (CAP // TM) + (t - first), left

        e_last, blk_last, _ = tile_entry(total - 1)
        for t in range(FFN_GRID):
            e_t, blk_t, left = tile_entry(t)
            live = t < total
            te_ref[t] = jnp.where(live, e_t, e_last)
            tb_ref[t] = jnp.where(live, blk_t, blk_last)
            tv_ref[t] = jnp.where(live, jnp.where(left <= TM // 2, TILE_HALF, TILE_FULL), TILE_SKIP)
        for e in range(N_EXPERTS):
            _zero_copy(zbuf_ref, xs_ref, sem_ref.at[2, e], e, 0).wait()


def _route(h, rsplit):
    return pl.pallas_call(
        _route_kernel,
        grid=(N_TILES,),
        in_specs=[pl.BlockSpec((TM, D_MODEL), lambda s: (s, 0)), _full((D_MODEL, LANES))],
        out_specs=[
            pl.BlockSpec((TM, LANES), lambda s: (s, 0)),
            _smem(), _smem(), _smem(), _smem(), _smem(),
            pl.BlockSpec(memory_space=pl.ANY),
        ],
        out_shape=[
            jax.ShapeDtypeStruct((NT, LANES), f32),
            jax.ShapeDtypeStruct((N_TILES, N_EXPERTS), i32),
            jax.ShapeDtypeStruct((N_EXPERTS,), i32),
            jax.ShapeDtypeStruct((FFN_GRID,), i32),
            jax.ShapeDtypeStruct((FFN_GRID,), i32),
            jax.ShapeDtypeStruct((FFN_GRID,), i32),
            jax.ShapeDtypeStruct((N_EXPERTS * CAP, XS_W), bf16),
        ],
        scratch_shapes=[
            pltpu.VMEM((2, N_EXPERTS * CHUNK, XS_W), bf16),
            pltpu.VMEM((N_EXPERTS * CHUNK, XS_W), bf16),
            pltpu.VMEM((TM, XS_W), bf16),
            pltpu.SMEM((N_EXPERTS,), i32),
            pltpu.SemaphoreType.DMA((3, N_EXPERTS)),
        ],
        compiler_params=_cparams("arbitrary"),
        name="moe_route_dispatch",
    )(h, rsplit)


def _expert_kernel(te_ref, tb_ref, tv_ref, x_ref, wg_ref, wu_ref, wd_ref, y_ref):
    t = pl.program_id(0)

    def ffn(rows):
        x = x_ref[:rows, :]
        gate = jnp.sum(x[:, D_MODEL:].astype(f32), axis=1, keepdims=True)
        y = _swiglu_chunks(x[:, :D_MODEL], wg_ref, wu_ref, wd_ref, FF_CHUNKS_MOE, row_scale=gate)
        y_ref[:rows, :] = y.astype(bf16)

    @pl.when(tv_ref[t] == TILE_FULL)
    def _():
        ffn(TM)

    @pl.when(tv_ref[t] == TILE_HALF)
    def _():
        ffn(TM // 2)
        y_ref[TM // 2:, :] = jnp.zeros((TM // 2, D_MODEL), bf16)


def _expert_ffn(j, tile_e, tile_blk, tile_valid, xs, wg, wu, wd):
    per_expert = lambda r, c: pl.BlockSpec((None, None, r, c), lambda t, te, tb, tv: (j, te[t], 0, 0))
    grid_spec = pltpu.PrefetchScalarGridSpec(
        num_scalar_prefetch=3,
        grid=(FFN_GRID,),
        in_specs=[
            pl.BlockSpec((TM, XS_W), lambda t, te, tb, tv: (tb[t], 0)),
            per_expert(D_MODEL, D_EXPERT), per_expert(D_MODEL, D_EXPERT), per_expert(D_EXPERT, D_MODEL),
        ],
        out_specs=pl.BlockSpec((TM, D_MODEL), lambda t, te, tb, tv: (tb[t], 0)),
    )
    return pl.pallas_call(
        _expert_kernel,
        grid_spec=grid_spec,
        out_shape=jax.ShapeDtypeStruct((N_EXPERTS * CAP, D_MODEL), bf16),
        compiler_params=_cparams("arbitrary"),
        name="moe_expert_ffn",
    )(tile_e, tile_blk, tile_valid, xs, wg, wu, wd)


def _combine_copy(ys_ref, dst_ref, sem, e, row):
    return pltpu.make_async_copy(
        ys_ref.at[pl.ds(pl.multiple_of(e * CAP + row, ALIGN), CHUNK), :],
        dst_ref.at[pl.ds(e * CHUNK, CHUNK), :],
        sem)


def _window(base_ref, rows_ref, s, e, p):
    want = base_ref[s, e] + p * CHUNK
    start = jnp.minimum(want, rows_ref[e] - CHUNK)
    return start, want - start


def _onehot_cols(info, deltas, p):
    col = lax.broadcasted_iota(i32, (TM, N_EXPERTS * CHUNK), 1)
    hit = None
    for k in range(2):
        ek = info[:, k:k + 1].astype(i32)
        rk = info[:, 2 + k:3 + k].astype(i32) - p * CHUNK
        dk = functools.reduce(lambda a, b: a + b,
                              [jnp.where(ek == e, deltas[e], 0) for e in range(N_EXPERTS)])
        ck = jnp.where((rk >= 0) & (rk < CHUNK), ek * CHUNK + rk + dk, -1)
        hk = col == ck
        hit = hk if hit is None else hit | hk
    return jnp.where(hit, 1.0, 0.0).astype(bf16)


def _combine_kernel(layer, base_ref, rows_ref, h_ref, info_ref, ys_ref, g_ref, b_ref, o_ref,
                    ybuf_ref, yov_ref, f_ref, sem_ref):
    i = pl.program_id(0)
    n = pl.num_programs(0)
    s = i + 1
    slot = i % 2

    def start_fetch(tile, to_slot):
        for e in range(N_EXPERTS):
            start, _ = _window(base_ref, rows_ref, tile, e, 0)
            _combine_copy(ys_ref, ybuf_ref.at[to_slot], sem_ref.at[to_slot, e], e, start).start()

    @pl.when(i == 0)
    def _():
        start_fetch(s, slot)

    @pl.when(i + 1 < n)
    def _():
        start_fetch(s + 1, 1 - slot)

    info = info_ref[...]
    deltas = [_window(base_ref, rows_ref, s, e, 0)[1] for e in range(N_EXPERTS)]
    qm = _onehot_cols(info, deltas, 0)
    for e in range(N_EXPERTS):
        _combine_copy(ys_ref, ybuf_ref.at[slot], sem_ref.at[slot, e], e, 0).wait()
    f_ref[...] = jnp.dot(qm, ybuf_ref[slot], preferred_element_type=f32)

    deepest = jnp.max(jnp.maximum(info[:, 2:3], info[:, 3:4])).astype(i32)
    for p in range(1, N_PASS):
        @pl.when(deepest >= p * CHUNK)
        def _(p=p):
            wins = [_window(base_ref, rows_ref, s, e, p) for e in range(N_EXPERTS)]
            for e in range(N_EXPERTS):
                _combine_copy(ys_ref, yov_ref, sem_ref.at[2, e], e, wins[e][0]).start()
            qp = _onehot_cols(info, [w[1] for w in wins], p)
            for e in range(N_EXPERTS):
                _combine_copy(ys_ref, yov_ref, sem_ref.at[2, e], e, 0).wait()
            f_ref[...] += jnp.dot(qp, yov_ref[...], preferred_element_type=f32)

    o_ref[...] = _ln(ALPHA * h_ref[...] + f_ref[...], _layer_row(g_ref, layer), _layer_row(b_ref, layer))


def _combine(layer, base, rows, h, info, ys, g, b):
    n_out = N_TILES - 1
    grid_spec = pltpu.PrefetchScalarGridSpec(
        num_scalar_prefetch=2,
        grid=(n_out,),
        in_specs=[
            pl.BlockSpec((TM, D_MODEL), lambda i, base, rows: (i + 1, 0)),
            pl.BlockSpec((TM, LANES), lambda i, base, rows: (i + 1, 0)),
            pl.BlockSpec(memory_space=pl.ANY),
            pl.BlockSpec((DEPTH, D_MODEL), lambda i, base, rows: (0, 0)),
            pl.BlockSpec((DEPTH, D_MODEL), lambda i, base, rows: (0, 0)),
        ],
        out_specs=pl.BlockSpec((TM, D_MODEL), lambda i, base, rows: (i, 0)),
        scratch_shapes=[
            pltpu.VMEM((2, N_EXPERTS * CHUNK, D_MODEL), bf16),
            pltpu.VMEM((N_EXPERTS * CHUNK, D_MODEL), bf16),
            pltpu.VMEM((TM, D_MODEL), f32),
            pltpu.SemaphoreType.DMA((3, N_EXPERTS)),
        ],
    )
    return pl.pallas_call(
        functools.partial(_combine_kernel, layer),
        grid_spec=grid_spec,
        out_shape=jax.ShapeDtypeStruct((n_out * TM, D_MODEL), f32),
        compiler_params=_cparams("arbitrary"),
        name="moe_combine_ln2",
    )(base, rows, h, info, ys, g, b)


def _split2(x):
    hi = x.astype(bf16)
    lo = (x - hi.astype(f32)).astype(bf16)
    return hi, lo


def kernel(x, meta_tokens, emb_ln_g, emb_ln_b, rel_bias, w_in, conv_dw, conv_b, conv_ln_g, conv_ln_b, sinks, w_attn_proj, w_conv_proj, w_out, ln1_g, ln1_b, ffn_w_gate, ffn_w_up, ffn_w_down, router, moe_w_gate, moe_w_up, moe_w_down, ln2_g, ln2_b):
    row = lambda v: v.reshape(1, -1)
    h = None
    out = None
    bias_real, bias_meta = _bias_tables(rel_bias)
    for i in range(DEPTH):
        if i == 0:
            q, k, v, glu, h = _inproj(i, w_in, x2d=x.reshape(BATCH * SEQ, D_MODEL), meta=meta_tokens,
                                      eg=row(emb_ln_g), eb=row(emb_ln_b))
        else:
            q, k, v, glu = _inproj(i, w_in, h=h)
        a = _attention(i, bias_real, bias_meta, sinks, q, k, v)
        h = _mix(i, h, a, glu, conv_dw, conv_b, conv_ln_g, conv_ln_b,
                 w_in, w_attn_proj, w_conv_proj, w_out, ln1_g, ln1_b)
        j = i // 2
        if i % 2 == 0:
            h = _dense_ffn(i, j, h, ffn_w_gate, ffn_w_up, ffn_w_down, ln2_g, ln2_b)
        else:
            r_hi, r_lo = _split2(router[j])
            rsplit = jnp.concatenate(
                [r_hi, r_lo, jnp.zeros((D_MODEL, LANES - 2 * N_EXPERTS), bf16)], axis=1)
            info, base, rows, tile_e, tile_blk, tile_valid, xs = _route(h, rsplit)
            ys = _expert_ffn(j, tile_e, tile_blk, tile_valid, xs, moe_w_gate, moe_w_up, moe_w_down)
            out = _combine(i, base, rows, h, info, ys, ln2_g, ln2_b)
    return out.reshape(BATCH, SEQ, D_MODEL)
```

```python
import functools
import math

import numpy as np
import jax
import jax.numpy as jnp
from jax import lax
from jax.experimental import pallas as pl
from jax.experimental.pallas import tpu as pltpu

f32 = jnp.float32
bf16 = jnp.bfloat16
i32 = jnp.int32

D_MODEL = 1024
BATCH = 8
SEQ = 2048
DEPTH = 2
N_META = 16
HEAD_DIM = 64
N_Q_HEADS = 8
N_KV_HEADS = 2
GROUP = N_Q_HEADS // N_KV_HEADS
ATTN_WIDTH = N_Q_HEADS * HEAD_DIM
KV_WIDTH = N_KV_HEADS * HEAD_DIM
WINDOW = 128
BLOCK = 128
CONV_CH = D_MODEL // 2
CONV_WIDTH = 31
N_BUCKETS = 32
MAX_DISTANCE = 128
D_FF = 2816
N_EXPERTS = 8
D_EXPERT = 1408
ALPHA = (2 * DEPTH) ** 0.25
LN_EPS = 1e-5
Q_END = ATTN_WIDTH
K_END = Q_END + KV_WIDTH
V_END = K_END + KV_WIDTH
GLU_END = V_END + 2 * CONV_CH

TM = 512
N_META_ROWS = BATCH * N_META
NT = TM + BATCH * SEQ
N_TILES = NT // TM
TILES_PER_BATCH = SEQ // TM
BLOCKS_PER_TILE = TM // BLOCK
KSEG = 3 * BLOCK
CONV_HALO = 32
NEG = -1e30
LANES = 128
SUBLANES = 8

CHUNK = 192
N_PASS = -(-TM // CHUNK)
ALIGN = 2 * SUBLANES
XS_W = D_MODEL + LANES
CAP = 35 * TM
FFN_GRID = 85
MXU_COLS = 256
FF_CHUNKS_DENSE = (256,) * 11
FF_CHUNKS_MOE = (256,) * 5 + (128,)

VMEM_LIMIT = 56 * 1024 * 1024


def _cparams(sem="arbitrary"):
    return pltpu.CompilerParams(dimension_semantics=(sem,), vmem_limit_bytes=VMEM_LIMIT)


def _ln(x, g, b):
    mu = jnp.mean(x, -1, keepdims=True)
    xc = x - mu
    var = jnp.mean(xc * xc, -1, keepdims=True)
    return xc * lax.rsqrt(var + LN_EPS) * g + b


def _full(shape):
    return pl.BlockSpec(shape, lambda *_: (0,) * len(shape))


def _smem():
    return pl.BlockSpec(memory_space=pltpu.SMEM)


def _layer_row(ref, layer):
    return ref[layer:layer + 1, :]


def _embed_tile(s, x_ref, meta_ref, g_ref, b_ref):
    m = _ln(meta_ref[...], g_ref[...], b_ref[...])
    meta_tile = jnp.concatenate([m] * BATCH + [jnp.zeros((TM - N_META_ROWS, D_MODEL), f32)], axis=0)
    return jnp.where(s == 0, meta_tile, _ln(x_ref[...], g_ref[...], b_ref[...]))


def _inproj_kernel(embed, *refs):
    s = pl.program_id(0)
    if embed:
        x_ref, meta_ref, eg_ref, eb_ref, w_ref, q_ref, k_ref, v_ref, glu_ref, h_ref, wb_ref = refs
        h = _embed_tile(s, x_ref, meta_ref, eg_ref, eb_ref)
        h_ref[...] = h
    else:
        hin_ref, w_ref, q_ref, k_ref, v_ref, glu_ref, wb_ref = refs
        h = hin_ref[...]

    @pl.when(s == 0)
    def _():
        wb_ref[...] = w_ref[...].astype(bf16)

    hb = h.astype(bf16)
    col = lambda c0, w: jnp.dot(hb, wb_ref[:, c0:c0 + w], preferred_element_type=f32)
    for c0 in range(0, Q_END, MXU_COLS):
        q_ref[:, c0:c0 + MXU_COLS] = (col(c0, MXU_COLS) * (HEAD_DIM ** -0.5)).astype(bf16)
    kv = col(Q_END, 2 * KV_WIDTH)
    k_ref[...] = kv[:, :KV_WIDTH].astype(bf16)
    v_ref[...] = kv[:, KV_WIDTH:].astype(bf16)
    for c0 in range(0, CONV_CH, MXU_COLS):
        val = col(V_END + c0, MXU_COLS)
        gate = col(V_END + CONV_CH + c0, MXU_COLS)
        glu_ref[:, c0:c0 + MXU_COLS] = (val * jax.nn.sigmoid(gate)).astype(bf16)


def _rel_bucket_np(dist):
    n = np.maximum(dist, 0)
    max_exact = N_BUCKETS // 2
    nf = np.maximum(n, 1).astype(np.float32)
    scaled = (np.log(nf / np.float32(max_exact)) / np.float32(math.log(MAX_DISTANCE / max_exact))
              * np.float32(N_BUCKETS - max_exact))
    large = np.minimum(max_exact + scaled.astype(np.int32), N_BUCKETS - 1)
    return np.where(n < max_exact, n, large).astype(np.int32)


def _bucket_tables():
    a = np.arange(BLOCK)[:, None]
    m = np.arange(BLOCK)[None, :]
    kk = np.arange(2 * BLOCK)[None, :]
    dist_band = BLOCK + a - kk
    band_ok = (dist_band >= 0) & (dist_band < WINDOW)
    band = np.where(band_ok, _rel_bucket_np(dist_band), -1)
    real = np.full((2, BLOCK, KSEG), -1, np.int32)
    for t, n in enumerate((0, 1)):
        dist_meta = N_META + n * BLOCK + a - m
        real[t, :, :BLOCK] = np.where(m < N_META, _rel_bucket_np(dist_meta), -1)
        real[t, :, BLOCK:] = band
    real[0, :, BLOCK:2 * BLOCK] = -1
    for n in range(2, SEQ // BLOCK):
        dist_meta = N_META + n * BLOCK + a - m
        assert np.array_equal(np.where(m < N_META, _rel_bucket_np(dist_meta), -1), real[1, :, :BLOCK])
    r = np.arange(N_META_ROWS)
    same = (r[:, None] // N_META) == (r[None, :] // N_META)
    dmm = (r[:, None] % N_META) - (r[None, :] % N_META)
    meta = np.where(same & (dmm >= 0), _rel_bucket_np(dmm), -1).astype(np.int32)
    return real, meta


_BK_REAL, _BK_META = _bucket_tables()


def _bias_from_buckets(bk, rb_ref, h):
    acc = jnp.full(bk.shape, NEG, f32)
    for bkt in range(N_BUCKETS):
        acc = jnp.where(bk == bkt, rb_ref[bkt, h], acc)
    return acc


def _expand_kv(x2):
    xf = x2.astype(f32)
    r = pltpu.roll(xf, HEAD_DIM, axis=1)
    low = lax.broadcasted_iota(i32, xf.shape, 1) < HEAD_DIM
    h0 = jnp.where(low, xf, r).astype(bf16)
    h1 = jnp.where(low, r, xf).astype(bf16)
    return (jnp.concatenate([h0, h0], axis=1), jnp.concatenate([h1, h1], axis=1))


def _block_diag(x, n_keys):
    lane_blk = lax.broadcasted_iota(i32, (n_keys, GROUP * HEAD_DIM), 1) // HEAD_DIM
    zero = jnp.zeros_like(x)
    return jnp.concatenate([jnp.where(lane_blk == h, x, zero) for h in range(GROUP)], axis=0)


def _softmax_block(sh, sink):
    mx = jnp.maximum(jnp.max(sh, -1, keepdims=True), sink)
    p = jnp.exp(sh - mx)
    den = jnp.sum(p, -1, keepdims=True) + jnp.exp(sink - mx)
    return p.astype(bf16), 1.0 / den


def _head_scale(rs, m_rows):
    ol = lax.broadcasted_iota(i32, (m_rows, GROUP * HEAD_DIM), 1) // HEAD_DIM
    return jnp.where(ol == 0, rs[0], jnp.where(ol == 1, rs[1], jnp.where(ol == 2, rs[2], rs[3])))


def _attend(qg, kx, vx, bias_of_head, sink_of_head, n_keys):
    kbd = _block_diag(kx, n_keys)
    vbd = _block_diag(vx, n_keys)
    s = lax.dot_general(qg, kbd, (((1,), (1,)), ((), ())), preferred_element_type=f32)
    ps, rs = [], []
    for h in range(GROUP):
        p, r = _softmax_block(s[:, h * n_keys:(h + 1) * n_keys] + bias_of_head(h), sink_of_head(h))
        ps.append(p)
        rs.append(r)
    o = jnp.dot(jnp.concatenate(ps, axis=1), vbd, preferred_element_type=f32)
    return o * _head_scale(rs, qg.shape[0])


def _attend_tile(qg, kbd, vbd, kbd_meta, vbd_meta, bias_of, sink_of_head):
    nt = (((1,), (1,)), ((), ()))
    nb = BLOCKS_PER_TILE
    s_meta = lax.dot_general(qg, kbd_meta, nt, preferred_element_type=f32)
    s_prev, s_cur = [None] * nb, [None] * nb
    for b in range(nb + 1):
        lo, hi = max(b - 1, 0) * BLOCK, min(b + 1, nb) * BLOCK
        sc = lax.dot_general(qg[lo:hi], kbd[b], nt, preferred_element_type=f32)
        if b > 0:
            s_cur[b - 1] = sc[:BLOCK]
        if b < nb:
            s_prev[b] = sc[-BLOCK:]
    pm, pp, pc, scales = [], [], [], []
    for blk in range(nb):
        rows = slice(blk * BLOCK, (blk + 1) * BLOCK)
        pms, pps, pcs, rs = [], [], [], []
        for h in range(GROUP):
            cols = slice(h * BLOCK, (h + 1) * BLOCK)
            sh = jnp.concatenate([s_meta[rows, cols], s_prev[blk][:, cols], s_cur[blk][:, cols]], axis=1)
            p, r = _softmax_block(sh + bias_of(blk, h), sink_of_head(h))
            pms.append(p[:, :BLOCK])
            pps.append(p[:, BLOCK:2 * BLOCK])
            pcs.append(p[:, 2 * BLOCK:])
            rs.append(r)
        pm.append(jnp.concatenate(pms, axis=1))
        pp.append(jnp.concatenate(pps, axis=1))
        pc.append(jnp.concatenate(pcs, axis=1))
        scales.append(_head_scale(rs, BLOCK))
    o = [None] * nb
    for b in range(nb + 1):
        parts = ([pc[b - 1]] if b > 0 else []) + ([pp[b]] if b < nb else [])
        ob = jnp.dot(jnp.concatenate(parts, axis=0), vbd[b], preferred_element_type=f32)
        if b > 0:
            o[b - 1] = o[b - 1] + ob[:BLOCK]
        if b < nb:
            o[b] = ob[-BLOCK:]
    om = jnp.dot(jnp.concatenate(pm, axis=0), vbd_meta, preferred_element_type=f32)
    return (jnp.concatenate(o, axis=0) + om) * jnp.concatenate(scales, axis=0)


def _bias_kernel(rb_ref, bkr_ref, bkm_ref, br_ref, bm_ref):
    for h in range(N_Q_HEADS):
        for t in range(2):
            br_ref[t, h] = _bias_from_buckets(bkr_ref[t], rb_ref, h)
        bm_ref[h] = _bias_from_buckets(bkm_ref[...], rb_ref, h)


def _bias_tables(rel_bias):
    return pl.pallas_call(
        _bias_kernel,
        in_specs=[_smem(), pl.BlockSpec(memory_space=pltpu.VMEM), pl.BlockSpec(memory_space=pltpu.VMEM)],
        out_specs=[pl.BlockSpec(memory_space=pltpu.VMEM), pl.BlockSpec(memory_space=pltpu.VMEM)],
        out_shape=[
            jax.ShapeDtypeStruct((2, N_Q_HEADS, BLOCK, KSEG), f32),
            jax.ShapeDtypeStruct((N_Q_HEADS, N_META_ROWS, N_META_ROWS), f32),
        ],
        name="rel_bias_tables",
    )(rel_bias, jnp.asarray(_BK_REAL), jnp.asarray(_BK_META))


def _attn_kernel(layer, sinks_ref, q_ref, k_ref, v_ref, kp_ref, vp_ref, km_ref, vm_ref,
                 br_ref, bm_ref, o_ref):
    s = pl.program_id(0)
    gw = GROUP * HEAD_DIM
    sink = lambda head: sinks_ref[layer, head]

    @pl.when(s == 0)
    def _():
        kxs = _expand_kv(k_ref[:N_META_ROWS, :])
        vxs = _expand_kv(v_ref[:N_META_ROWS, :])
        for g in range(N_KV_HEADS):
            o = _attend(q_ref[:N_META_ROWS, g * gw:(g + 1) * gw], kxs[g], vxs[g],
                        lambda h, g=g: bm_ref[g * GROUP + h],
                        lambda h, g=g: sink(g * GROUP + h), N_META_ROWS)
            o_ref[:N_META_ROWS, g * gw:(g + 1) * gw] = o.astype(bf16)
        o_ref[N_META_ROWS:, :] = jnp.zeros((TM - N_META_ROWS, ATTN_WIDTH), bf16)

    @pl.when(s > 0)
    def _():
        first = ((s - 1) % TILES_PER_BATCH) == 0
        tbl0 = jnp.where(first, 0, 1)
        zpad = jnp.zeros((BLOCK - N_META, KV_WIDTH), bf16)
        kms = _expand_kv(jnp.concatenate([km_ref[...], zpad], axis=0))
        vms = _expand_kv(jnp.concatenate([vm_ref[...], zpad], axis=0))
        kxs = [_expand_kv(kp_ref[...])] + [_expand_kv(k_ref[b * BLOCK:(b + 1) * BLOCK, :])
                                          for b in range(BLOCKS_PER_TILE)]
        vxs = [_expand_kv(vp_ref[...])] + [_expand_kv(v_ref[b * BLOCK:(b + 1) * BLOCK, :])
                                          for b in range(BLOCKS_PER_TILE)]
        for g in range(N_KV_HEADS):
            o = _attend_tile(
                q_ref[:, g * gw:(g + 1) * gw],
                [_block_diag(x[g], BLOCK) for x in kxs], [_block_diag(x[g], BLOCK) for x in vxs],
                _block_diag(kms[g], BLOCK), _block_diag(vms[g], BLOCK),
                lambda blk, h, g=g: br_ref[tbl0 if blk == 0 else 1, g * GROUP + h],
                lambda h, g=g: sink(g * GROUP + h))
            o_ref[:, g * gw:(g + 1) * gw] = o.astype(bf16)


def _inproj_attn_kernel(embed, layer, *refs):
    n_in = 5 if embed else 2
    sinks_ref, br_ref, bm_ref = refs[n_in:n_in + 3]
    rest = refs[n_in + 3:]
    if embed:
        a_ref, glu_ref, h_ref, wb_ref, q_s, k_s, v_s, kp_s, vp_s, km_s, vm_s = rest
        _inproj_kernel(embed, *refs[:n_in], q_s, k_s, v_s, glu_ref, h_ref, wb_ref)
    else:
        a_ref, glu_ref, wb_ref, q_s, k_s, v_s, kp_s, vp_s, km_s, vm_s = rest
        _inproj_kernel(embed, *refs[:n_in], q_s, k_s, v_s, glu_ref, wb_ref)

    s = pl.program_id(0)
    batch = jnp.maximum(s - 1, 0) // TILES_PER_BATCH
    meta_rows = pl.ds(pl.multiple_of(batch * N_META, N_META), N_META)
    _attn_kernel(layer, sinks_ref, q_s, k_s, v_s, kp_s, vp_s,
                 km_s.at[meta_rows, :], vm_s.at[meta_rows, :], br_ref, bm_ref, a_ref)

    kp_s[...] = k_s[TM - BLOCK:, :]
    vp_s[...] = v_s[TM - BLOCK:, :]

    @pl.when(s == 0)
    def _():
        km_s[...] = k_s[:N_META_ROWS, :]
        vm_s[...] = v_s[:N_META_ROWS, :]


def _inproj_attention(layer, w_in, sinks, bias_real, bias_meta, h=None, x2d=None, meta=None, eg=None, eb=None):
    embed = h is None
    row = lambda w: pl.BlockSpec((TM, w), lambda s: (s, 0))
    w_spec = pl.BlockSpec((None, D_MODEL, GLU_END), lambda s: (layer, 0, 0), pipeline_mode=pl.Buffered(1))
    shared = [_smem(), _full((2, N_Q_HEADS, BLOCK, KSEG)), _full((N_Q_HEADS, N_META_ROWS, N_META_ROWS))]
    out_specs = [row(ATTN_WIDTH), row(CONV_CH)]
    out_shape = [jax.ShapeDtypeStruct((NT, ATTN_WIDTH), bf16), jax.ShapeDtypeStruct((NT, CONV_CH), bf16)]
    if embed:
        in_specs = [
            pl.BlockSpec((TM, D_MODEL), lambda s: (jnp.maximum(s - 1, 0), 0)),
            _full((N_META, D_MODEL)), _full((1, D_MODEL)), _full((1, D_MODEL)), w_spec,
        ]
        args = (x2d, meta, eg, eb, w_in)
        out_specs.append(row(D_MODEL))
        out_shape.append(jax.ShapeDtypeStruct((NT, D_MODEL), f32))
    else:
        in_specs = [row(D_MODEL), w_spec]
        args = (h, w_in)
    return pl.pallas_call(
        functools.partial(_inproj_attn_kernel, embed, layer),
        grid=(N_TILES,),
        in_specs=in_specs + shared,
        out_specs=out_specs,
        out_shape=out_shape,
        scratch_shapes=[
            pltpu.VMEM((D_MODEL, GLU_END), bf16),
            pltpu.VMEM((TM, ATTN_WIDTH), bf16), pltpu.VMEM((TM, KV_WIDTH), bf16), pltpu.VMEM((TM, KV_WIDTH), bf16),
            pltpu.VMEM((BLOCK, KV_WIDTH), bf16), pltpu.VMEM((BLOCK, KV_WIDTH), bf16),
            pltpu.VMEM((N_META_ROWS, KV_WIDTH), bf16), pltpu.VMEM((N_META_ROWS, KV_WIDTH), bf16),
        ],
        compiler_params=_cparams("arbitrary"),
        name="inproj_attention",
    )(*args, sinks, bias_real, bias_meta)


SPREAD = TM // BATCH
SH_ROWS = CONV_HALO + TM - SUBLANES
CONV_ROWS = 128
GATE_BLOCKS = 2 * D_MODEL // MXU_COLS


def _conv_window(s, glu_ref, gprev_ref, gmeta_ref, win_ref, sh_ref):
    is_meta = s == 0
    first = ((s - 1) % TILES_PER_BATCH) == 0
    glu = glu_ref[...].astype(f32)
    zgap = jnp.zeros((SPREAD - N_META, CONV_CH), f32)
    spread = jnp.concatenate(
        [p for bb in range(BATCH) for p in (zgap, glu[bb * N_META:(bb + 1) * N_META])], axis=0)
    win_ref[CONV_HALO:, :] = jnp.where(is_meta, spread, glu)
    halo_first = jnp.concatenate(
        [jnp.zeros((CONV_HALO - N_META, CONV_CH), f32), gmeta_ref[...].astype(f32)], axis=0)
    halo = jnp.where(first, halo_first, gprev_ref[...].astype(f32))
    win_ref[:CONV_HALO, :] = jnp.where(is_meta, 0.0, halo)

    for r in range(1, SUBLANES):
        sh_ref[r - 1] = win_ref[r:r + SH_ROWS, :]


def _conv_finish(s, layer, dw_ref, cb_ref, cg_ref, cbeta_ref, win_ref, sh_ref, cv_ref):
    off = CONV_HALO - (CONV_WIDTH - 1)
    for rc in range(TM // CONV_ROWS):
        for lc in range(CONV_CH // LANES):
            ls = slice(lc * LANES, (lc + 1) * LANES)
            acc = None
            for t in range(CONV_WIDTH):
                r = (off + t) % SUBLANES
                r0 = rc * CONV_ROWS + (off + t) - r
                x = win_ref[r0:r0 + CONV_ROWS, ls] if r == 0 else sh_ref[r - 1, r0:r0 + CONV_ROWS, ls]
                term = dw_ref[t:t + 1, ls] * x
                acc = term if acc is None else acc + term
            cv_ref[rc * CONV_ROWS:(rc + 1) * CONV_ROWS, ls] = acc
    y = _ln(cv_ref[...] + _layer_row(cb_ref, layer), _layer_row(cg_ref, layer), _layer_row(cbeta_ref, layer))
    c = y * jax.nn.sigmoid(y)
    gathered = jnp.concatenate(
        [c[bb * SPREAD + SPREAD - N_META:(bb + 1) * SPREAD] for bb in range(BATCH)]
        + [jnp.zeros((TM - N_META_ROWS, CONV_CH), f32)], axis=0)
    return jnp.where(s == 0, gathered, c).astype(bf16)


def _mix_kernel(layer, h_ref, a_ref, glu_ref, gprev_ref, gmeta_ref, dw_ref, cb_ref, cg_ref, cbeta_ref, *rest):
    wg_refs = rest[:GATE_BLOCKS]
    (wap_ref, wcp_ref, wo_ref, g_ref, b_ref, o_ref,
     win_ref, sh_ref, cv_ref, wgb_ref, wapb_ref, wcpb_ref, wob_ref) = rest[GATE_BLOCKS:]
    s = pl.program_id(0)

    @pl.when(s == 0)
    def _():
        for k in range(GATE_BLOCKS):
            wgb_ref[:, k * MXU_COLS:(k + 1) * MXU_COLS] = wg_refs[k][...].astype(bf16)
        wapb_ref[...] = wap_ref[...].astype(bf16)
        wcpb_ref[...] = wcp_ref[...].astype(bf16)
        wob_ref[...] = wo_ref[...].astype(bf16)

    h = h_ref[...]
    gates = jnp.dot(h.astype(bf16), wgb_ref[...], preferred_element_type=f32)
    ya = jnp.dot(a_ref[...], wapb_ref[...], preferred_element_type=f32)
    _conv_window(s, glu_ref, gprev_ref, gmeta_ref, win_ref, sh_ref)
    c = _conv_finish(s, layer, dw_ref, cb_ref, cg_ref, cbeta_ref, win_ref, sh_ref, cv_ref)
    yc = jnp.dot(c, wcpb_ref[...], preferred_element_type=f32)
    m = jax.nn.sigmoid(gates[:, :D_MODEL]) * ya + jax.nn.sigmoid(gates[:, D_MODEL:]) * yc
    mix = jnp.dot(m.astype(bf16), wob_ref[...], preferred_element_type=f32)
    o_ref[...] = _ln(ALPHA * h + mix, _layer_row(g_ref, layer), _layer_row(b_ref, layer))


def _mix(layer, h, a, glu, dw, cb, cg, cbeta, w_in, wap, wcp, wo, g, b):
    row = lambda w: pl.BlockSpec((TM, w), lambda s: (s, 0))
    gate_block = lambda k: pl.BlockSpec((None, D_MODEL, MXU_COLS),
                                        lambda s: (layer, 0, GLU_END // MXU_COLS + k),
                                        pipeline_mode=pl.Buffered(1))
    gprev = pl.BlockSpec((CONV_HALO, CONV_CH),
                         lambda s: (jnp.maximum(s * (TM // CONV_HALO) - 1, 0), 0))
    gmeta = pl.BlockSpec((N_META, CONV_CH), lambda s: (jnp.maximum(s - 1, 0) // TILES_PER_BATCH, 0))
    per_layer = lambda r, c: pl.BlockSpec((None, r, c), lambda s: (layer, 0, 0), pipeline_mode=pl.Buffered(1))
    conv_vec = _full((DEPTH, CONV_CH))
    model_vec = _full((DEPTH, D_MODEL))
    return pl.pallas_call(
        functools.partial(_mix_kernel, layer),
        grid=(N_TILES,),
        in_specs=[
            row(D_MODEL), row(ATTN_WIDTH), row(CONV_CH), gprev, gmeta,
            pl.BlockSpec((None, CONV_WIDTH, CONV_CH), lambda s: (layer, 0, 0)), conv_vec, conv_vec, conv_vec,
            *[gate_block(k) for k in range(GATE_BLOCKS)], per_layer(ATTN_WIDTH, D_MODEL),
            per_layer(CONV_CH, D_MODEL), per_layer(D_MODEL, D_MODEL),
            model_vec, model_vec,
        ],
        out_specs=row(D_MODEL),
        out_shape=jax.ShapeDtypeStruct((NT, D_MODEL), f32),
        scratch_shapes=[
            pltpu.VMEM((CONV_HALO + TM, CONV_CH), f32),
            pltpu.VMEM((7, SH_ROWS, CONV_CH), f32),
            pltpu.VMEM((TM, CONV_CH), f32),
            pltpu.VMEM((D_MODEL, 2 * D_MODEL), bf16),
            pltpu.VMEM((ATTN_WIDTH, D_MODEL), bf16),
            pltpu.VMEM((CONV_CH, D_MODEL), bf16),
            pltpu.VMEM((D_MODEL, D_MODEL), bf16),
        ],
        compiler_params=_cparams("arbitrary"),
        name="conv_mix_ln1",
    )(h, a, glu, glu, glu, dw, cb, cg, cbeta, *([w_in] * GATE_BLOCKS), wap, wcp, wo, g, b)


def _swiglu_chunks(xb, wg_ref, wu_ref, wd_ref, chunks, row_scale=None):
    acc = None
    c0 = 0
    for w in chunks:
        gt = jnp.dot(xb, wg_ref[:, c0:c0 + w].astype(bf16), preferred_element_type=f32)
        up = jnp.dot(xb, wu_ref[:, c0:c0 + w].astype(bf16), preferred_element_type=f32)
        act = gt * jax.nn.sigmoid(gt) * up
        if row_scale is not None:
            act = act * row_scale
        part = jnp.dot(act.astype(bf16), wd_ref[c0:c0 + w, :].astype(bf16), preferred_element_type=f32)
        acc = part if acc is None else acc + part
        c0 += w
    return acc


def _ffn_kernel(layer, h_ref, wg_ref, wu_ref, wd_ref, g_ref, b_ref, o_ref):
    h = h_ref[...]
    f = _swiglu_chunks(h.astype(bf16), wg_ref, wu_ref, wd_ref, FF_CHUNKS_DENSE)
    o_ref[...] = _ln(ALPHA * h + f, _layer_row(g_ref, layer), _layer_row(b_ref, layer))


def _dense_ffn(layer, j, h, wg, wu, wd, g, b):
    row = pl.BlockSpec((TM, D_MODEL), lambda s: (s, 0))
    per_layer = lambda r, c: pl.BlockSpec((None, r, c), lambda s: (j, 0, 0), pipeline_mode=pl.Buffered(1))
    return pl.pallas_call(
        functools.partial(_ffn_kernel, layer),
        grid=(N_TILES,),
        in_specs=[row, per_layer(D_MODEL, D_FF), per_layer(D_MODEL, D_FF), per_layer(D_FF, D_MODEL),
                  _full((DEPTH, D_MODEL)), _full((DEPTH, D_MODEL))],
        out_specs=row,
        out_shape=jax.ShapeDtypeStruct((NT, D_MODEL), f32),
        compiler_params=_cparams("parallel"),
        name="dense_ffn_ln2",
    )(h, wg, wu, wd, g, b)


def _route_copy(src_ref, xs_ref, sem, e, row):
    return pltpu.make_async_copy(
        src_ref.at[pl.ds(e * CHUNK, CHUNK), :],
        xs_ref.at[pl.ds(pl.multiple_of(e * CAP + row, ALIGN), CHUNK), :],
        sem)


def _zero_copy(zbuf_ref, xs_ref, sem, e, row):
    return pltpu.make_async_copy(
        zbuf_ref,
        xs_ref.at[pl.ds(pl.multiple_of(e * CAP + row, ALIGN), TM), :],
        sem)


def _dispatch_pass(p, ranki, sel, gate, hi, dst_ref):
    slot_id = lax.broadcasted_iota(i32, (CHUNK, TM), 0) + p * CHUNK
    lane = lax.broadcasted_iota(i32, (CHUNK, LANES), 1)
    ps = []
    for e in range(N_EXPERTS):
        pe = (slot_id == ranki[e:e + 1, :]) & sel[e:e + 1, :]
        ps.append(jnp.where(pe, 1.0, 0.0).astype(bf16))
        gs = jnp.sum(jnp.where(pe, gate[e:e + 1, :], 0.0), axis=1, keepdims=True)
        g_hi = gs.astype(bf16).astype(f32)
        g_mid = (gs - g_hi).astype(bf16).astype(f32)
        g_lo = gs - g_hi - g_mid
        parts = jnp.where(lane == 0, g_hi, jnp.where(lane == 1, g_mid, jnp.where(lane == 2, g_lo, 0.0)))
        dst_ref[e * CHUNK:(e + 1) * CHUNK, D_MODEL:] = parts.astype(bf16)
    pall = jnp.concatenate(ps, axis=0)
    dst_ref[:, :D_MODEL] = jnp.dot(pall, hi, preferred_element_type=f32).astype(bf16)


def _route_kernel(h_ref, r_ref, info_ref, base_ref, rows_ref, te_ref, tb_ref, tv_ref, xs_ref,
                  xbuf_ref, xov_ref, zbuf_ref, cnt_ref, sem_ref):
    s = pl.program_id(0)
    slot = s % 2

    @pl.when(s == 0)
    def _():
        for e in range(N_EXPERTS):
            cnt_ref[e] = 0
        zbuf_ref[...] = jnp.zeros((TM, XS_W), bf16)

    h = h_ref[...]
    hi = h.astype(bf16)
    lo = (h - hi.astype(f32)).astype(bf16)
    pa = jnp.dot(hi, r_ref[...], preferred_element_type=f32)
    pb = jnp.dot(lo, r_ref[...], preferred_element_type=f32)
    logits = pa + pltpu.roll(pa, LANES - N_EXPERTS, axis=1) + pb
    lt = logits.T[:N_EXPERTS, :]

    eid = lax.broadcasted_iota(i32, (N_EXPERTS, TM), 0)
    m1 = jnp.max(lt, axis=0, keepdims=True)
    i1 = jnp.min(jnp.where(lt == m1, eid, N_EXPERTS), axis=0, keepdims=True)
    lt2 = jnp.where(eid == i1, NEG, lt)
    m2 = jnp.max(lt2, axis=0, keepdims=True)
    i2 = jnp.min(jnp.where(lt2 == m2, eid, N_EXPERTS), axis=0, keepdims=True)
    ex = jnp.exp(m2 - m1)
    g1 = 1.0 / (1.0 + ex)
    g2 = ex * g1
    tok = lax.broadcasted_iota(i32, (1, TM), 1)
    valid = (s > 0) | (tok < N_META_ROWS)
    sel1 = (eid == i1) & valid
    sel2 = (eid == i2) & valid
    sel = sel1 | sel2
    onehot = jnp.where(sel, 1.0, 0.0)
    tri = jnp.where(lax.broadcasted_iota(i32, (TM, TM), 0) < lax.broadcasted_iota(i32, (TM, TM), 1), 1.0, 0.0)
    rank = jnp.dot(onehot, tri, preferred_element_type=f32)
    gate = jnp.where(sel1, g1, jnp.where(sel2, g2, 0.0))
    rank1 = jnp.sum(jnp.where(sel1, rank, 0.0), axis=0, keepdims=True)
    rank2 = jnp.sum(jnp.where(sel2, rank, 0.0), axis=0, keepdims=True)
    info = jnp.concatenate([i1.astype(f32), i2.astype(f32), rank1, rank2,
                            jnp.zeros((LANES - 4, TM), f32)], axis=0)
    info_ref[...] = info.T
    ranki = rank.astype(i32)

    _dispatch_pass(0, ranki, sel, gate, hi, xbuf_ref.at[slot])

    @pl.when(s > 0)
    def _():
        for e in range(N_EXPERTS):
            _route_copy(xbuf_ref.at[1 - slot], xs_ref, sem_ref.at[1 - slot, e], e, 0).wait()

    bases, counts = [], []
    for e in range(N_EXPERTS):
        base = cnt_ref[e]
        n_e = jnp.sum(onehot[e:e + 1, :]).astype(i32)
        bases.append(base)
        counts.append(n_e)
        base_ref[s, e] = base
        _route_copy(xbuf_ref.at[slot], xs_ref, sem_ref.at[slot, e], e, base).start()
        cnt_ref[e] = base + ((n_e + (ALIGN - 1)) // ALIGN) * ALIGN
    most = functools.reduce(jnp.maximum, counts)

    for p in range(1, N_PASS):
        @pl.when(most > p * CHUNK)
        def _(p=p):
            _dispatch_pass(p, ranki, sel, gate, hi, xov_ref)
            for e in range(N_EXPERTS):
                _route_copy(xov_ref, xs_ref, sem_ref.at[2, e], e, bases[e] + p * CHUNK).start()
            for e in range(N_EXPERTS):
                _route_copy(xov_ref, xs_ref, sem_ref.at[2, e], e, 0).wait()

    @pl.when(s == N_TILES - 1)
    def _():
        for e in range(N_EXPERTS):
            _route_copy(xbuf_ref.at[slot], xs_ref, sem_ref.at[slot, e], e, 0).wait()
        used = [cnt_ref[e] for e in range(N_EXPERTS)]
        for e in range(N_EXPERTS):
            _zero_copy(zbuf_ref, xs_ref, sem_ref.at[2, e], e, used[e]).start()
        starts, ends, total = [], [], 0
        for e in range(N_EXPERTS):
            n_t = jnp.maximum(lax.shift_right_logical(used[e] + (TM - 1), TM.bit_length() - 1), 1)
            rows_ref[e] = n_t * TM
            starts.append(total)
            total = total + n_t
            ends.append(total)

        def tile_entry(t):
            e_t = sum(jnp.where(t >= ends[e], 1, 0) for e in range(N_EXPERTS - 1))
            first = sum(jnp.where(e_t == e, starts[e], 0) for e in range(N_EXPERTS))
            return e_t, e_t * (CAP // TM) + (t - first)

        e_last, blk_last = tile_entry(total - 1)
        for t in range(FFN_GRID):
            e_t, blk_t = tile_entry(t)
            live = t < total
            te_ref[t] = jnp.where(live, e_t, e_last)
            tb_ref[t] = jnp.where(live, blk_t, blk_last)
            tv_ref[t] = jnp.where(live, 1, 0)
        for e in range(N_EXPERTS):
            _zero_copy(zbuf_ref, xs_ref, sem_ref.at[2, e], e, 0).wait()


def _route(h, rsplit):
    return pl.pallas_call(
        _route_kernel,
        grid=(N_TILES,),
        in_specs=[pl.BlockSpec((TM, D_MODEL), lambda s: (s, 0)), _full((D_MODEL, LANES))],
        out_specs=[
            pl.BlockSpec((TM, LANES), lambda s: (s, 0)),
            _smem(), _smem(), _smem(), _smem(), _smem(),
            pl.BlockSpec(memory_space=pl.ANY),
        ],
        out_shape=[
            jax.ShapeDtypeStruct((NT, LANES), f32),
            jax.ShapeDtypeStruct((N_TILES, N_EXPERTS), i32),
            jax.ShapeDtypeStruct((N_EXPERTS,), i32),
            jax.ShapeDtypeStruct((FFN_GRID,), i32),
            jax.ShapeDtypeStruct((FFN_GRID,), i32),
            jax.ShapeDtypeStruct((FFN_GRID,), i32),
            jax.ShapeDtypeStruct((N_EXPERTS * CAP, XS_W), bf16),
        ],
        scratch_shapes=[
            pltpu.VMEM((2, N_EXPERTS * CHUNK, XS_W), bf16),
            pltpu.VMEM((N_EXPERTS * CHUNK, XS_W), bf16),
            pltpu.VMEM((TM, XS_W), bf16),
            pltpu.SMEM((N_EXPERTS,), i32),
            pltpu.SemaphoreType.DMA((3, N_EXPERTS)),
        ],
        compiler_params=_cparams("arbitrary"),
        name="moe_route_dispatch",
    )(h, rsplit)


def _expert_kernel(te_ref, tb_ref, tv_ref, x_ref, wg_ref, wu_ref, wd_ref, y_ref):
    t = pl.program_id(0)

    @pl.when(tv_ref[t] > 0)
    def _():
        x = x_ref[...]
        gate = jnp.sum(x[:, D_MODEL:].astype(f32), axis=1, keepdims=True)
        y = _swiglu_chunks(x[:, :D_MODEL], wg_ref, wu_ref, wd_ref, FF_CHUNKS_MOE, row_scale=gate)
        y_ref[...] = y.astype(bf16)


def _expert_ffn(j, tile_e, tile_blk, tile_valid, xs, wg, wu, wd):
    per_expert = lambda r, c: pl.BlockSpec((None, None, r, c), lambda t, te, tb, tv: (j, te[t], 0, 0))
    grid_spec = pltpu.PrefetchScalarGridSpec(
        num_scalar_prefetch=3,
        grid=(FFN_GRID,),
        in_specs=[
            pl.BlockSpec((TM, XS_W), lambda t, te, tb, tv: (tb[t], 0)),
            per_expert(D_MODEL, D_EXPERT), per_expert(D_MODEL, D_EXPERT), per_expert(D_EXPERT, D_MODEL),
        ],
        out_specs=pl.BlockSpec((TM, D_MODEL), lambda t, te, tb, tv: (tb[t], 0)),
    )
    return pl.pallas_call(
        _expert_kernel,
        grid_spec=grid_spec,
        out_shape=jax.ShapeDtypeStruct((N_EXPERTS * CAP, D_MODEL), bf16),
        compiler_params=_cparams("arbitrary"),
        name="moe_expert_ffn",
    )(tile_e, tile_blk, tile_valid, xs, wg, wu, wd)


def _combine_copy(ys_ref, dst_ref, sem, e, row):
    return pltpu.make_async_copy(
        ys_ref.at[pl.ds(pl.multiple_of(e * CAP + row, ALIGN), CHUNK), :],
        dst_ref.at[pl.ds(e * CHUNK, CHUNK), :],
        sem)


def _window(base_ref, rows_ref, s, e, p):
    want = base_ref[s, e] + p * CHUNK
    start = jnp.minimum(want, rows_ref[e] - CHUNK)
    return start, want - start


def _onehot_cols(info, deltas, p):
    col = lax.broadcasted_iota(i32, (TM, N_EXPERTS * CHUNK), 1)
    hit = None
    for k in range(2):
        ek = info[:, k:k + 1].astype(i32)
        rk = info[:, 2 + k:3 + k].astype(i32) - p * CHUNK
        dk = functools.reduce(lambda a, b: a + b,
                              [jnp.where(ek == e, deltas[e], 0) for e in range(N_EXPERTS)])
        ck = jnp.where((rk >= 0) & (rk < CHUNK), ek * CHUNK + rk + dk, -1)
        hk = col == ck
        hit = hk if hit is None else hit | hk
    return jnp.where(hit, 1.0, 0.0).astype(bf16)


def _combine_kernel(layer, base_ref, rows_ref, h_ref, info_ref, ys_ref, g_ref, b_ref, o_ref,
                    ybuf_ref, yov_ref, f_ref, sem_ref):
    i = pl.program_id(0)
    n = pl.num_programs(0)
    s = i + 1
    slot = i % 2

    def start_fetch(tile, to_slot):
        for e in range(N_EXPERTS):
            start, _ = _window(base_ref, rows_ref, tile, e, 0)
            _combine_copy(ys_ref, ybuf_ref.at[to_slot], sem_ref.at[to_slot, e], e, start).start()

    @pl.when(i == 0)
    def _():
        start_fetch(s, slot)

    @pl.when(i + 1 < n)
    def _():
        start_fetch(s + 1, 1 - slot)

    info = info_ref[...]
    deltas = [_window(base_ref, rows_ref, s, e, 0)[1] for e in range(N_EXPERTS)]
    qm = _onehot_cols(info, deltas, 0)
    for e in range(N_EXPERTS):
        _combine_copy(ys_ref, ybuf_ref.at[slot], sem_ref.at[slot, e], e, 0).wait()
    f_ref[...] = jnp.dot(qm, ybuf_ref[slot], preferred_element_type=f32)

    deepest = jnp.max(jnp.maximum(info[:, 2:3], info[:, 3:4])).astype(i32)
    for p in range(1, N_PASS):
        @pl.when(deepest >= p * CHUNK)
        def _(p=p):
            wins = [_window(base_ref, rows_ref, s, e, p) for e in range(N_EXPERTS)]
            for e in range(N_EXPERTS):
                _combine_copy(ys_ref, yov_ref, sem_ref.at[2, e], e, wins[e][0]).start()
            qp = _onehot_cols(info, [w[1] for w in wins], p)
            for e in range(N_EXPERTS):
                _combine_copy(ys_ref, yov_ref, sem_ref.at[2, e], e, 0).wait()
            f_ref[...] += jnp.dot(qp, yov_ref[...], preferred_element_type=f32)

    o_ref[...] = _ln(ALPHA * h_ref[...] + f_ref[...], _layer_row(g_ref, layer), _layer_row(b_ref, layer))


def _combine(layer, base, rows, h, info, ys, g, b):
    n_out = N_TILES - 1
    grid_spec = pltpu.PrefetchScalarGridSpec(
        num_scalar_prefetch=2,
        grid=(n_out,),
        in_specs=[
            pl.BlockSpec((TM, D_MODEL), lambda i, base, rows: (i + 1, 0)),
            pl.BlockSpec((TM, LANES), lambda i, base, rows: (i + 1, 0)),
            pl.BlockSpec(memory_space=pl.ANY),
            pl.BlockSpec((DEPTH, D_MODEL), lambda i, base, rows: (0, 0)),
            pl.BlockSpec((DEPTH, D_MODEL), lambda i, base, rows: (0, 0)),
        ],
        out_specs=pl.BlockSpec((TM, D_MODEL), lambda i, base, rows: (i, 0)),
        scratch_shapes=[
            pltpu.VMEM((2, N_EXPERTS * CHUNK, D_MODEL), bf16),
            pltpu.VMEM((N_EXPERTS * CHUNK, D_MODEL), bf16),
            pltpu.VMEM((TM, D_MODEL), f32),
            pltpu.SemaphoreType.DMA((3, N_EXPERTS)),
        ],
    )
    return pl.pallas_call(
        functools.partial(_combine_kernel, layer),
        grid_spec=grid_spec,
        out_shape=jax.ShapeDtypeStruct((n_out * TM, D_MODEL), f32),
        compiler_params=_cparams("arbitrary"),
        name="moe_combine_ln2",
    )(base, rows, h, info, ys, g, b)


def _split2(x):
    hi = x.astype(bf16)
    lo = (x - hi.astype(f32)).astype(bf16)
    return hi, lo


def kernel(x, meta_tokens, emb_ln_g, emb_ln_b, rel_bias, w_in, conv_dw, conv_b, conv_ln_g, conv_ln_b, sinks, w_attn_proj, w_conv_proj, w_out, ln1_g, ln1_b, ffn_w_gate, ffn_w_up, ffn_w_down, router, moe_w_gate, moe_w_up, moe_w_down, ln2_g, ln2_b):
    row = lambda v: v.reshape(1, -1)
    h = None
    out = None
    bias_real, bias_meta = _bias_tables(rel_bias)
    for i in range(DEPTH):
        if i == 0:
            a, glu, h = _inproj_attention(i, w_in, sinks, bias_real, bias_meta,
                                          x2d=x.reshape(BATCH * SEQ, D_MODEL), meta=meta_tokens,
                                          eg=row(emb_ln_g), eb=row(emb_ln_b))
        else:
            a, glu = _inproj_attention(i, w_in, sinks, bias_real, bias_meta, h=h)
        h = _mix(i, h, a, glu, conv_dw, conv_b, conv_ln_g, conv_ln_b,
                 w_in, w_attn_proj, w_conv_proj, w_out, ln1_g, ln1_b)
        j = i // 2
        if i % 2 == 0:
            h = _dense_ffn(i, j, h, ffn_w_gate, ffn_w_up, ffn_w_down, ln2_g, ln2_b)
        else:
            r_hi, r_lo = _split2(router[j])
            rsplit = jnp.concatenate(
                [r_hi, r_lo, jnp.zeros((D_MODEL, LANES - 2 * N_EXPERTS), bf16)], axis=1)
            info, base, rows, tile_e, tile_blk, tile_valid, xs = _route(h, rsplit)
            ys = _expert_ffn(j, tile_e, tile_blk, tile_valid, xs, moe_w_gate, moe_w_up, moe_w_down)
            out = _combine(i, base, rows, h, info, ys, ln2_g, ln2_b)
    return out.reshape(BATCH, SEQ, D_MODEL)
```
